```python
import math
import jax, jax.numpy as jnp
from jax import lax
import numpy as np

D_MODEL = 2048
BATCH = 8
SEQ = 2048
DEPTH = 1

EPS = 1e-5
SSD_HEAD_DIM = 64
SSD_HEADS = D_MODEL // SSD_HEAD_DIM
D_SSD = SSD_HEADS * SSD_HEAD_DIM
SSD_GROUPS = 4
SSD_HEADS_PER_GROUP = SSD_HEADS // SSD_GROUPS
SSD_STATE = 128
SSD_CONV = 4
SSD_CHUNK = 128
SSD_CONV_DIM = D_SSD + 2 * SSD_GROUPS * SSD_STATE
ML_HEADS = 8
ML_DV = D_MODEL // ML_HEADS
ML_DQK = ML_DV // 2
D_ML = ML_HEADS * ML_DV
ML_CHUNK = 128
D_MIX = D_SSD + D_ML
SSD_PROJ = D_SSD + SSD_CONV_DIM + SSD_HEADS
ML_PROJ = 2 * ML_HEADS * ML_DQK + 2 * D_ML + 2 * ML_HEADS
D_PROJ = SSD_PROJ + ML_PROJ
N_EXPERTS = 32
TOP_K = 4
D_FF = D_MODEL
SWIGLU_ALPHA = 1.702
SWIGLU_LIMIT = 7.0
EXPERT_BLOCK = 256

kernel_name = 'hymba_ssd_mlstm_moe_layer'


def rmsnorm(x, w):
    xf = x.astype(jnp.float32)
    y = xf * lax.rsqrt(jnp.mean(xf * xf, axis=-1, keepdims=True) + EPS)
    return y.astype(x.dtype) * w


def grouped_rmsnorm(x, w, groups):
    sh = x.shape
    xg = x.reshape(*sh[:-1], groups, sh[-1] // groups)
    xg = xg * lax.rsqrt(jnp.mean(xg * xg, axis=-1, keepdims=True) + EPS)
    return xg.reshape(sh) * w


def split_columns(x, sizes):
    out, start = [], 0
    for size in sizes:
        out.append(x[..., start:start + size])
        start += size
    return out


def causal_depthwise_conv(x, w, b):
    s = x.shape[1]
    k = w.shape[0]
    xp = jnp.pad(x, ((0, 0), (k - 1, 0), (0, 0)))
    out = b
    for j in range(k):
        out = out + xp[:, j:j + s] * w[j]
    return out


def ssd_mixer(z, xbc, dt_raw, conv_w, conv_b, dt_bias, a_log, d_skip, norm_w):
    f32 = jnp.float32
    b, s, _ = z.shape
    G, R, P, N, Q = SSD_GROUPS, SSD_HEADS_PER_GROUP, SSD_HEAD_DIM, SSD_STATE, SSD_CHUNK
    nc = s // Q
    xbc = jax.nn.silu(causal_depthwise_conv(xbc, conv_w, conv_b)).astype(f32)
    xs = xbc[..., :D_SSD].reshape(b, nc, Q, G, R, P)
    bm = xbc[..., D_SSD:D_SSD + G * N].reshape(b, nc, Q, G, N)
    cm = xbc[..., D_SSD + G * N:].reshape(b, nc, Q, G, N)
    dt = jax.nn.softplus(dt_raw.astype(f32) + dt_bias.astype(f32)).reshape(b, nc, Q, G, R)
    a = -jnp.exp(a_log.astype(f32)).reshape(G, R)
    da = (dt * a).transpose(0, 3, 4, 1, 2)
    a_cum = jnp.cumsum(da, axis=-1)
    xdt = xs * dt[..., None]
    causal = jnp.tril(jnp.ones((Q, Q), bool))
    seg = a_cum[..., :, None] - a_cum[..., None, :]
    decay_in = jnp.exp(jnp.where(causal, seg, -jnp.inf))
    cb = jnp.einsum('bclgn,bcsgn->bgcls', cm, bm)
    y_diag = jnp.einsum('bgrcls,bcsgrp->bclgrp', cb[:, :, None] * decay_in, xdt)
    decay_to_end = jnp.exp(a_cum[..., -1:] - a_cum).transpose(0, 3, 4, 1, 2)
    states = jnp.einsum('bclgn,bclgrp->bcgrpn', bm, xdt * decay_to_end[..., None])
    chunk_tot = a_cum[..., -1]
    cs = jnp.cumsum(chunk_tot, axis=-1)
    strict = jnp.tril(jnp.ones((nc, nc), bool), -1)
    decay_chunk = jnp.exp(jnp.where(strict, (cs - chunk_tot)[..., :, None] - cs[..., None, :], -jnp.inf))
    prev_states = jnp.einsum('bgrzc,bcgrpn->bzgrpn', decay_chunk, states)
    decay_from_start = jnp.exp(a_cum).transpose(0, 3, 4, 1, 2)
    y_off = jnp.einsum('bclgn,bcgrpn->bclgrp', cm, prev_states) * decay_from_start[..., None]
    y = y_diag + y_off + xs * d_skip.astype(f32).reshape(G, R)[:, :, None]
    y = y.reshape(b, s, D_SSD) * jax.nn.silu(z.astype(f32))
    y = grouped_rmsnorm(y, norm_w, SSD_GROUPS)
    return y.astype(z.dtype)


def mlstm_mixer(q_in, k_in, v_in, o_pre, i_pre, f_pre, i_bias, f_bias, norm_w):
    f32 = jnp.float32
    b, s, _ = q_in.shape
    H, L, DK, DV = ML_HEADS, ML_CHUNK, ML_DQK, ML_DV
    nc = s // L
    q = q_in.astype(f32).reshape(b, nc, L, H, DK) * (DK ** -0.5)
    k = k_in.astype(f32).reshape(b, nc, L, H, DK)
    v = v_in.astype(f32).reshape(b, nc, L, H, DV)
    log_i = (i_pre.astype(f32) + i_bias.astype(f32)).reshape(b, nc, L, H).transpose(0, 3, 1, 2)
    log_f = jax.nn.log_sigmoid(f_pre.astype(f32) + f_bias.astype(f32)).reshape(b, nc, L, H).transpose(0, 3, 1, 2)
    bcum = jnp.cumsum(log_f, axis=-1)
    btot = bcum[..., -1]
    w_end = btot[..., None] - bcum + log_i
    m_loc = jnp.max(w_end, axis=-1)
    k_end = k * jnp.exp(w_end - m_loc[..., None]).transpose(0, 2, 3, 1)[..., None]
    kv_chunk = jnp.einsum('bclhk,bclhv->bchkv', k_end, v)
    n_chunk = jnp.sum(k_end, axis=2)

    def chunk_step(carry, inp):
        c_state, n_state, m_state = carry
        kv, nk, mloc, tot = inp
        m_new = jnp.maximum(tot + m_state, mloc)
        a = jnp.exp(tot + m_state - m_new)
        g = jnp.exp(mloc - m_new)
        c_new = a[..., None, None] * c_state + g[..., None, None] * kv
        n_new = a[..., None] * n_state + g[..., None] * nk
        return (c_new, n_new, m_new), (c_state, n_state, m_state)

    init = (jnp.zeros((b, H, DK, DV), f32), jnp.zeros((b, H, DK), f32), jnp.zeros((b, H), f32))
    scan_in = (kv_chunk.transpose(1, 0, 2, 3, 4), n_chunk.transpose(1, 0, 2, 3),
               m_loc.transpose(2, 0, 1), btot.transpose(2, 0, 1))
    _, (c_prev, n_prev, m_prev) = lax.scan(chunk_step, init, scan_in)
    c_prev = c_prev.transpose(1, 0, 2, 3, 4)
    n_prev = n_prev.transpose(1, 0, 2, 3)
    m_prev = m_prev.transpose(1, 2, 0)
    causal = jnp.tril(jnp.ones((L, L), bool))
    d_log = jnp.where(causal, bcum[..., :, None] - bcum[..., None, :] + log_i[..., None, :], -jnp.inf)
    m_inter = bcum + m_prev[..., None]
    m_t = jnp.maximum(m_inter, jnp.max(d_log, axis=-1))
    s_mat = jnp.einsum('bclhk,bcshk->bhcls', q, k) * jnp.exp(d_log - m_t[..., None])
    inter_scale = jnp.exp(m_inter - m_t)
    num = (jnp.einsum('bhcls,bcshv->bclhv', s_mat, v)
           + jnp.einsum('bclhk,bchkv->bclhv', q, c_prev) * inter_scale.transpose(0, 2, 3, 1)[..., None])
    den = jnp.sum(s_mat, axis=-1) + jnp.einsum('bclhk,bchk->bhcl', q, n_prev) * inter_scale
    den = jnp.maximum(jnp.abs(den), jnp.exp(-m_t))
    h_tilde = (num / den.transpose(0, 2, 3, 1)[..., None]).reshape(b, s, D_ML)
    y = grouped_rmsnorm(h_tilde, norm_w, ML_HEADS) * jax.nn.sigmoid(o_pre.astype(f32))
    return y.astype(q_in.dtype)


def clamped_swiglu(h):
    glu = jnp.minimum(h[..., ::2], SWIGLU_LIMIT)
    lin = jnp.clip(h[..., 1::2], -SWIGLU_LIMIT, SWIGLU_LIMIT)
    return glu * jax.nn.sigmoid(SWIGLU_ALPHA * glu) * (lin + 1.0)


def moe_ffn(xn, router_w, router_b, w_gate_up, b_gate_up, w_down, b_down):
    b, s, d = xn.shape
    T = b * s
    xf = xn.reshape(T, d)
    logits = (xf @ router_w + router_b).astype(jnp.float32)
    top_vals, top_idx = lax.top_k(logits, TOP_K)
    gates = jax.nn.softmax(top_vals, axis=-1).astype(xn.dtype)
    n_assign = T * TOP_K
    flat_e = top_idx.reshape(-1).astype(jnp.int32)
    order = jnp.argsort(flat_e)
    sorted_e = flat_e[order]
    counts = jnp.zeros((N_EXPERTS,), jnp.int32).at[flat_e].add(1)
    starts = jnp.cumsum(counts) - counts
    padded = (counts + EXPERT_BLOCK - 1) // EXPERT_BLOCK * EXPERT_BLOCK
    padded_ends = jnp.cumsum(padded)
    padded_starts = padded_ends - padded
    rank = jnp.arange(n_assign, dtype=jnp.int32) - starts[sorted_e]
    dest = jnp.zeros((n_assign,), jnp.int32).at[order].set(padded_starts[sorted_e] + rank)
    n_rows = n_assign + N_EXPERTS * EXPERT_BLOCK
    n_blocks = n_rows // EXPERT_BLOCK
    row_tok = jnp.full((n_rows,), T, jnp.int32).at[dest].set(jnp.arange(n_assign, dtype=jnp.int32) // TOP_K)
    block_start = jnp.arange(n_blocks, dtype=jnp.int32) * EXPERT_BLOCK
    block_expert = jnp.minimum(jnp.searchsorted(padded_ends, block_start, side='right'), N_EXPERTS - 1)
    x_pad = jnp.concatenate([xf, jnp.zeros((1, d), xf.dtype)], axis=0)
    x_rows = x_pad[row_tok].reshape(n_blocks, EXPERT_BLOCK, d)

    def expert_block(args):
        xb, e = args
        h = xb @ w_gate_up[e] + b_gate_up[e]
        return clamped_swiglu(h) @ w_down[e] + b_down[e]

    y_rows = lax.map(expert_block, (x_rows, block_expert)).reshape(n_rows, d)
    y = jnp.sum(y_rows[dest].reshape(T, TOP_K, d) * gates[..., None], axis=1)
    return y.reshape(b, s, d)


def setup_inputs(seed: int = 0) -> dict:
    key = jax.random.key(seed)
    ks = jax.random.split(key, 24)
    f32 = jnp.float32
    L = DEPTH

    def nrm(k, shape, scale):
        return jax.random.normal(k, shape, f32) * scale

    x = jax.random.normal(ks[0], (BATCH, SEQ, D_MODEL), f32)
    norm1_w = 1.0 + nrm(ks[1], (L, D_MODEL), 0.02)
    w_in = nrm(ks[2], (L, D_MODEL, D_PROJ), D_MODEL ** -0.5)
    conv_w = nrm(ks[3], (L, SSD_CONV, SSD_CONV_DIM), SSD_CONV ** -0.5)
    conv_b = nrm(ks[4], (L, SSD_CONV_DIM), 0.02)
    dt0 = jnp.exp(jax.random.uniform(ks[5], (L, SSD_HEADS), f32, math.log(1e-3), math.log(1e-1)))
    dt_bias = dt0 + jnp.log(-jnp.expm1(-dt0))
    a_log = jnp.log(jax.random.uniform(ks[6], (L, SSD_HEADS), f32, 1.0, 16.0))
    d_skip = 1.0 + nrm(ks[7], (L, SSD_HEADS), 0.1)
    ssd_norm_w = 1.0 + nrm(ks[8], (L, D_SSD), 0.02)
    i_bias = nrm(ks[9], (L, ML_HEADS), 0.1)
    f_bias = jnp.linspace(3.0, 6.0, ML_HEADS, dtype=f32) + nrm(ks[10], (L, ML_HEADS), 0.1)
    ml_norm_w = 1.0 + nrm(ks[11], (L, D_ML), 0.02)
    w_out = nrm(ks[12], (L, D_MIX, D_MODEL), D_MIX ** -0.5)
    norm2_w = 1.0 + nrm(ks[13], (L, D_MODEL), 0.02)
    router_w = nrm(ks[14], (L, D_MODEL, N_EXPERTS), D_MODEL ** -0.5)
    router_b = nrm(ks[15], (L, N_EXPERTS), 0.01)
    w_gate_up = nrm(ks[16], (L, N_EXPERTS, D_MODEL, 2 * D_FF), D_MODEL ** -0.5)
    b_gate_up = nrm(ks[17], (L, N_EXPERTS, 2 * D_FF), 0.01)
    w_down = nrm(ks[18], (L, N_EXPERTS, D_FF, D_MODEL), D_FF ** -0.5)
    b_down = nrm(ks[19], (L, N_EXPERTS, D_MODEL), 0.01)
    final_norm_w = 1.0 + nrm(ks[20], (D_MODEL,), 0.02)
    return {'x': x, 'norm1_w': norm1_w, 'w_in': w_in, 'conv_w': conv_w, 'conv_b': conv_b,
            'dt_bias': dt_bias, 'a_log': a_log, 'd_skip': d_skip, 'ssd_norm_w': ssd_norm_w,
            'i_bias': i_bias, 'f_bias': f_bias, 'ml_norm_w': ml_norm_w, 'w_out': w_out,
            'norm2_w': norm2_w, 'router_w': router_w, 'router_b': router_b,
            'w_gate_up': w_gate_up, 'b_gate_up': b_gate_up, 'w_down': w_down, 'b_down': b_down,
            'final_norm_w': final_norm_w}


def reference(x, norm1_w, w_in, conv_w, conv_b, dt_bias, a_log, d_skip, ssd_norm_w,
              i_bias, f_bias, ml_norm_w, w_out, norm2_w, router_w, router_b,
              w_gate_up, b_gate_up, w_down, b_down, final_norm_w):
    h = x
    for l in range(DEPTH):
        n = rmsnorm(h, norm1_w[l])
        proj = n @ w_in[l]
        z, xbc, dt_raw, q, k, v, o_pre, i_pre, f_pre = split_columns(
            proj, (D_SSD, SSD_CONV_DIM, SSD_HEADS, ML_HEADS * ML_DQK, ML_HEADS * ML_DQK,
                   D_ML, D_ML, ML_HEADS, ML_HEADS))
        y_ssd = ssd_mixer(z, xbc, dt_raw, conv_w[l], conv_b[l], dt_bias[l], a_log[l],
                          d_skip[l], ssd_norm_w[l])
        y_ml = mlstm_mixer(q, k, v, o_pre, i_pre, f_pre, i_bias[l], f_bias[l], ml_norm_w[l])
        h = h + jnp.concatenate([y_ssd, y_ml], axis=-1) @ w_out[l]
        h = h + moe_ffn(rmsnorm(h, norm2_w[l]), router_w[l], router_b[l], w_gate_up[l],
                        b_gate_up[l], w_down[l], b_down[l])
    return rmsnorm(h, final_norm_w)
```

```python
import functools

import jax
import jax.numpy as jnp
from jax import lax
from jax.experimental import pallas as pl
from jax.experimental.pallas import tpu as pltpu

F32 = jnp.float32
BF16 = jnp.bfloat16
I32 = jnp.int32
U32 = jnp.uint32

LANES = 128
SUBLANES = 8
VMEM_LIMIT = 56 * 1024 * 1024

D_MODEL = 2048
EPS = 1e-5
SSD_HEAD_DIM = 64
SSD_HEADS = 32
D_SSD = 2048
SSD_GROUPS = 4
SSD_STATE = 128
SSD_CONV = 4
CHUNK = 128
ML_HEADS = 8
ML_DV = 256
ML_DQK = 128
D_ML = 2048
N_EXPERTS = 32
TOP_K = 4
D_FF = 2048
SWIGLU_ALPHA = 1.702
SWIGLU_LIMIT = 7.0

COL_Z, COL_X, COL_V, COL_O = 0, 2048, 4096, 6144
COL_BC, COL_Q, COL_K = 8192, 9216, 10240
D_BIG = 11264
LANE_DT, LANE_I, LANE_F = 0, 32, 40

IN_TM, IN_TN = 1024, 1024
OUT_TM = 512
ROUTE_TM = 2048
DISP_TT = 512
MOE_TM = 512
MOE_TF = 512
PERM_W = 256
COMB_TT = 256


def _cparams(sem, vmem=VMEM_LIMIT):
    return pltpu.CompilerParams(dimension_semantics=sem, vmem_limit_bytes=vmem)


def _dot(a, b):
    return jnp.dot(a, b, preferred_element_type=F32)


def _dot_exact(a, b):
    return jnp.dot(a, b, preferred_element_type=F32, precision=lax.Precision.HIGHEST)


def _split3_dot(x, e_bf16):
    x0 = x.astype(BF16)
    r1 = x - x0.astype(F32)
    x1 = r1.astype(BF16)
    x2 = (r1 - x1.astype(F32)).astype(BF16)
    return _dot(x0, e_bf16) + _dot(x1, e_bf16) + _dot(x2, e_bf16)


def _sigmoid(x):
    return 1.0 / (1.0 + jnp.exp(-x))


def _softplus(x):
    return jnp.maximum(x, 0.0) + jnp.log1p(jnp.exp(-jnp.abs(x)))


def _tril(n, strict=False):
    r = lax.broadcasted_iota(I32, (n, n), 0)
    c = lax.broadcasted_iota(I32, (n, n), 1)
    return (r > c) if strict else (r >= c)


def _deinterleave_kernel(w_ref, p_ref, o_ref):
    half = o_ref.shape[2] // 2
    hw = PERM_W // 2
    for s in range(o_ref.shape[2] // PERM_W):
        t = _dot(w_ref[0, :, s * PERM_W:(s + 1) * PERM_W].astype(BF16), p_ref[...])
        o_ref[0, :, s * hw:(s + 1) * hw] = t[:, :hw].astype(BF16)
        o_ref[0, :, half + s * hw:half + (s + 1) * hw] = t[:, hw:].astype(BF16)


def _deinterleave_gate_up(w_gate_up):
    e, k, n = w_gate_up.shape
    tn = 2 * MOE_TF
    src = jnp.arange(PERM_W)
    dst = jnp.where(src % 2 == 0, src // 2, PERM_W // 2 + src // 2)
    perm = (dst[:, None] == jnp.arange(PERM_W)[None, :]).astype(BF16)
    return pl.pallas_call(
        _deinterleave_kernel,
        grid=(e, n // tn),
        in_specs=[pl.BlockSpec((1, k, tn), lambda i, j: (i, 0, j)),
                  pl.BlockSpec((PERM_W, PERM_W), lambda i, j: (0, 0))],
        out_specs=pl.BlockSpec((1, k, tn), lambda i, j: (i, 0, j)),
        out_shape=jax.ShapeDtypeStruct((e, k, n), BF16),
        compiler_params=_cparams(("arbitrary", "arbitrary")),
        name="moe_deinterleave",
    )(w_gate_up, perm)


def _inproj_kernel(x_ref, nw_ref, w_ref, ws_ref, o_ref, os_ref, n_scr):
    @pl.when(pl.program_id(1) == 0)
    def _():
        x = x_ref[...]
        n = x * lax.rsqrt(jnp.mean(x * x, axis=-1, keepdims=True) + EPS) * nw_ref[...]
        nb = n.astype(BF16)
        n_scr[...] = nb
        os_ref[...] = _dot(nb, ws_ref[...])

    o_ref[...] = _dot(n_scr[...], w_ref[...]).astype(BF16)


def _input_projection(xf, norm_w, w_big, w_small):
    t = xf.shape[0]
    return pl.pallas_call(
        _inproj_kernel,
        grid=(t // IN_TM, D_BIG // IN_TN),
        in_specs=[pl.BlockSpec((IN_TM, D_MODEL), lambda i, j: (i, 0)),
                  pl.BlockSpec((1, D_MODEL), lambda i, j: (0, 0)),
                  pl.BlockSpec((D_MODEL, IN_TN), lambda i, j: (0, j)),
                  pl.BlockSpec((D_MODEL, LANES), lambda i, j: (0, 0))],
        out_specs=[pl.BlockSpec((IN_TM, IN_TN), lambda i, j: (i, j)),
                   pl.BlockSpec((IN_TM, LANES), lambda i, j: (i, 0))],
        out_shape=[jax.ShapeDtypeStruct((t, D_BIG), BF16),
                   jax.ShapeDtypeStruct((t, LANES), F32)],
        scratch_shapes=[pltpu.VMEM((IN_TM, D_MODEL), BF16)],
        compiler_params=_cparams(("arbitrary", "arbitrary")),
        name="inproj",
    )(xf, norm_w, w_big, w_small)


def _ssd_kernel(z_ref, x_ref, bc_ref, sm_ref, cwx_ref, cwbc_ref, cbx_ref, cbbc_ref, dtb_ref, alog_ref,
                dskip_ref, nw_ref, e_ref, o_ref, extx, extbc, state, y_scr):
    q = CHUNK
    halo = SUBLANES

    @pl.when(pl.program_id(1) == 0)
    def _():
        extx[0:halo, :] = jnp.zeros((halo, extx.shape[1]), F32)
        extbc[0:halo, :] = jnp.zeros((halo, extbc.shape[1]), F32)
        state[...] = jnp.zeros_like(state)

    extx[halo:halo + q, :] = x_ref[0].astype(F32)
    extbc[halo:halo + q, :] = bc_ref[0].astype(F32)

    def conv_silu(ext, w_ref, b_ref):
        first = halo - (SSD_CONV - 1)
        acc = b_ref[...] + ext[first:first + q, :] * w_ref[0:1, :]
        for j in range(1, SSD_CONV):
            acc = acc + ext[first + j:first + j + q, :] * w_ref[j:j + 1, :]
        return acc * _sigmoid(acc)

    xs = conv_silu(extx, cwx_ref, cbx_ref)
    bcs = conv_silu(extbc, cwbc_ref, cbbc_ref)
    extx[0:halo, :] = extx[q:q + halo, :]
    extbc[0:halo, :] = extbc[q:q + halo, :]

    lane = lax.broadcasted_iota(I32, (1, LANES), 1)
    g = sm_ref[0]
    dt = _softplus(g + dtb_ref[...])
    a = jnp.where(lane < SSD_HEADS, -jnp.exp(alog_ref[...]), 0.0)
    da = dt * a
    tri = _tril(q).astype(F32)
    a_cum = _dot_exact(tri, da)
    a_cum_t = a_cum.T
    a_end = a_cum[q - 1:q, :]
    stack = jnp.concatenate(
        [dt, jnp.exp(a_end - a_cum), jnp.exp(a_cum), jnp.broadcast_to(jnp.exp(a_end), (SUBLANES, LANES))], axis=0)
    ex = _split3_dot(stack, e_ref[...])
    dt_x = ex[0:q]
    to_end_x = ex[q:2 * q]
    from_start_x = ex[2 * q:3 * q]
    chunk_decay_x = ex[3 * q:3 * q + 1]

    xdt = xs * dt_x
    xdt_b = xdt.astype(BF16)
    xde_b = (xdt * to_end_x).astype(BF16)
    causal = _tril(q)
    lane_q = lax.broadcasted_iota(I32, (1, LANES), 1)
    gw = SSD_STATE
    hpg = SSD_HEADS // SSD_GROUPS
    cpg = D_SSD // SSD_GROUPS
    for gi in range(SSD_GROUPS):
        bg = bcs[:, gi * gw:(gi + 1) * gw]
        cg_b = bcs[:, SSD_GROUPS * gw + gi * gw:SSD_GROUPS * gw + (gi + 1) * gw].astype(BF16)
        cb = lax.dot_general(cg_b, bg.astype(BF16), (((1,), (1,)), ((), ())), preferred_element_type=F32)
        bg_t = bg.T.astype(BF16)
        cs = slice(gi * cpg, (gi + 1) * cpg)
        prev = state[:, cs]
        y_off = _dot(cg_b, prev.astype(BF16)) * from_start_x[:, cs]
        state[:, cs] = prev * chunk_decay_x[:, cs] + _dot(bg_t, xde_b[:, cs])
        for pr in range(hpg // 2):
            c0 = gi * cpg + pr * LANES
            xp = xdt_b[:, c0:c0 + LANES]
            acc = y_off[:, pr * LANES:(pr + 1) * LANES]
            for sub in range(2):
                h = gi * hpg + pr * 2 + sub
                seg = a_cum[:, h:h + 1] - a_cum_t[h:h + 1, :]
                decay = jnp.exp(jnp.where(causal, seg, -jnp.inf))
                m = (cb * decay).astype(BF16)
                keep = (lane_q < SSD_HEAD_DIM) if sub == 0 else (lane_q >= SSD_HEAD_DIM)
                acc = acc + _dot(m, jnp.where(keep, xp, jnp.zeros_like(xp)))
            y_scr[:, c0:c0 + LANES] = acc

    y = y_scr[...] + xs * dskip_ref[...]
    zf = z_ref[0].astype(F32)
    y = y * (zf * _sigmoid(zf))
    for gi in range(SSD_GROUPS):
        cs = slice(gi * cpg, (gi + 1) * cpg)
        yg = y[:, cs]
        yg = yg * lax.rsqrt(jnp.mean(yg * yg, axis=-1, keepdims=True) + EPS)
        o_ref[0, :, cs] = (yg * nw_ref[:, cs]).astype(BF16)


def _ssd_mixer(proj, small, conv_w, conv_b, dt_bias, a_log, d_skip, norm_w, b, s):
    nc = s // CHUNK
    pad = LANES - SSD_HEADS
    dtb = jnp.pad(dt_bias, (0, pad)).reshape(1, LANES)
    alog = jnp.pad(a_log, (0, pad)).reshape(1, LANES)
    dskip = jnp.repeat(d_skip, SSD_HEAD_DIM).reshape(1, D_SSD)
    expand = (jnp.arange(LANES)[:, None] == (jnp.arange(D_SSD) // SSD_HEAD_DIM)[None, :]).astype(BF16)
    d_bc = 2 * SSD_GROUPS * SSD_STATE
    const = lambda shape: pl.BlockSpec(shape, lambda i, j: (0,) * len(shape))
    col = lambda width, off: pl.BlockSpec((1, CHUNK, width), lambda i, j: (i, j, off // width))
    return pl.pallas_call(
        _ssd_kernel,
        grid=(b, nc),
        in_specs=[col(D_SSD, COL_Z), col(D_SSD, COL_X), col(d_bc, COL_BC), col(LANES, 0),
                  const((SSD_CONV, D_SSD)), const((SSD_CONV, d_bc)), const((1, D_SSD)), const((1, d_bc)),
                  const((1, LANES)), const((1, LANES)), const((1, D_SSD)), const((1, D_SSD)),
                  const((LANES, D_SSD))],
        out_specs=pl.BlockSpec((1, CHUNK, D_SSD), lambda i, j: (i, j, 0)),
        out_shape=jax.ShapeDtypeStruct((b, s, D_SSD), BF16),
        scratch_shapes=[pltpu.VMEM((CHUNK + SUBLANES, D_SSD), F32),
                        pltpu.VMEM((CHUNK + SUBLANES, d_bc), F32),
                        pltpu.VMEM((SSD_STATE, D_SSD), F32),
                        pltpu.VMEM((CHUNK, D_SSD), F32)],
        compiler_params=_cparams(("arbitrary", "arbitrary")),
        name="ssd_mixer",
    )(proj, proj, proj, small, conv_w[:, :D_SSD], conv_w[:, D_SSD:], conv_b[:D_SSD].reshape(1, -1),
      conv_b[D_SSD:].reshape(1, -1), dtb, alog, dskip, norm_w.reshape(1, -1), expand)


def _mlstm_kernel(q_ref, k_ref, v_ref, o_ref, sm_ref, bias_ref, nw_ref, out_ref, c_scr, n_scr, m_scr):
    L = CHUNK

    @pl.when(pl.program_id(1) == 0)
    def _():
        c_scr[...] = jnp.zeros_like(c_scr)
        n_scr[...] = jnp.zeros_like(n_scr)
        m_scr[...] = jnp.zeros_like(m_scr)

    g = sm_ref[0] + bias_ref[...]
    log_f = jnp.minimum(g, 0.0) - jnp.log1p(jnp.exp(-jnp.abs(g)))
    tri = _tril(L).astype(F32)
    bcum = _dot_exact(tri, log_f)
    g_t = g.T
    bcum_t = bcum.T
    causal = _tril(L)
    scale = ML_DQK ** -0.5
    for h in range(ML_HEADS):
        li, lf = LANE_I + h, LANE_F + h
        li_col, li_row = g[:, li:li + 1], g_t[li:li + 1, :]
        b_col, b_row = bcum[:, lf:lf + 1], bcum_t[lf:lf + 1, :]
        btot = bcum[L - 1:L, lf:lf + 1]
        m_prev = m_scr[h:h + 1, 0:1]
        c_prev = c_scr[h]
        n_prev = n_scr[h:h + 1, :]
        qb = q_ref[0, :, h * ML_DQK:(h + 1) * ML_DQK]
        kb = k_ref[0, :, h * ML_DQK:(h + 1) * ML_DQK]
        vb = v_ref[0, :, h * ML_DV:(h + 1) * ML_DV]
        w_end = btot - b_col + li_col
        m_loc = jnp.max(w_end, axis=0, keepdims=True)
        k_end = kb.astype(F32) * jnp.exp(w_end - m_loc)
        kv = lax.dot_general(k_end.astype(BF16), vb, (((0,), (0,)), ((), ())), preferred_element_type=F32)
        nk = jnp.sum(k_end, axis=0, keepdims=True)
        d_log = jnp.where(causal, b_col - b_row + li_row, -jnp.inf)
        m_inter = b_col + m_prev
        m_t = jnp.maximum(m_inter, jnp.max(d_log, axis=1, keepdims=True))
        s_mat = lax.dot_general(qb, kb, (((1,), (1,)), ((), ())), preferred_element_type=F32)
        s_mat = s_mat * (scale * jnp.exp(d_log - m_t))
        inter = scale * jnp.exp(m_inter - m_t)
        num = _dot(s_mat.astype(BF16), vb) + _dot(qb, c_prev.astype(BF16)) * inter
        qn = jnp.sum(qb.astype(F32) * n_prev, axis=1, keepdims=True)
        den = jnp.sum(s_mat, axis=1, keepdims=True) + qn * inter
        den = jnp.maximum(jnp.abs(den), jnp.exp(-m_t))
        ht = num / den
        ht = ht * lax.rsqrt(jnp.mean(ht * ht, axis=-1, keepdims=True) + EPS)
        vs = slice(h * ML_DV, (h + 1) * ML_DV)
        og = _sigmoid(o_ref[0, :, vs].astype(F32))
        out_ref[0, :, vs] = (ht * nw_ref[:, vs] * og).astype(BF16)
        m_new = jnp.maximum(btot + m_prev, m_loc)
        fa = jnp.exp(btot + m_prev - m_new)
        fg = jnp.exp(m_loc - m_new)
        c_scr[h] = fa * c_prev + fg * kv
        n_scr[h:h + 1, :] = fa * n_prev + fg * nk
        m_scr[h:h + 1, :] = jnp.broadcast_to(m_new, (1, LANES))


def _mlstm_mixer(proj, small, i_bias, f_bias, norm_w, b, s):
    nc = s // CHUNK
    bias = jnp.zeros((LANES,), F32).at[LANE_I:LANE_I + ML_HEADS].set(i_bias).at[LANE_F:LANE_F + ML_HEADS].set(f_bias)
    d_qk = ML_HEADS * ML_DQK
    const = lambda shape: pl.BlockSpec(shape, lambda i, j: (0,) * len(shape))
    col = lambda width, off: pl.BlockSpec((1, CHUNK, width), lambda i, j: (i, j, off // width))
    return pl.pallas_call(
        _mlstm_kernel,
        grid=(b, nc),
        in_specs=[col(d_qk, COL_Q), col(d_qk, COL_K), col(D_ML, COL_V), col(D_ML, COL_O), col(LANES, 0),
                  const((1, LANES)), const((1, D_ML))],
        out_specs=pl.BlockSpec((1, CHUNK, D_ML), lambda i, j: (i, j, 0)),
        out_shape=jax.ShapeDtypeStruct((b, s, D_ML), BF16),
        scratch_shapes=[pltpu.VMEM((ML_HEADS, ML_DQK, ML_DV), F32),
                        pltpu.VMEM((ML_HEADS, ML_DQK), F32),
                        pltpu.VMEM((ML_HEADS, LANES), F32)],
        compiler_params=_cparams(("arbitrary", "arbitrary")),
        name="mlstm_mixer",
    )(proj, proj, proj, proj, small, bias.reshape(1, LANES), norm_w.reshape(1, -1))


def _outproj_router_kernel(x_ref, ys_ref, ym_ref, wo_ref, n2w_ref, rw_ref, rb_ref,
                           h1_ref, xnp_ref, seli_ref, selg_ref, cnt_ref, carry):
    tm = x_ref.shape[0]

    @pl.when(pl.program_id(0) == 0)
    def _():
        carry[...] = jnp.zeros_like(carry)

    h1 = x_ref[...] + _dot(ys_ref[...], wo_ref[0:D_SSD, :]) + _dot(ym_ref[...], wo_ref[D_SSD:, :])
    h1_ref[...] = h1
    xn = h1 * lax.rsqrt(jnp.mean(h1 * h1, axis=-1, keepdims=True) + EPS) * n2w_ref[...]
    half = D_MODEL // 2
    lo = pltpu.bitcast(xn[:, :half].astype(BF16).astype(F32), U32)
    hi = pltpu.bitcast(xn[:, half:].astype(BF16).astype(F32), U32)
    xnp_ref[...] = hi | (lo >> 16)

    lane = lax.broadcasted_iota(I32, (tm, LANES), 1)
    lane_f = lane.astype(F32)
    logits = _dot_exact(xn, rw_ref[...]) + rb_ref[...]
    work = jnp.where(lane < N_EXPERTS, logits, -jnp.inf)
    vals, idxs = [], []
    chosen = jnp.zeros((tm, LANES), jnp.bool_)
    for _ in range(TOP_K):
        m = jnp.max(work, axis=1, keepdims=True)
        idx = jnp.min(jnp.where(work == m, lane_f, float(LANES)), axis=1, keepdims=True)
        sel = lane_f == idx
        vals.append(m)
        idxs.append(idx)
        chosen = jnp.logical_or(chosen, sel)
        work = jnp.where(sel, -jnp.inf, work)
    exps = [jnp.exp(v - vals[0]) for v in vals]
    denom = exps[0] + exps[1] + exps[2] + exps[3]

    onehot = jnp.where(chosen, 1.0, 0.0)
    ranks = _dot(_tril(tm, strict=True).astype(BF16), onehot.astype(BF16)) + carry[...]
    total = carry[...] + jnp.sum(onehot, axis=0, keepdims=True)
    carry[...] = total
    cnt_ref[...] = jnp.broadcast_to(total, cnt_ref.shape)

    seli = jnp.zeros((tm, LANES), F32)
    selg = jnp.zeros((tm, LANES), F32)
    for k in range(TOP_K):
        rank_k = jnp.sum(jnp.where(lane_f == idxs[k], ranks, 0.0), axis=1, keepdims=True)
        seli = jnp.where(lane == k, idxs[k], seli)
        seli = jnp.where(lane == TOP_K + k, rank_k, seli)
        selg = jnp.where(lane == k, exps[k] / denom, selg)
    seli_ref[...] = seli.astype(I32)
    selg_ref[...] = selg


def _outproj_router(xf, y_ssd, y_ml, w_out_b, norm2_w, router_w_pad, router_b_pad):
    t = xf.shape[0]
    tm = OUT_TM
    row = lambda width: pl.BlockSpec((tm, width), lambda i: (i, 0))
    const = lambda shape: pl.BlockSpec(shape, lambda i: (0,) * len(shape))
    return pl.pallas_call(
        _outproj_router_kernel,
        grid=(t // tm,),
        in_specs=[row(D_MODEL), row(D_SSD), row(D_ML),
                  pl.BlockSpec((D_SSD + D_ML, D_MODEL), lambda i: (0, 0), pipeline_mode=pl.Buffered(1)),
                  const((1, D_MODEL)),
                  const((D_MODEL, LANES)), const((1, LANES))],
        out_specs=[row(D_MODEL), row(D_MODEL // 2), row(LANES), row(LANES), const((SUBLANES, LANES))],
        out_shape=[jax.ShapeDtypeStruct((t, D_MODEL), F32),
                   jax.ShapeDtypeStruct((t, D_MODEL // 2), U32),
                   jax.ShapeDtypeStruct((t, LANES), I32),
                   jax.ShapeDtypeStruct((t, LANES), F32),
                   jax.ShapeDtypeStruct((SUBLANES, LANES), F32)],
        scratch_shapes=[pltpu.VMEM((1, LANES), F32)],
        compiler_params=_cparams(("arbitrary",)),
        name="outproj_router",
    )(xf, y_ssd, y_ml, w_out_b, norm2_w, router_w_pad, router_b_pad)


def _route_kernel(seli_ref, cnt_ref, dest_ref, be_ref):
    tm = seli_ref.shape[0]
    nbp = be_ref.shape[0]
    lane1 = lax.broadcasted_iota(I32, (1, LANES), 1)
    cnt = jnp.where(lane1 < N_EXPERTS, cnt_ref[0:1, :], 0.0)
    padded = jnp.ceil(cnt / MOE_TM) * MOE_TM
    r = lax.broadcasted_iota(I32, (LANES, LANES), 0)
    c = lax.broadcasted_iota(I32, (LANES, LANES), 1)
    upper = jnp.where(r <= c, 1.0, 0.0)
    pend = jnp.round(_dot_exact(jnp.broadcast_to(padded, (SUBLANES, LANES)), upper))[0:1, :]
    pstart = pend - padded

    lane = lax.broadcasted_iota(I32, (tm, LANES), 1)
    lane_f = lane.astype(F32)
    seli = seli_ref[...].astype(F32)
    dest = jnp.zeros((tm, LANES), F32)
    for k in range(TOP_K):
        idx_k = jnp.sum(jnp.where(lane == k, seli, 0.0), axis=1, keepdims=True)
        rank_k = jnp.sum(jnp.where(lane == TOP_K + k, seli, 0.0), axis=1, keepdims=True)
        start_k = jnp.sum(jnp.where(lane_f == idx_k, pstart, 0.0), axis=1, keepdims=True)
        dest = jnp.where(lane == k, start_k + rank_k, dest)
    dest_ref[...] = dest.astype(I32)

    blk_start = (lax.broadcasted_iota(I32, (nbp, LANES), 0) * MOE_TM).astype(F32)
    lane_b = lax.broadcasted_iota(I32, (nbp, LANES), 1)
    passed = jnp.where(jnp.logical_and(lane_b < N_EXPERTS, pend <= blk_start), 1.0, 0.0)
    expert = jnp.minimum(jnp.sum(passed, axis=1, keepdims=True), float(N_EXPERTS - 1))
    n_active = pend[:, N_EXPERTS - 1:N_EXPERTS] / MOE_TM
    be = jnp.where(lane_b == 0, expert, jnp.where(lane_b == 1, n_active, 0.0))
    be_ref[...] = be.astype(I32)


def _route_offsets(seli, cnt, n_blocks):
    t = seli.shape[0]
    nbp = -(-n_blocks // SUBLANES) * SUBLANES
    return pl.pallas_call(
        _route_kernel,
        grid=(t // ROUTE_TM,),
        in_specs=[pl.BlockSpec((ROUTE_TM, LANES), lambda i: (i, 0)),
                  pl.BlockSpec((SUBLANES, LANES), lambda i: (0, 0))],
        out_specs=[pl.BlockSpec((ROUTE_TM, LANES), lambda i: (i, 0)),
                   pl.BlockSpec((nbp, LANES), lambda i: (0, 0))],
        out_shape=[jax.ShapeDtypeStruct((t, LANES), I32),
                   jax.ShapeDtypeStruct((nbp, LANES), I32)],
        compiler_params=_cparams(("arbitrary",)),
        name="route_offsets",
    )(seli, cnt)


def _dispatch_kernel(dest_ref, xnp_ref, rows_in_ref, rows_ref, sem):
    del rows_in_ref
    tt = xnp_ref.shape[0]

    def row_copy(t, d):
        return pltpu.make_async_copy(xnp_ref.at[pl.ds(t, 1)], rows_ref.at[pl.ds(d, 1)], sem)

    def issue(j, carry):
        row_copy(j // TOP_K, dest_ref[j]).start()
        return carry

    lax.fori_loop(0, tt * TOP_K, issue, 0)

    def drain(j, carry):
        row_copy(0, 0).wait()
        return carry

    lax.fori_loop(0, tt * TOP_K, drain, 0)


def _dispatch(dest_flat, xnp, n_rows):
    t, w = xnp.shape
    rows0 = jnp.zeros((n_rows, w), U32)
    return pl.pallas_call(
        _dispatch_kernel,
        grid=(t // DISP_TT,),
        in_specs=[pl.BlockSpec((DISP_TT * TOP_K,), lambda i: (i,), memory_space=pltpu.SMEM),
                  pl.BlockSpec((DISP_TT, w), lambda i: (i, 0)),
                  pl.BlockSpec(memory_space=pl.ANY)],
        out_specs=pl.BlockSpec(memory_space=pl.ANY),
        out_shape=jax.ShapeDtypeStruct((n_rows, w), U32),
        scratch_shapes=[pltpu.SemaphoreType.DMA(())],
        input_output_aliases={2: 0},
        compiler_params=_cparams(("arbitrary",)),
        name="moe_dispatch",
    )(dest_flat, xnp, rows0)


def _moe_kernel(be_ref, na_ref, x_ref, wgu_ref, bgu_ref, wd_ref, bd_ref, o_ref, xb, acc):
    m = pl.program_id(0)
    f = pl.program_id(1)
    nf = pl.num_programs(1)
    half = D_MODEL // 2

    @pl.when(m < na_ref[0])
    def _():
        @pl.when(f == 0)
        def _():
            u = x_ref[...]
            xb[:, :half] = pltpu.bitcast(u << 16, F32).astype(BF16)
            xb[:, half:] = pltpu.bitcast(u & jnp.uint32(0xFFFF0000), F32).astype(BF16)

        h = _dot(xb[...], wgu_ref[0]) + bgu_ref[0]
        glu = jnp.minimum(h[:, :MOE_TF], SWIGLU_LIMIT)
        lin = jnp.clip(h[:, MOE_TF:], -SWIGLU_LIMIT, SWIGLU_LIMIT)
        act = glu * _sigmoid(SWIGLU_ALPHA * glu) * (lin + 1.0)
        part = _dot(act.astype(BF16), wd_ref[0].astype(BF16))

        @pl.when(f == 0)
        def _():
            acc[...] = part

        @pl.when(f > 0)
        def _():
            acc[...] += part

        @pl.when(f == nf - 1)
        def _():
            o_ref[...] = acc[...] + bd_ref[0]

    @pl.when(jnp.logical_and(m >= na_ref[0], f == nf - 1))
    def _():
        o_ref[...] = jnp.zeros_like(o_ref)


def _moe_ffn(block_expert, n_active, x_rows, wgu_b, bgu_p, w_down, b_down, n_blocks):
    nf = D_FF // MOE_TF
    half = D_MODEL // 2

    def blk(m, na):
        return jnp.minimum(m, na[0] - 1)

    def ftile(m, f, na):
        return jnp.where(m < na[0], f, nf - 1)

    grid_spec = pltpu.PrefetchScalarGridSpec(
        num_scalar_prefetch=2,
        grid=(n_blocks, nf),
        in_specs=[pl.BlockSpec((MOE_TM, half), lambda m, f, be, na: (blk(m, na), 0)),
                  pl.BlockSpec((1, D_MODEL, 2 * MOE_TF), lambda m, f, be, na: (be[blk(m, na)], 0, ftile(m, f, na))),
                  pl.BlockSpec((1, 1, 2 * MOE_TF), lambda m, f, be, na: (be[blk(m, na)], 0, ftile(m, f, na))),
                  pl.BlockSpec((1, MOE_TF, D_MODEL), lambda m, f, be, na: (be[blk(m, na)], ftile(m, f, na), 0)),
                  pl.BlockSpec((1, 1, D_MODEL), lambda m, f, be, na: (be[blk(m, na)], 0, 0))],
        out_specs=pl.BlockSpec((MOE_TM, D_MODEL), lambda m, f, be, na: (m, 0)),
        scratch_shapes=[pltpu.VMEM((MOE_TM, D_MODEL), BF16), pltpu.VMEM((MOE_TM, D_MODEL), F32)],
    )
    return pl.pallas_call(
        _moe_kernel,
        grid_spec=grid_spec,
        out_shape=jax.ShapeDtypeStruct((n_blocks * MOE_TM, D_MODEL), F32),
        compiler_params=_cparams(("arbitrary", "arbitrary")),
        name="moe_ffn",
    )(block_expert, n_active, x_rows, wgu_b, bgu_p, w_down, b_down)


def _combine_kernel(dest_ref, h1_ref, selg_ref, fw_ref, rows_ref, o_ref, buf, sem):
    tt = h1_ref.shape[0]

    def row_copy(j, d):
        return pltpu.make_async_copy(rows_ref.at[pl.ds(d, 1)], buf.at[j % TOP_K, pl.ds(j // TOP_K, 1)], sem)

    def issue(j, carry):
        row_copy(j, dest_ref[j]).start()
        return carry

    lax.fori_loop(0, tt * TOP_K, issue, 0)

    def drain(j, carry):
        row_copy(0, 0).wait()
        return carry

    lax.fori_loop(0, tt * TOP_K, drain, 0)

    gates = selg_ref[...]
    h = h1_ref[...]
    for k in range(TOP_K):
        h = h + gates[:, k:k + 1] * buf[k]
    o_ref[...] = h * lax.rsqrt(jnp.mean(h * h, axis=-1, keepdims=True) + EPS) * fw_ref[...]


def _combine(dest_flat, h1, selg, final_w, y_rows):
    t = h1.shape[0]
    tt = COMB_TT
    return pl.pallas_call(
        _combine_kernel,
        grid=(t // tt,),
        in_specs=[pl.BlockSpec((tt * TOP_K,), lambda i: (i,), memory_space=pltpu.SMEM),
                  pl.BlockSpec((tt, D_MODEL), lambda i: (i, 0)),
                  pl.BlockSpec((tt, LANES), lambda i: (i, 0)),
                  pl.BlockSpec((1, D_MODEL), lambda i: (0, 0)),
                  pl.BlockSpec(memory_space=pl.ANY)],
        out_specs=pl.BlockSpec((tt, D_MODEL), lambda i: (i, 0)),
        out_shape=jax.ShapeDtypeStruct((t, D_MODEL), F32),
        scratch_shapes=[pltpu.VMEM((TOP_K, tt, D_MODEL), F32), pltpu.SemaphoreType.DMA(())],
        compiler_params=_cparams(("arbitrary",)),
        name="moe_combine",
    )(dest_flat, h1, selg, final_w, y_rows)


def _permute_gate_up_cols(a):
    lead = a.shape[:-1]
    return a.reshape(*lead, -1, MOE_TF, 2).swapaxes(-1, -2).reshape(*lead, -1)


def kernel(x, norm1_w, w_in, conv_w, conv_b, dt_bias, a_log, d_skip, ssd_norm_w, i_bias, f_bias, ml_norm_w,
           w_out, norm2_w, router_w, router_b, w_gate_up, b_gate_up, w_down, b_down, final_norm_w):
    b, s, d = x.shape
    t = b * s
    assert w_in.shape[0] == 1, "single-layer problem"
    h = x.reshape(t, d)
    for l in range(1):
        wi = w_in[l]
        o_z, o_xbc, o_dt = 0, D_SSD, D_SSD + D_SSD + 2 * SSD_GROUPS * SSD_STATE
        o_q = o_dt + SSD_HEADS
        o_k = o_q + ML_HEADS * ML_DQK
        o_v = o_k + ML_HEADS * ML_DQK
        o_o = o_v + D_ML
        o_i = o_o + D_ML
        o_f = o_i + ML_HEADS
        w_big = jnp.concatenate(
            [wi[:, o_z:o_z + D_SSD], wi[:, o_xbc:o_xbc + D_SSD], wi[:, o_v:o_v + D_ML], wi[:, o_o:o_o + D_ML],
             wi[:, o_xbc + D_SSD:o_dt], wi[:, o_q:o_k], wi[:, o_k:o_v]], axis=1).astype(BF16)
        w_small = jnp.concatenate(
            [wi[:, o_dt:o_q], wi[:, o_i:o_f], wi[:, o_f:o_f + ML_HEADS],
             jnp.zeros((d, LANES - SSD_HEADS - 2 * ML_HEADS), F32)], axis=1).astype(BF16)

        proj, small = _input_projection(h, norm1_w[l].reshape(1, d), w_big, w_small)
        proj3 = proj.reshape(b, s, D_BIG)
        small3 = small.reshape(b, s, LANES)
        y_ssd = _ssd_mixer(proj3, small3, conv_w[l], conv_b[l], dt_bias[l], a_log[l], d_skip[l], ssd_norm_w[l], b, s)
        y_ml = _mlstm_mixer(proj3, small3, i_bias[l], f_bias[l], ml_norm_w[l], b, s)

        rw = jnp.pad(router_w[l], ((0, 0), (0, LANES - N_EXPERTS)))
        rb = jnp.pad(router_b[l], (0, LANES - N_EXPERTS)).reshape(1, LANES)
        h1, xnp, seli, selg, cnt = _outproj_router(
            h, y_ssd.reshape(t, D_SSD), y_ml.reshape(t, D_ML), w_out[l].astype(BF16), norm2_w[l].reshape(1, d), rw, rb)

        n_blocks = (t * TOP_K) // MOE_TM + N_EXPERTS
        dest, table = _route_offsets(seli, cnt, n_blocks)
        dest_flat = dest[:, :TOP_K].reshape(-1)
        block_expert = table[:n_blocks, 0]
        n_active = table[0:1, 1]

        x_rows = _dispatch(dest_flat, xnp, n_blocks * MOE_TM)
        wgu_b = _deinterleave_gate_up(w_gate_up[l])
        bgu_p = _permute_gate_up_cols(b_gate_up[l]).reshape(N_EXPERTS, 1, 2 * D_FF)
        y_rows = _moe_ffn(block_expert, n_active, x_rows, wgu_b, bgu_p, w_down[l],
                          b_down[l].reshape(N_EXPERTS, 1, d), n_blocks)
        h = _combine(dest_flat, h1, selg, final_norm_w.reshape(1, d), y_rows)
    return h.reshape(b, s, d)
```

```python
import functools

import jax
import jax.numpy as jnp
from jax import lax
from jax.experimental import pallas as pl
from jax.experimental.pallas import tpu as pltpu

F32 = jnp.float32
BF16 = jnp.bfloat16
I32 = jnp.int32
U32 = jnp.uint32

LANES = 128
SUBLANES = 8
VMEM_LIMIT = 56 * 1024 * 1024

D_MODEL = 2048
EPS = 1e-5
SSD_HEAD_DIM = 64
SSD_HEADS = 32
D_SSD = 2048
SSD_GROUPS = 4
SSD_STATE = 128
SSD_CONV = 4
CHUNK = 128
ML_HEADS = 8
ML_DV = 256
ML_DQK = 128
D_ML = 2048
N_EXPERTS = 32
TOP_K = 4
D_FF = 2048
SWIGLU_ALPHA = 1.702
SWIGLU_LIMIT = 7.0

COL_Z, COL_X, COL_V, COL_O = 0, 2048, 4096, 6144
COL_BC, COL_Q, COL_K = 8192, 9216, 10240
D_BIG = 11264
LANE_DT, LANE_I, LANE_F = 0, 32, 40

IN_TM, IN_TN = 1024, 1024
OUT_TM = 512
ROUTE_TM = 2048
DISP_TT = 512
MOE_TM = 512
MOE_TF = 512
PERM_W = 256
COMB_TT = 256
X_SUB = D_MODEL // 2 // LANES
Y_SUB = D_MODEL // LANES


def _cparams(sem, vmem=VMEM_LIMIT):
    return pltpu.CompilerParams(dimension_semantics=sem, vmem_limit_bytes=vmem)


def _dot(a, b):
    return jnp.dot(a, b, preferred_element_type=F32)


def _dot_exact(a, b):
    return jnp.dot(a, b, preferred_element_type=F32, precision=lax.Precision.HIGHEST)


def _split3_dot(x, e_bf16):
    x0 = x.astype(BF16)
    r1 = x - x0.astype(F32)
    x1 = r1.astype(BF16)
    x2 = (r1 - x1.astype(F32)).astype(BF16)
    return _dot(x0, e_bf16) + _dot(x1, e_bf16) + _dot(x2, e_bf16)


def _sigmoid(x):
    return 1.0 / (1.0 + jnp.exp(-x))


def _softplus(x):
    return jnp.maximum(x, 0.0) + jnp.log1p(jnp.exp(-jnp.abs(x)))


def _tril(n, strict=False):
    r = lax.broadcasted_iota(I32, (n, n), 0)
    c = lax.broadcasted_iota(I32, (n, n), 1)
    return (r > c) if strict else (r >= c)


def _deinterleave_kernel(w_ref, p_ref, o_ref):
    half = o_ref.shape[2] // 2
    hw = PERM_W // 2
    for s in range(o_ref.shape[2] // PERM_W):
        t = _dot(w_ref[0, :, s * PERM_W:(s + 1) * PERM_W].astype(BF16), p_ref[...])
        o_ref[0, :, s * hw:(s + 1) * hw] = t[:, :hw].astype(BF16)
        o_ref[0, :, half + s * hw:half + (s + 1) * hw] = t[:, hw:].astype(BF16)


def _deinterleave_gate_up(w_gate_up):
    e, k, n = w_gate_up.shape
    tn = 2 * MOE_TF
    src = jnp.arange(PERM_W)
    dst = jnp.where(src % 2 == 0, src // 2, PERM_W // 2 + src // 2)
    perm = (dst[:, None] == jnp.arange(PERM_W)[None, :]).astype(BF16)
    return pl.pallas_call(
        _deinterleave_kernel,
        grid=(e, n // tn),
        in_specs=[pl.BlockSpec((1, k, tn), lambda i, j: (i, 0, j)),
                  pl.BlockSpec((PERM_W, PERM_W), lambda i, j: (0, 0))],
        out_specs=pl.BlockSpec((1, k, tn), lambda i, j: (i, 0, j)),
        out_shape=jax.ShapeDtypeStruct((e, k, n), BF16),
        compiler_params=_cparams(("arbitrary", "arbitrary")),
        name="moe_deinterleave",
    )(w_gate_up, perm)


def _inproj_kernel(x_ref, nw_ref, w_ref, ws_ref, o_ref, os_ref, n_scr):
    @pl.when(pl.program_id(1) == 0)
    def _():
        x = x_ref[...]
        n = x * lax.rsqrt(jnp.mean(x * x, axis=-1, keepdims=True) + EPS) * nw_ref[...]
        nb = n.astype(BF16)
        n_scr[...] = nb
        os_ref[...] = _dot(nb, ws_ref[...])

    o_ref[...] = _dot(n_scr[...], w_ref[...]).astype(BF16)


def _input_projection(xf, norm_w, w_big, w_small):
    t = xf.shape[0]
    return pl.pallas_call(
        _inproj_kernel,
        grid=(t // IN_TM, D_BIG // IN_TN),
        in_specs=[pl.BlockSpec((IN_TM, D_MODEL), lambda i, j: (i, 0)),
                  pl.BlockSpec((1, D_MODEL), lambda i, j: (0, 0)),
                  pl.BlockSpec((D_MODEL, IN_TN), lambda i, j: (0, j)),
                  pl.BlockSpec((D_MODEL, LANES), lambda i, j: (0, 0))],
        out_specs=[pl.BlockSpec((IN_TM, IN_TN), lambda i, j: (i, j)),
                   pl.BlockSpec((IN_TM, LANES), lambda i, j: (i, 0))],
        out_shape=[jax.ShapeDtypeStruct((t, D_BIG), BF16),
                   jax.ShapeDtypeStruct((t, LANES), F32)],
        scratch_shapes=[pltpu.VMEM((IN_TM, D_MODEL), BF16)],
        compiler_params=_cparams(("arbitrary", "arbitrary")),
        name="inproj",
    )(xf, norm_w, w_big, w_small)


def _ssd_kernel(z_ref, x_ref, bc_ref, sm_ref, cwx_ref, cwbc_ref, cbx_ref, cbbc_ref, dtb_ref, alog_ref,
                dskip_ref, nw_ref, e_ref, o_ref, extx, extbc, state, y_scr):
    q = CHUNK
    halo = SUBLANES

    @pl.when(pl.program_id(1) == 0)
    def _():
        extx[0:halo, :] = jnp.zeros((halo, extx.shape[1]), F32)
        extbc[0:halo, :] = jnp.zeros((halo, extbc.shape[1]), F32)
        state[...] = jnp.zeros_like(state)

    extx[halo:halo + q, :] = x_ref[0].astype(F32)
    extbc[halo:halo + q, :] = bc_ref[0].astype(F32)

    def conv_silu(ext, w_ref, b_ref):
        first = halo - (SSD_CONV - 1)
        acc = b_ref[...] + ext[first:first + q, :] * w_ref[0:1, :]
        for j in range(1, SSD_CONV):
            acc = acc + ext[first + j:first + j + q, :] * w_ref[j:j + 1, :]
        return acc * _sigmoid(acc)

    xs = conv_silu(extx, cwx_ref, cbx_ref)
    bcs = conv_silu(extbc, cwbc_ref, cbbc_ref)
    extx[0:halo, :] = extx[q:q + halo, :]
    extbc[0:halo, :] = extbc[q:q + halo, :]

    lane = lax.broadcasted_iota(I32, (1, LANES), 1)
    g = sm_ref[0]
    dt = _softplus(g + dtb_ref[...])
    a = jnp.where(lane < SSD_HEADS, -jnp.exp(alog_ref[...]), 0.0)
    da = dt * a
    tri = _tril(q).astype(F32)
    a_cum = _dot_exact(tri, da)
    a_cum_t = a_cum.T
    a_end = a_cum[q - 1:q, :]
    stack = jnp.concatenate(
        [dt, jnp.exp(a_end - a_cum), jnp.exp(a_cum), jnp.broadcast_to(jnp.exp(a_end), (SUBLANES, LANES))], axis=0)
    ex = _split3_dot(stack, e_ref[...])
    dt_x = ex[0:q]
    to_end_x = ex[q:2 * q]
    from_start_x = ex[2 * q:3 * q]
    chunk_decay_x = ex[3 * q:3 * q + 1]

    xdt = xs * dt_x
    xdt_b = xdt.astype(BF16)
    xde_b = (xdt * to_end_x).astype(BF16)
    causal = _tril(q)
    lane_q = lax.broadcasted_iota(I32, (1, LANES), 1)
    gw = SSD_STATE
    hpg = SSD_HEADS // SSD_GROUPS
    cpg = D_SSD // SSD_GROUPS
    for gi in range(SSD_GROUPS):
        bg = bcs[:, gi * gw:(gi + 1) * gw]
        cg_b = bcs[:, SSD_GROUPS * gw + gi * gw:SSD_GROUPS * gw + (gi + 1) * gw].astype(BF16)
        cb = lax.dot_general(cg_b, bg.astype(BF16), (((1,), (1,)), ((), ())), preferred_element_type=F32)
        bg_t = bg.T.astype(BF16)
        cs = slice(gi * cpg, (gi + 1) * cpg)
        prev = state[:, cs]
        y_off = _dot(cg_b, prev.astype(BF16)) * from_start_x[:, cs]
        state[:, cs] = prev * chunk_decay_x[:, cs] + _dot(bg_t, xde_b[:, cs])
        for pr in range(hpg // 2):
            c0 = gi * cpg + pr * LANES
            xp = xdt_b[:, c0:c0 + LANES]
            acc = y_off[:, pr * LANES:(pr + 1) * LANES]
            for sub in range(2):
                h = gi * hpg + pr * 2 + sub
                seg = a_cum[:, h:h + 1] - a_cum_t[h:h + 1, :]
                decay = jnp.exp(jnp.where(causal, seg, -jnp.inf))
                m = (cb * decay).astype(BF16)
                keep = (lane_q < SSD_HEAD_DIM) if sub == 0 else (lane_q >= SSD_HEAD_DIM)
                acc = acc + _dot(m, jnp.where(keep, xp, jnp.zeros_like(xp)))
            y_scr[:, c0:c0 + LANES] = acc

    y = y_scr[...] + xs * dskip_ref[...]
    zf = z_ref[0].astype(F32)
    y = y * (zf * _sigmoid(zf))
    for gi in range(SSD_GROUPS):
        cs = slice(gi * cpg, (gi + 1) * cpg)
        yg = y[:, cs]
        yg = yg * lax.rsqrt(jnp.mean(yg * yg, axis=-1, keepdims=True) + EPS)
        o_ref[0, :, cs] = (yg * nw_ref[:, cs]).astype(BF16)


def _ssd_mixer(proj, small, conv_w, conv_b, dt_bias, a_log, d_skip, norm_w, b, s):
    nc = s // CHUNK
    pad = LANES - SSD_HEADS
    dtb = jnp.pad(dt_bias, (0, pad)).reshape(1, LANES)
    alog = jnp.pad(a_log, (0, pad)).reshape(1, LANES)
    dskip = jnp.repeat(d_skip, SSD_HEAD_DIM).reshape(1, D_SSD)
    expand = (jnp.arange(LANES)[:, None] == (jnp.arange(D_SSD) // SSD_HEAD_DIM)[None, :]).astype(BF16)
    d_bc = 2 * SSD_GROUPS * SSD_STATE
    const = lambda shape: pl.BlockSpec(shape, lambda i, j: (0,) * len(shape))
    col = lambda width, off: pl.BlockSpec((1, CHUNK, width), lambda i, j: (i, j, off // width))
    return pl.pallas_call(
        _ssd_kernel,
        grid=(b, nc),
        in_specs=[col(D_SSD, COL_Z), col(D_SSD, COL_X), col(d_bc, COL_BC), col(LANES, 0),
                  const((SSD_CONV, D_SSD)), const((SSD_CONV, d_bc)), const((1, D_SSD)), const((1, d_bc)),
                  const((1, LANES)), const((1, LANES)), const((1, D_SSD)), const((1, D_SSD)),
                  const((LANES, D_SSD))],
        out_specs=pl.BlockSpec((1, CHUNK, D_SSD), lambda i, j: (i, j, 0)),
        out_shape=jax.ShapeDtypeStruct((b, s, D_SSD), BF16),
        scratch_shapes=[pltpu.VMEM((CHUNK + SUBLANES, D_SSD), F32),
                        pltpu.VMEM((CHUNK + SUBLANES, d_bc), F32),
                        pltpu.VMEM((SSD_STATE, D_SSD), F32),
                        pltpu.VMEM((CHUNK, D_SSD), F32)],
        compiler_params=_cparams(("arbitrary", "arbitrary")),
        name="ssd_mixer",
    )(proj, proj, proj, small, conv_w[:, :D_SSD], conv_w[:, D_SSD:], conv_b[:D_SSD].reshape(1, -1),
      conv_b[D_SSD:].reshape(1, -1), dtb, alog, dskip, norm_w.reshape(1, -1), expand)


def _mlstm_kernel(q_ref, k_ref, v_ref, o_ref, sm_ref, bias_ref, nw_ref, out_ref, c_scr, n_scr, m_scr):
    L = CHUNK

    @pl.when(pl.program_id(1) == 0)
    def _():
        c_scr[...] = jnp.zeros_like(c_scr)
        n_scr[...] = jnp.zeros_like(n_scr)
        m_scr[...] = jnp.zeros_like(m_scr)

    g = sm_ref[0] + bias_ref[...]
    log_f = jnp.minimum(g, 0.0) - jnp.log1p(jnp.exp(-jnp.abs(g)))
    tri = _tril(L).astype(F32)
    bcum = _dot_exact(tri, log_f)
    g_t = g.T
    bcum_t = bcum.T
    causal = _tril(L)
    scale = ML_DQK ** -0.5
    for h in range(ML_HEADS):
        li, lf = LANE_I + h, LANE_F + h
        li_col, li_row = g[:, li:li + 1], g_t[li:li + 1, :]
        b_col, b_row = bcum[:, lf:lf + 1], bcum_t[lf:lf + 1, :]
        btot = bcum[L - 1:L, lf:lf + 1]
        m_prev = m_scr[h:h + 1, 0:1]
        c_prev = c_scr[h]
        n_prev = n_scr[h:h + 1, :]
        qb = q_ref[0, :, h * ML_DQK:(h + 1) * ML_DQK]
        kb = k_ref[0, :, h * ML_DQK:(h + 1) * ML_DQK]
        vb = v_ref[0, :, h * ML_DV:(h + 1) * ML_DV]
        w_end = btot - b_col + li_col
        m_loc = jnp.max(w_end, axis=0, keepdims=True)
        k_end = kb.astype(F32) * jnp.exp(w_end - m_loc)
        kv = lax.dot_general(k_end.astype(BF16), vb, (((0,), (0,)), ((), ())), preferred_element_type=F32)
        nk = jnp.sum(k_end, axis=0, keepdims=True)
        d_log = jnp.where(causal, b_col - b_row + li_row, -jnp.inf)
        m_inter = b_col + m_prev
        m_t = jnp.maximum(m_inter, jnp.max(d_log, axis=1, keepdims=True))
        s_mat = lax.dot_general(qb, kb, (((1,), (1,)), ((), ())), preferred_element_type=F32)
        s_mat = s_mat * (scale * jnp.exp(d_log - m_t))
        inter = scale * jnp.exp(m_inter - m_t)
        num = _dot(s_mat.astype(BF16), vb) + _dot(qb, c_prev.astype(BF16)) * inter
        qn = jnp.sum(qb.astype(F32) * n_prev, axis=1, keepdims=True)
        den = jnp.sum(s_mat, axis=1, keepdims=True) + qn * inter
        den = jnp.maximum(jnp.abs(den), jnp.exp(-m_t))
        ht = num / den
        ht = ht * lax.rsqrt(jnp.mean(ht * ht, axis=-1, keepdims=True) + EPS)
        vs = slice(h * ML_DV, (h + 1) * ML_DV)
        og = _sigmoid(o_ref[0, :, vs].astype(F32))
        out_ref[0, :, vs] = (ht * nw_ref[:, vs] * og).astype(BF16)
        m_new = jnp.maximum(btot + m_prev, m_loc)
        fa = jnp.exp(btot + m_prev - m_new)
        fg = jnp.exp(m_loc - m_new)
        c_scr[h] = fa * c_prev + fg * kv
        n_scr[h:h + 1, :] = fa * n_prev + fg * nk
        m_scr[h:h + 1, :] = jnp.broadcast_to(m_new, (1, LANES))


def _mlstm_mixer(proj, small, i_bias, f_bias, norm_w, b, s):
    nc = s // CHUNK
    bias = jnp.zeros((LANES,), F32).at[LANE_I:LANE_I + ML_HEADS].set(i_bias).at[LANE_F:LANE_F + ML_HEADS].set(f_bias)
    d_qk = ML_HEADS * ML_DQK
    const = lambda shape: pl.BlockSpec(shape, lambda i, j: (0,) * len(shape))
    col = lambda width, off: pl.BlockSpec((1, CHUNK, width), lambda i, j: (i, j, off // width))
    return pl.pallas_call(
        _mlstm_kernel,
        grid=(b, nc),
        in_specs=[col(d_qk, COL_Q), col(d_qk, COL_K), col(D_ML, COL_V), col(D_ML, COL_O), col(LANES, 0),
                  const((1, LANES)), const((1, D_ML))],
        out_specs=pl.BlockSpec((1, CHUNK, D_ML), lambda i, j: (i, j, 0)),
        out_shape=jax.ShapeDtypeStruct((b, s, D_ML), BF16),
        scratch_shapes=[pltpu.VMEM((ML_HEADS, ML_DQK, ML_DV), F32),
                        pltpu.VMEM((ML_HEADS, ML_DQK), F32),
                        pltpu.VMEM((ML_HEADS, LANES), F32)],
        compiler_params=_cparams(("arbitrary", "arbitrary")),
        name="mlstm_mixer",
    )(proj, proj, proj, proj, small, bias.reshape(1, LANES), norm_w.reshape(1, -1))


def _outproj_router_kernel(x_ref, ys_ref, ym_ref, wo_ref, n2w_ref, rw_ref, rb_ref,
                           h1_ref, xnp_ref, seli_ref, selg_ref, cnt_ref, carry):
    tm = x_ref.shape[0]

    @pl.when(pl.program_id(0) == 0)
    def _():
        carry[...] = jnp.zeros_like(carry)

    h1 = x_ref[...] + _dot(ys_ref[...], wo_ref[0:D_SSD, :]) + _dot(ym_ref[...], wo_ref[D_SSD:, :])
    h1_ref[...] = h1
    xn = h1 * lax.rsqrt(jnp.mean(h1 * h1, axis=-1, keepdims=True) + EPS) * n2w_ref[...]
    half = D_MODEL // 2
    lo = pltpu.bitcast(xn[:, :half].astype(BF16).astype(F32), U32)
    hi = pltpu.bitcast(xn[:, half:].astype(BF16).astype(F32), U32)
    packed = hi | (lo >> 16)
    for sub in range(X_SUB):
        xnp_ref[pl.ds(sub, tm, stride=X_SUB), :] = packed[:, sub * LANES:(sub + 1) * LANES]

    lane = lax.broadcasted_iota(I32, (tm, LANES), 1)
    lane_f = lane.astype(F32)
    xn_hi = xn.astype(BF16)
    xn_lo = (xn - xn_hi.astype(F32)).astype(BF16)
    hh_hl = _dot(xn_hi, rw_ref[...])
    logits = hh_hl[:, :LANES] + hh_hl[:, LANES:] + _dot(xn_lo, rw_ref[:, :LANES]) + rb_ref[...]
    work = jnp.where(lane < N_EXPERTS, logits, -jnp.inf)
    vals, idxs = [], []
    chosen = jnp.zeros((tm, LANES), jnp.bool_)
    for _ in range(TOP_K):
        m = jnp.max(work, axis=1, keepdims=True)
        idx = jnp.min(jnp.where(work == m, lane_f, float(LANES)), axis=1, keepdims=True)
        sel = lane_f == idx
        vals.append(m)
        idxs.append(idx)
        chosen = jnp.logical_or(chosen, sel)
        work = jnp.where(sel, -jnp.inf, work)
    exps = [jnp.exp(v - vals[0]) for v in vals]
    denom = exps[0] + exps[1] + exps[2] + exps[3]

    onehot = jnp.where(chosen, 1.0, 0.0)
    ranks = _dot(_tril(tm, strict=True).astype(BF16), onehot.astype(BF16)) + carry[...]
    total = carry[...] + jnp.sum(onehot, axis=0, keepdims=True)
    carry[...] = total
    cnt_ref[...] = jnp.broadcast_to(total, cnt_ref.shape)

    seli = jnp.zeros((tm, LANES), F32)
    selg = jnp.zeros((tm, LANES), F32)
    for k in range(TOP_K):
        rank_k = jnp.sum(jnp.where(lane_f == idxs[k], ranks, 0.0), axis=1, keepdims=True)
        seli = jnp.where(lane == k, idxs[k], seli)
        seli = jnp.where(lane == TOP_K + k, rank_k, seli)
        selg = jnp.where(lane == k, exps[k] / denom, selg)
    seli_ref[...] = seli.astype(I32)
    selg_ref[...] = selg


def _outproj_router(xf, y_ssd, y_ml, w_out_b, norm2_w, router_w_pad, router_b_pad):
    t = xf.shape[0]
    tm = OUT_TM
    row = lambda width: pl.BlockSpec((tm, width), lambda i: (i, 0))
    const = lambda shape: pl.BlockSpec(shape, lambda i: (0,) * len(shape))
    return pl.pallas_call(
        _outproj_router_kernel,
        grid=(t // tm,),
        in_specs=[row(D_MODEL), row(D_SSD), row(D_ML),
                  pl.BlockSpec((D_SSD + D_ML, D_MODEL), lambda i: (0, 0), pipeline_mode=pl.Buffered(1)),
                  const((1, D_MODEL)),
                  const((D_MODEL, 2 * LANES)), const((1, LANES))],
        out_specs=[row(D_MODEL), pl.BlockSpec((tm * X_SUB, LANES), lambda i: (i, 0)), row(LANES), row(LANES),
                   const((SUBLANES, LANES))],
        out_shape=[jax.ShapeDtypeStruct((t, D_MODEL), F32),
                   jax.ShapeDtypeStruct((t * X_SUB, LANES), U32),
                   jax.ShapeDtypeStruct((t, LANES), I32),
                   jax.ShapeDtypeStruct((t, LANES), F32),
                   jax.ShapeDtypeStruct((SUBLANES, LANES), F32)],
        scratch_shapes=[pltpu.VMEM((1, LANES), F32)],
        compiler_params=_cparams(("arbitrary",)),
        name="outproj_router",
    )(xf, y_ssd, y_ml, w_out_b, norm2_w, router_w_pad, router_b_pad)


def _route_kernel(seli_ref, cnt_ref, dest_ref, be_ref):
    tm = seli_ref.shape[0]
    nbp = be_ref.shape[0]
    lane1 = lax.broadcasted_iota(I32, (1, LANES), 1)
    cnt = jnp.where(lane1 < N_EXPERTS, cnt_ref[0:1, :], 0.0)
    padded = jnp.ceil(cnt / MOE_TM) * MOE_TM
    r = lax.broadcasted_iota(I32, (LANES, LANES), 0)
    c = lax.broadcasted_iota(I32, (LANES, LANES), 1)
    upper = jnp.where(r <= c, 1.0, 0.0)
    pend = jnp.round(_dot_exact(jnp.broadcast_to(padded, (SUBLANES, LANES)), upper))[0:1, :]
    pstart = pend - padded

    lane = lax.broadcasted_iota(I32, (tm, LANES), 1)
    lane_f = lane.astype(F32)
    seli = seli_ref[...].astype(F32)
    dest = jnp.zeros((tm, LANES), F32)
    for k in range(TOP_K):
        idx_k = jnp.sum(jnp.where(lane == k, seli, 0.0), axis=1, keepdims=True)
        rank_k = jnp.sum(jnp.where(lane == TOP_K + k, seli, 0.0), axis=1, keepdims=True)
        start_k = jnp.sum(jnp.where(lane_f == idx_k, pstart, 0.0), axis=1, keepdims=True)
        dest = jnp.where(lane == k, start_k + rank_k, dest)
    dest_ref[...] = dest.astype(I32)

    blk_start = (lax.broadcasted_iota(I32, (nbp, LANES), 0) * MOE_TM).astype(F32)
    lane_b = lax.broadcasted_iota(I32, (nbp, LANES), 1)
    passed = jnp.where(jnp.logical_and(lane_b < N_EXPERTS, pend <= blk_start), 1.0, 0.0)
    expert = jnp.minimum(jnp.sum(passed, axis=1, keepdims=True), float(N_EXPERTS - 1))
    n_active = pend[:, N_EXPERTS - 1:N_EXPERTS] / MOE_TM
    be = jnp.where(lane_b == 0, expert, jnp.where(lane_b == 1, n_active, 0.0))
    be_ref[...] = be.astype(I32)


def _route_offsets(seli, cnt, n_blocks):
    t = seli.shape[0]
    nbp = -(-n_blocks // SUBLANES) * SUBLANES
    return pl.pallas_call(
        _route_kernel,
        grid=(t // ROUTE_TM,),
        in_specs=[pl.BlockSpec((ROUTE_TM, LANES), lambda i: (i, 0)),
                  pl.BlockSpec((SUBLANES, LANES), lambda i: (0, 0))],
        out_specs=[pl.BlockSpec((ROUTE_TM, LANES), lambda i: (i, 0)),
                   pl.BlockSpec((nbp, LANES), lambda i: (0, 0))],
        out_shape=[jax.ShapeDtypeStruct((t, LANES), I32),
                   jax.ShapeDtypeStruct((nbp, LANES), I32)],
        compiler_params=_cparams(("arbitrary",)),
        name="route_offsets",
    )(seli, cnt)


def _dispatch_kernel(dest_ref, xnp_ref, rows_in_ref, rows_ref, sem):
    del rows_in_ref
    tt = xnp_ref.shape[0] // X_SUB

    def tile_copy(t, d):
        src = xnp_ref.at[pl.ds(pl.multiple_of(t * X_SUB, X_SUB), X_SUB)]
        dst = rows_ref.at[pl.ds(pl.multiple_of(d * X_SUB, X_SUB), X_SUB)]
        return pltpu.make_async_copy(src, dst, sem)

    def issue(t, carry):
        for k in range(TOP_K):
            tile_copy(t, dest_ref[t * TOP_K + k]).start()
        return carry

    lax.fori_loop(0, tt, issue, 0, unroll=4)
    for _ in range(TOP_K):
        pltpu.make_async_copy(xnp_ref, rows_ref.at[pl.ds(0, tt * X_SUB)], sem).wait()


def _dispatch(dest_flat, xnp, n_rows):
    t = xnp.shape[0] // X_SUB
    rows0 = jnp.zeros((n_rows * X_SUB, LANES), U32)
    return pl.pallas_call(
        _dispatch_kernel,
        grid=(t // DISP_TT,),
        in_specs=[pl.BlockSpec((DISP_TT * TOP_K,), lambda i: (i,), memory_space=pltpu.SMEM),
                  pl.BlockSpec((DISP_TT * X_SUB, LANES), lambda i: (i, 0)),
                  pl.BlockSpec(memory_space=pl.ANY)],
        out_specs=pl.BlockSpec(memory_space=pl.ANY),
        out_shape=jax.ShapeDtypeStruct((n_rows * X_SUB, LANES), U32),
        scratch_shapes=[pltpu.SemaphoreType.DMA(())],
        input_output_aliases={2: 0},
        compiler_params=_cparams(("arbitrary",)),
        name="moe_dispatch",
    )(dest_flat, xnp, rows0)


def _moe_kernel(be_ref, na_ref, x_ref, wgu_ref, bgu_ref, wd_ref, bd_ref, o_ref, xb, acc):
    m = pl.program_id(0)
    f = pl.program_id(1)
    nf = pl.num_programs(1)
    half = D_MODEL // 2

    @pl.when(m < na_ref[0])
    def _():
        @pl.when(f == 0)
        def _():
            for sub in range(X_SUB):
                u = x_ref[pl.ds(sub, MOE_TM, stride=X_SUB), :]
                cs = slice(sub * LANES, (sub + 1) * LANES)
                xb[:, cs] = pltpu.bitcast(u << 16, F32).astype(BF16)
                cs = slice(half + sub * LANES, half + (sub + 1) * LANES)
                xb[:, cs] = pltpu.bitcast(u & jnp.uint32(0xFFFF0000), F32).astype(BF16)

        h = _dot(xb[...], wgu_ref[0]) + bgu_ref[0]
        glu = jnp.minimum(h[:, :MOE_TF], SWIGLU_LIMIT)
        lin = jnp.clip(h[:, MOE_TF:], -SWIGLU_LIMIT, SWIGLU_LIMIT)
        act = glu * _sigmoid(SWIGLU_ALPHA * glu) * (lin + 1.0)
        part = _dot(act.astype(BF16), wd_ref[0].astype(BF16))

        @pl.when(f == 0)
        def _():
            acc[...] = part

        @pl.when(f > 0)
        def _():
            acc[...] += part

        @pl.when(f == nf - 1)
        def _():
            for sub in range(Y_SUB):
                cs = slice(sub * LANES, (sub + 1) * LANES)
                o_ref[pl.ds(sub, MOE_TM, stride=Y_SUB), :] = acc[:, cs] + bd_ref[0, :, cs]

    @pl.when(jnp.logical_and(m >= na_ref[0], f == nf - 1))
    def _():
        o_ref[...] = jnp.zeros_like(o_ref)


def _moe_ffn(block_expert, n_active, x_rows, wgu_b, bgu_p, w_down, b_down, n_blocks):
    nf = D_FF // MOE_TF
    half = D_MODEL // 2

    def blk(m, na):
        return jnp.minimum(m, na[0] - 1)

    def ftile(m, f, na):
        return jnp.where(m < na[0], f, nf - 1)

    grid_spec = pltpu.PrefetchScalarGridSpec(
        num_scalar_prefetch=2,
        grid=(n_blocks, nf),
        in_specs=[pl.BlockSpec((MOE_TM * X_SUB, LANES), lambda m, f, be, na: (blk(m, na), 0)),
                  pl.BlockSpec((1, D_MODEL, 2 * MOE_TF), lambda m, f, be, na: (be[blk(m, na)], 0, ftile(m, f, na))),
                  pl.BlockSpec((1, 1, 2 * MOE_TF), lambda m, f, be, na: (be[blk(m, na)], 0, ftile(m, f, na))),
                  pl.BlockSpec((1, MOE_TF, D_MODEL), lambda m, f, be, na: (be[blk(m, na)], ftile(m, f, na), 0)),
                  pl.BlockSpec((1, 1, D_MODEL), lambda m, f, be, na: (be[blk(m, na)], 0, 0))],
        out_specs=pl.BlockSpec((MOE_TM * Y_SUB, LANES), lambda m, f, be, na: (m, 0)),
        scratch_shapes=[pltpu.VMEM((MOE_TM, D_MODEL), BF16), pltpu.VMEM((MOE_TM, D_MODEL), F32)],
    )
    return pl.pallas_call(
        _moe_kernel,
        grid_spec=grid_spec,
        out_shape=jax.ShapeDtypeStruct((n_blocks * MOE_TM * Y_SUB, LANES), F32),
        compiler_params=_cparams(("arbitrary", "arbitrary")),
        name="moe_ffn",
    )(block_expert, n_active, x_rows, wgu_b, bgu_p, w_down, b_down)


def _combine_kernel(dest_ref, h1_ref, selg_ref, fw_ref, rows_ref, o_ref, buf, h_scr, sem):
    tt = h1_ref.shape[0]

    def row_copy(t, k, d):
        src = rows_ref.at[pl.ds(pl.multiple_of(d * Y_SUB, Y_SUB), Y_SUB)]
        dst = buf.at[k, pl.ds(pl.multiple_of(t * Y_SUB, Y_SUB), Y_SUB)]
        return pltpu.make_async_copy(src, dst, sem)

    def issue(t, carry):
        for k in range(TOP_K):
            row_copy(t, k, dest_ref[t * TOP_K + k]).start()
        return carry

    lax.fori_loop(0, tt, issue, 0, unroll=4)
    for k in range(TOP_K):
        pltpu.make_async_copy(rows_ref.at[pl.ds(0, tt * Y_SUB)], buf.at[k], sem).wait()

    gates = selg_ref[...]
    for sub in range(Y_SUB):
        cs = slice(sub * LANES, (sub + 1) * LANES)
        h = h1_ref[:, cs]
        for k in range(TOP_K):
            h = h + gates[:, k:k + 1] * buf[k, pl.ds(sub, tt, stride=Y_SUB), :]
        h_scr[:, cs] = h
    h = h_scr[...]
    o_ref[...] = h * lax.rsqrt(jnp.mean(h * h, axis=-1, keepdims=True) + EPS) * fw_ref[...]


def _combine(dest_flat, h1, selg, final_w, y_rows):
    t = h1.shape[0]
    tt = COMB_TT
    return pl.pallas_call(
        _combine_kernel,
        grid=(t // tt,),
        in_specs=[pl.BlockSpec((tt * TOP_K,), lambda i: (i,), memory_space=pltpu.SMEM),
                  pl.BlockSpec((tt, D_MODEL), lambda i: (i, 0)),
                  pl.BlockSpec((tt, LANES), lambda i: (i, 0)),
                  pl.BlockSpec((1, D_MODEL), lambda i: (0, 0)),
                  pl.BlockSpec(memory_space=pl.ANY)],
        out_specs=pl.BlockSpec((tt, D_MODEL), lambda i: (i, 0)),
        out_shape=jax.ShapeDtypeStruct((t, D_MODEL), F32),
        scratch_shapes=[pltpu.VMEM((TOP_K, tt * Y_SUB, LANES), F32), pltpu.VMEM((tt, D_MODEL), F32),
                        pltpu.SemaphoreType.DMA(())],
        compiler_params=_cparams(("arbitrary",)),
        name="moe_combine",
    )(dest_flat, h1, selg, final_w, y_rows)


def _permute_gate_up_cols(a):
    lead = a.shape[:-1]
    return a.reshape(*lead, -1, MOE_TF, 2).swapaxes(-1, -2).reshape(*lead, -1)


def kernel(x, norm1_w, w_in, conv_w, conv_b, dt_bias, a_log, d_skip, ssd_norm_w, i_bias, f_bias, ml_norm_w,
           w_out, norm2_w, router_w, router_b, w_gate_up, b_gate_up, w_down, b_down, final_norm_w):
    b, s, d = x.shape
    t = b * s
    assert w_in.shape[0] == 1, "single-layer problem"
    h = x.reshape(t, d)
    for l in range(1):
        wi = w_in[l]
        o_z, o_xbc, o_dt = 0, D_SSD, D_SSD + D_SSD + 2 * SSD_GROUPS * SSD_STATE
        o_q = o_dt + SSD_HEADS
        o_k = o_q + ML_HEADS * ML_DQK
        o_v = o_k + ML_HEADS * ML_DQK
        o_o = o_v + D_ML
        o_i = o_o + D_ML
        o_f = o_i + ML_HEADS
        w_big = jnp.concatenate(
            [wi[:, o_z:o_z + D_SSD], wi[:, o_xbc:o_xbc + D_SSD], wi[:, o_v:o_v + D_ML], wi[:, o_o:o_o + D_ML],
             wi[:, o_xbc + D_SSD:o_dt], wi[:, o_q:o_k], wi[:, o_k:o_v]], axis=1).astype(BF16)
        w_small = jnp.concatenate(
            [wi[:, o_dt:o_q], wi[:, o_i:o_f], wi[:, o_f:o_f + ML_HEADS],
             jnp.zeros((d, LANES - SSD_HEADS - 2 * ML_HEADS), F32)], axis=1).astype(BF16)

        proj, small = _input_projection(h, norm1_w[l].reshape(1, d), w_big, w_small)
        proj3 = proj.reshape(b, s, D_BIG)
        small3 = small.reshape(b, s, LANES)
        y_ssd = _ssd_mixer(proj3, small3, conv_w[l], conv_b[l], dt_bias[l], a_log[l], d_skip[l], ssd_norm_w[l], b, s)
        y_ml = _mlstm_mixer(proj3, small3, i_bias[l], f_bias[l], ml_norm_w[l], b, s)

        rw = jnp.pad(router_w[l], ((0, 0), (0, LANES - N_EXPERTS)))
        rw_hi = rw.astype(BF16)
        rw = jnp.concatenate([rw_hi, (rw - rw_hi.astype(F32)).astype(BF16)], axis=1)
        rb = jnp.pad(router_b[l], (0, LANES - N_EXPERTS)).reshape(1, LANES)
        h1, xnp, seli, selg, cnt = _outproj_router(
            h, y_ssd.reshape(t, D_SSD), y_ml.reshape(t, D_ML), w_out[l].astype(BF16), norm2_w[l].reshape(1, d), rw, rb)

        n_blocks = (t * TOP_K) // MOE_TM + N_EXPERTS
        dest, table = _route_offsets(seli, cnt, n_blocks)
        dest_flat = dest[:, :TOP_K].reshape(-1)
        block_expert = table[:n_blocks, 0]
        n_active = table[0:1, 1]

        x_rows = _dispatch(dest_flat, xnp, n_blocks * MOE_TM)
        wgu_b = _deinterleave_gate_up(w_gate_up[l])
        bgu_p = _permute_gate_up_cols(b_gate_up[l]).reshape(N_EXPERTS, 1, 2 * D_FF)
        y_rows = _moe_ffn(block_expert, n_active, x_rows, wgu_b, bgu_p, w_down[l],
                          b_down[l].reshape(N_EXPERTS, 1, d), n_blocks)
        h = _combine(dest_flat, h1, selg, final_norm_w.reshape(1, d), y_rows)
    return h.reshape(b, s, d)
```

```python
import functools

import jax
import jax.numpy as jnp
from jax import lax
from jax.experimental import pallas as pl
from jax.experimental.pallas import tpu as pltpu

F32 = jnp.float32
BF16 = jnp.bfloat16
I32 = jnp.int32
U32 = jnp.uint32

LANES = 128
SUBLANES = 8
VMEM_LIMIT = 56 * 1024 * 1024

D_MODEL = 2048
EPS = 1e-5
SSD_HEAD_DIM = 64
SSD_HEADS = 32
D_SSD = 2048
SSD_GROUPS = 4
SSD_STATE = 128
SSD_CONV = 4
CHUNK = 128
ML_HEADS = 8
ML_DV = 256
ML_DQK = 128
D_ML = 2048
N_EXPERTS = 32
TOP_K = 4
D_FF = 2048
SWIGLU_ALPHA = 1.702
SWIGLU_LIMIT = 7.0

COL_Z, COL_X, COL_V, COL_O = 0, 2048, 4096, 6144
COL_BC, COL_Q, COL_K = 8192, 9216, 10240
D_BIG = 11264
LANE_DT, LANE_I, LANE_F = 0, 32, 40

IN_TM, IN_TN = 1024, 1024
OUT_TM = 512
ROUTE_TM = 2048
DISP_TT = 512
MOE_TM = 512
MOE_SB = 256
MOE_TF = 512
PERM_W = 256
COMB_TT = 256
X_SUB = D_MODEL // 2 // LANES


def _cparams(sem, vmem=VMEM_LIMIT):
    return pltpu.CompilerParams(dimension_semantics=sem, vmem_limit_bytes=vmem)


def _dot(a, b):
    return jnp.dot(a, b, preferred_element_type=F32)


def _dot_exact(a, b):
    return jnp.dot(a, b, preferred_element_type=F32, precision=lax.Precision.HIGHEST)


def _split3_dot(x, e_bf16):
    x0 = x.astype(BF16)
    r1 = x - x0.astype(F32)
    x1 = r1.astype(BF16)
    x2 = (r1 - x1.astype(F32)).astype(BF16)
    return _dot(x0, e_bf16) + _dot(x1, e_bf16) + _dot(x2, e_bf16)


def _sigmoid(x):
    return 1.0 / (1.0 + jnp.exp(-x))


def _softplus(x):
    return jnp.maximum(x, 0.0) + jnp.log1p(jnp.exp(-jnp.abs(x)))


def _tril(n, strict=False):
    r = lax.broadcasted_iota(I32, (n, n), 0)
    c = lax.broadcasted_iota(I32, (n, n), 1)
    return (r > c) if strict else (r >= c)


def _pack_bf16_pairs(y):
    half = D_MODEL // 2
    lo = pltpu.bitcast(y[:, :half].astype(BF16).astype(F32), U32)
    hi = pltpu.bitcast(y[:, half:].astype(BF16).astype(F32), U32)
    return hi | (lo >> 16)


def _unpack_bf16_pairs(src_ref, rows, dst_ref):
    half = D_MODEL // 2
    for sub in range(X_SUB):
        u = src_ref[pl.ds(sub, rows, stride=X_SUB), :]
        dst_ref[0:rows, sub * LANES:(sub + 1) * LANES] = pltpu.bitcast(u << 16, F32).astype(BF16)
        dst_ref[0:rows, half + sub * LANES:half + (sub + 1) * LANES] = (
            pltpu.bitcast(u & jnp.uint32(0xFFFF0000), F32).astype(BF16))


def _deinterleave_kernel(w_ref, p_ref, o_ref):
    half = o_ref.shape[2] // 2
    hw = PERM_W // 2
    for s in range(o_ref.shape[2] // PERM_W):
        t = _dot(w_ref[0, :, s * PERM_W:(s + 1) * PERM_W].astype(BF16), p_ref[...])
        o_ref[0, :, s * hw:(s + 1) * hw] = t[:, :hw].astype(BF16)
        o_ref[0, :, half + s * hw:half + (s + 1) * hw] = t[:, hw:].astype(BF16)


def _deinterleave_gate_up(w_gate_up):
    e, k, n = w_gate_up.shape
    tn = 2 * MOE_TF
    src = jnp.arange(PERM_W)
    dst = jnp.where(src % 2 == 0, src // 2, PERM_W // 2 + src // 2)
    perm = (dst[:, None] == jnp.arange(PERM_W)[None, :]).astype(BF16)
    return pl.pallas_call(
        _deinterleave_kernel,
        grid=(e, n // tn),
        in_specs=[pl.BlockSpec((1, k, tn), lambda i, j: (i, 0, j)),
                  pl.BlockSpec((PERM_W, PERM_W), lambda i, j: (0, 0))],
        out_specs=pl.BlockSpec((1, k, tn), lambda i, j: (i, 0, j)),
        out_shape=jax.ShapeDtypeStruct((e, k, n), BF16),
        compiler_params=_cparams(("arbitrary", "arbitrary")),
        name="moe_deinterleave",
    )(w_gate_up, perm)


def _inproj_kernel(x_ref, nw_ref, w_ref, ws_ref, o_ref, os_ref, n_scr):
    @pl.when(pl.program_id(1) == 0)
    def _():
        x = x_ref[...]
        n = x * lax.rsqrt(jnp.mean(x * x, axis=-1, keepdims=True) + EPS) * nw_ref[...]
        nb = n.astype(BF16)
        n_scr[...] = nb
        os_ref[...] = _dot(nb, ws_ref[...])

    o_ref[...] = _dot(n_scr[...], w_ref[...]).astype(BF16)


def _input_projection(xf, norm_w, w_big, w_small):
    t = xf.shape[0]
    return pl.pallas_call(
        _inproj_kernel,
        grid=(t // IN_TM, D_BIG // IN_TN),
        in_specs=[pl.BlockSpec((IN_TM, D_MODEL), lambda i, j: (i, 0)),
                  pl.BlockSpec((1, D_MODEL), lambda i, j: (0, 0)),
                  pl.BlockSpec((D_MODEL, IN_TN), lambda i, j: (0, j)),
                  pl.BlockSpec((D_MODEL, LANES), lambda i, j: (0, 0))],
        out_specs=[pl.BlockSpec((IN_TM, IN_TN), lambda i, j: (i, j)),
                   pl.BlockSpec((IN_TM, LANES), lambda i, j: (i, 0))],
        out_shape=[jax.ShapeDtypeStruct((t, D_BIG), BF16),
                   jax.ShapeDtypeStruct((t, LANES), F32)],
        scratch_shapes=[pltpu.VMEM((IN_TM, D_MODEL), BF16)],
        compiler_params=_cparams(("arbitrary", "arbitrary")),
        name="inproj",
    )(xf, norm_w, w_big, w_small)


def _ssd_kernel(z_ref, x_ref, bc_ref, sm_ref, cwx_ref, cwbc_ref, cbx_ref, cbbc_ref, dtb_ref, alog_ref,
                dskip_ref, nw_ref, e_ref, o_ref, extx, extbc, state, y_scr):
    q = CHUNK
    halo = SUBLANES

    @pl.when(pl.program_id(1) == 0)
    def _():
        extx[0:halo, :] = jnp.zeros((halo, extx.shape[1]), F32)
        extbc[0:halo, :] = jnp.zeros((halo, extbc.shape[1]), F32)
        state[...] = jnp.zeros_like(state)

    extx[halo:halo + q, :] = x_ref[0].astype(F32)
    extbc[halo:halo + q, :] = bc_ref[0].astype(F32)

    def conv_silu(ext, w_ref, b_ref):
        first = halo - (SSD_CONV - 1)
        acc = b_ref[...] + ext[first:first + q, :] * w_ref[0:1, :]
        for j in range(1, SSD_CONV):
            acc = acc + ext[first + j:first + j + q, :] * w_ref[j:j + 1, :]
        return acc * _sigmoid(acc)

    xs = conv_silu(extx, cwx_ref, cbx_ref)
    bcs = conv_silu(extbc, cwbc_ref, cbbc_ref)
    extx[0:halo, :] = extx[q:q + halo, :]
    extbc[0:halo, :] = extbc[q:q + halo, :]

    lane = lax.broadcasted_iota(I32, (1, LANES), 1)
    g = sm_ref[0]
    dt = _softplus(g + dtb_ref[...])
    a = jnp.where(lane < SSD_HEADS, -jnp.exp(alog_ref[...]), 0.0)
    da = dt * a
    tri = _tril(q).astype(F32)
    a_cum = _dot_exact(tri, da)
    a_cum_t = a_cum.T
    a_end = a_cum[q - 1:q, :]
    stack = jnp.concatenate(
        [dt, jnp.exp(a_end - a_cum), jnp.exp(a_cum), jnp.broadcast_to(jnp.exp(a_end), (SUBLANES, LANES))], axis=0)
    ex = _split3_dot(stack, e_ref[...])
    dt_x = ex[0:q]
    to_end_x = ex[q:2 * q]
    from_start_x = ex[2 * q:3 * q]
    chunk_decay_x = ex[3 * q:3 * q + 1]

    xdt = xs * dt_x
    xdt_b = xdt.astype(BF16)
    xde_b = (xdt * to_end_x).astype(BF16)
    causal = _tril(q)
    lane_q = lax.broadcasted_iota(I32, (1, LANES), 1)
    gw = SSD_STATE
    hpg = SSD_HEADS // SSD_GROUPS
    cpg = D_SSD // SSD_GROUPS
    for gi in range(SSD_GROUPS):
        bg = bcs[:, gi * gw:(gi + 1) * gw]
        cg_b = bcs[:, SSD_GROUPS * gw + gi * gw:SSD_GROUPS * gw + (gi + 1) * gw].astype(BF16)
        cb = lax.dot_general(cg_b, bg.astype(BF16), (((1,), (1,)), ((), ())), preferred_element_type=F32)
        bg_t = bg.T.astype(BF16)
        cs = slice(gi * cpg, (gi + 1) * cpg)
        prev = state[:, cs]
        y_off = _dot(cg_b, prev.astype(BF16)) * from_start_x[:, cs]
        state[:, cs] = prev * chunk_decay_x[:, cs] + _dot(bg_t, xde_b[:, cs])
        for pr in range(hpg // 2):
            c0 = gi * cpg + pr * LANES
            xp = xdt_b[:, c0:c0 + LANES]
            acc = y_off[:, pr * LANES:(pr + 1) * LANES]
            for sub in range(2):
                h = gi * hpg + pr * 2 + sub
                seg = a_cum[:, h:h + 1] - a_cum_t[h:h + 1, :]
                decay = jnp.exp(jnp.where(causal, seg, -jnp.inf))
                m = (cb * decay).astype(BF16)
                keep = (lane_q < SSD_HEAD_DIM) if sub == 0 else (lane_q >= SSD_HEAD_DIM)
                acc = acc + _dot(m, jnp.where(keep, xp, jnp.zeros_like(xp)))
            y_scr[:, c0:c0 + LANES] = acc

    y = y_scr[...] + xs * dskip_ref[...]
    zf = z_ref[0].astype(F32)
    y = y * (zf * _sigmoid(zf))
    for gi in range(SSD_GROUPS):
        cs = slice(gi * cpg, (gi + 1) * cpg)
        yg = y[:, cs]
        yg = yg * lax.rsqrt(jnp.mean(yg * yg, axis=-1, keepdims=True) + EPS)
        o_ref[0, :, cs] = (yg * nw_ref[:, cs]).astype(BF16)


def _ssd_mixer(proj, small, conv_w, conv_b, dt_bias, a_log, d_skip, norm_w, b, s):
    nc = s // CHUNK
    pad = LANES - SSD_HEADS
    dtb = jnp.pad(dt_bias, (0, pad)).reshape(1, LANES)
    alog = jnp.pad(a_log, (0, pad)).reshape(1, LANES)
    dskip = jnp.repeat(d_skip, SSD_HEAD_DIM).reshape(1, D_SSD)
    expand = (jnp.arange(LANES)[:, None] == (jnp.arange(D_SSD) // SSD_HEAD_DIM)[None, :]).astype(BF16)
    d_bc = 2 * SSD_GROUPS * SSD_STATE
    const = lambda shape: pl.BlockSpec(shape, lambda i, j: (0,) * len(shape))
    col = lambda width, off: pl.BlockSpec((1, CHUNK, width), lambda i, j: (i, j, off // width))
    return pl.pallas_call(
        _ssd_kernel,
        grid=(b, nc),
        in_specs=[col(D_SSD, COL_Z), col(D_SSD, COL_X), col(d_bc, COL_BC), col(LANES, 0),
                  const((SSD_CONV, D_SSD)), const((SSD_CONV, d_bc)), const((1, D_SSD)), const((1, d_bc)),
                  const((1, LANES)), const((1, LANES)), const((1, D_SSD)), const((1, D_SSD)),
                  const((LANES, D_SSD))],
        out_specs=pl.BlockSpec((1, CHUNK, D_SSD), lambda i, j: (i, j, 0)),
        out_shape=jax.ShapeDtypeStruct((b, s, D_SSD), BF16),
        scratch_shapes=[pltpu.VMEM((CHUNK + SUBLANES, D_SSD), F32),
                        pltpu.VMEM((CHUNK + SUBLANES, d_bc), F32),
                        pltpu.VMEM((SSD_STATE, D_SSD), F32),
                        pltpu.VMEM((CHUNK, D_SSD), F32)],
        compiler_params=_cparams(("arbitrary", "arbitrary")),
        name="ssd_mixer",
    )(proj, proj, proj, small, conv_w[:, :D_SSD], conv_w[:, D_SSD:], conv_b[:D_SSD].reshape(1, -1),
      conv_b[D_SSD:].reshape(1, -1), dtb, alog, dskip, norm_w.reshape(1, -1), expand)


def _mlstm_kernel(q_ref, k_ref, v_ref, o_ref, sm_ref, bias_ref, nw_ref, out_ref, c_scr, n_scr, m_scr):
    L = CHUNK

    @pl.when(pl.program_id(1) == 0)
    def _():
        c_scr[...] = jnp.zeros_like(c_scr)
        n_scr[...] = jnp.zeros_like(n_scr)
        m_scr[...] = jnp.zeros_like(m_scr)

    g = sm_ref[0] + bias_ref[...]
    log_f = jnp.minimum(g, 0.0) - jnp.log1p(jnp.exp(-jnp.abs(g)))
    tri = _tril(L).astype(F32)
    bcum = _dot_exact(tri, log_f)
    g_t = g.T
    bcum_t = bcum.T
    causal = _tril(L)
    scale = ML_DQK ** -0.5
    for h in range(ML_HEADS):
        li, lf = LANE_I + h, LANE_F + h
        li_col, li_row = g[:, li:li + 1], g_t[li:li + 1, :]
        b_col, b_row = bcum[:, lf:lf + 1], bcum_t[lf:lf + 1, :]
        btot = bcum[L - 1:L, lf:lf + 1]
        m_prev = m_scr[h:h + 1, 0:1]
        c_prev = c_scr[h]
        n_prev = n_scr[h:h + 1, :]
        qb = q_ref[0, :, h * ML_DQK:(h + 1) * ML_DQK]
        kb = k_ref[0, :, h * ML_DQK:(h + 1) * ML_DQK]
        vb = v_ref[0, :, h * ML_DV:(h + 1) * ML_DV]
        w_end = btot - b_col + li_col
        m_loc = jnp.max(w_end, axis=0, keepdims=True)
        k_end = kb.astype(F32) * jnp.exp(w_end - m_loc)
        kv = lax.dot_general(k_end.astype(BF16), vb, (((0,), (0,)), ((), ())), preferred_element_type=F32)
        nk = jnp.sum(k_end, axis=0, keepdims=True)
        d_log = jnp.where(causal, b_col - b_row + li_row, -jnp.inf)
        m_inter = b_col + m_prev
        m_t = jnp.maximum(m_inter, jnp.max(d_log, axis=1, keepdims=True))
        s_mat = lax.dot_general(qb, kb, (((1,), (1,)), ((), ())), preferred_element_type=F32)
        s_mat = s_mat * (scale * jnp.exp(d_log - m_t))
        inter = scale * jnp.exp(m_inter - m_t)
        num = _dot(s_mat.astype(BF16), vb) + _dot(qb, c_prev.astype(BF16)) * inter
        qn = jnp.sum(qb.astype(F32) * n_prev, axis=1, keepdims=True)
        den = jnp.sum(s_mat, axis=1, keepdims=True) + qn * inter
        den = jnp.maximum(jnp.abs(den), jnp.exp(-m_t))
        ht = num / den
        ht = ht * lax.rsqrt(jnp.mean(ht * ht, axis=-1, keepdims=True) + EPS)
        vs = slice(h * ML_DV, (h + 1) * ML_DV)
        og = _sigmoid(o_ref[0, :, vs].astype(F32))
        out_ref[0, :, vs] = (ht * nw_ref[:, vs] * og).astype(BF16)
        m_new = jnp.maximum(btot + m_prev, m_loc)
        fa = jnp.exp(btot + m_prev - m_new)
        fg = jnp.exp(m_loc - m_new)
        c_scr[h] = fa * c_prev + fg * kv
        n_scr[h:h + 1, :] = fa * n_prev + fg * nk
        m_scr[h:h + 1, :] = jnp.broadcast_to(m_new, (1, LANES))


def _mlstm_mixer(proj, small, i_bias, f_bias, norm_w, b, s):
    nc = s // CHUNK
    bias = jnp.zeros((LANES,), F32).at[LANE_I:LANE_I + ML_HEADS].set(i_bias).at[LANE_F:LANE_F + ML_HEADS].set(f_bias)
    d_qk = ML_HEADS * ML_DQK
    const = lambda shape: pl.BlockSpec(shape, lambda i, j: (0,) * len(shape))
    col = lambda width, off: pl.BlockSpec((1, CHUNK, width), lambda i, j: (i, j, off // width))
    return pl.pallas_call(
        _mlstm_kernel,
        grid=(b, nc),
        in_specs=[col(d_qk, COL_Q), col(d_qk, COL_K), col(D_ML, COL_V), col(D_ML, COL_O), col(LANES, 0),
                  const((1, LANES)), const((1, D_ML))],
        out_specs=pl.BlockSpec((1, CHUNK, D_ML), lambda i, j: (i, j, 0)),
        out_shape=jax.ShapeDtypeStruct((b, s, D_ML), BF16),
        scratch_shapes=[pltpu.VMEM((ML_HEADS, ML_DQK, ML_DV), F32),
                        pltpu.VMEM((ML_HEADS, ML_DQK), F32),
                        pltpu.VMEM((ML_HEADS, LANES), F32)],
        compiler_params=_cparams(("arbitrary", "arbitrary")),
        name="mlstm_mixer",
    )(proj, proj, proj, proj, small, bias.reshape(1, LANES), norm_w.reshape(1, -1))


def _outproj_router_kernel(x_ref, ys_ref, ym_ref, wo_ref, n2w_ref, rw_ref, rb_ref,
                           h1_ref, xnp_ref, seli_ref, selg_ref, cnt_ref, carry):
    tm = x_ref.shape[0]

    @pl.when(pl.program_id(0) == 0)
    def _():
        carry[...] = jnp.zeros_like(carry)

    h1 = x_ref[...] + _dot(ys_ref[...], wo_ref[0:D_SSD, :]) + _dot(ym_ref[...], wo_ref[D_SSD:, :])
    h1_ref[...] = h1
    xn = h1 * lax.rsqrt(jnp.mean(h1 * h1, axis=-1, keepdims=True) + EPS) * n2w_ref[...]
    packed = _pack_bf16_pairs(xn)
    for sub in range(X_SUB):
        xnp_ref[pl.ds(sub, tm, stride=X_SUB), :] = packed[:, sub * LANES:(sub + 1) * LANES]

    lane = lax.broadcasted_iota(I32, (tm, LANES), 1)
    lane_f = lane.astype(F32)
    xn_hi = xn.astype(BF16)
    xn_lo = (xn - xn_hi.astype(F32)).astype(BF16)
    hh_hl = _dot(xn_hi, rw_ref[...])
    logits = hh_hl[:, :LANES] + hh_hl[:, LANES:] + _dot(xn_lo, rw_ref[:, :LANES]) + rb_ref[...]
    work = jnp.where(lane < N_EXPERTS, logits, -jnp.inf)
    vals, idxs = [], []
    chosen = jnp.zeros((tm, LANES), jnp.bool_)
    for _ in range(TOP_K):
        m = jnp.max(work, axis=1, keepdims=True)
        idx = jnp.min(jnp.where(work == m, lane_f, float(LANES)), axis=1, keepdims=True)
        sel = lane_f == idx
        vals.append(m)
        idxs.append(idx)
        chosen = jnp.logical_or(chosen, sel)
        work = jnp.where(sel, -jnp.inf, work)
    exps = [jnp.exp(v - vals[0]) for v in vals]
    denom = exps[0] + exps[1] + exps[2] + exps[3]

    onehot = jnp.where(chosen, 1.0, 0.0)
    ranks = _dot(_tril(tm, strict=True).astype(BF16), onehot.astype(BF16)) + carry[...]
    total = carry[...] + jnp.sum(onehot, axis=0, keepdims=True)
    carry[...] = total
    cnt_ref[...] = jnp.broadcast_to(total, cnt_ref.shape)

    seli = jnp.zeros((tm, LANES), F32)
    selg = jnp.zeros((tm, LANES), F32)
    for k in range(TOP_K):
        rank_k = jnp.sum(jnp.where(lane_f == idxs[k], ranks, 0.0), axis=1, keepdims=True)
        seli = jnp.where(lane == k, idxs[k], seli)
        seli = jnp.where(lane == TOP_K + k, rank_k, seli)
        selg = jnp.where(lane == k, exps[k] / denom, selg)
    seli_ref[...] = seli.astype(I32)
    selg_ref[...] = selg


def _outproj_router(xf, y_ssd, y_ml, w_out_b, norm2_w, router_w_pad, router_b_pad):
    t = xf.shape[0]
    tm = OUT_TM
    row = lambda width: pl.BlockSpec((tm, width), lambda i: (i, 0))
    const = lambda shape: pl.BlockSpec(shape, lambda i: (0,) * len(shape))
    return pl.pallas_call(
        _outproj_router_kernel,
        grid=(t // tm,),
        in_specs=[row(D_MODEL), row(D_SSD), row(D_ML),
                  pl.BlockSpec((D_SSD + D_ML, D_MODEL), lambda i: (0, 0), pipeline_mode=pl.Buffered(1)),
                  const((1, D_MODEL)),
                  const((D_MODEL, 2 * LANES)), const((1, LANES))],
        out_specs=[row(D_MODEL), pl.BlockSpec((tm * X_SUB, LANES), lambda i: (i, 0)), row(LANES), row(LANES),
                   const((SUBLANES, LANES))],
        out_shape=[jax.ShapeDtypeStruct((t, D_MODEL), F32),
                   jax.ShapeDtypeStruct((t * X_SUB, LANES), U32),
                   jax.ShapeDtypeStruct((t, LANES), I32),
                   jax.ShapeDtypeStruct((t, LANES), F32),
                   jax.ShapeDtypeStruct((SUBLANES, LANES), F32)],
        scratch_shapes=[pltpu.VMEM((1, LANES), F32)],
        compiler_params=_cparams(("arbitrary",)),
        name="outproj_router",
    )(xf, y_ssd, y_ml, w_out_b, norm2_w, router_w_pad, router_b_pad)


def _route_kernel(seli_ref, cnt_ref, dest_ref, be_ref):
    tm = seli_ref.shape[0]
    nbp = be_ref.shape[0]
    lane1 = lax.broadcasted_iota(I32, (1, LANES), 1)
    cnt = jnp.where(lane1 < N_EXPERTS, cnt_ref[0:1, :], 0.0)
    padded = jnp.ceil(cnt / MOE_TM) * MOE_TM
    r = lax.broadcasted_iota(I32, (LANES, LANES), 0)
    c = lax.broadcasted_iota(I32, (LANES, LANES), 1)
    upper = jnp.where(r <= c, 1.0, 0.0)
    pend = jnp.round(_dot_exact(jnp.broadcast_to(padded, (SUBLANES, LANES)), upper))[0:1, :]
    pstart = pend - padded

    lane = lax.broadcasted_iota(I32, (tm, LANES), 1)
    lane_f = lane.astype(F32)
    seli = seli_ref[...].astype(F32)
    dest = jnp.zeros((tm, LANES), F32)
    for k in range(TOP_K):
        idx_k = jnp.sum(jnp.where(lane == k, seli, 0.0), axis=1, keepdims=True)
        rank_k = jnp.sum(jnp.where(lane == TOP_K + k, seli, 0.0), axis=1, keepdims=True)
        start_k = jnp.sum(jnp.where(lane_f == idx_k, pstart, 0.0), axis=1, keepdims=True)
        dest = jnp.where(lane == k, start_k + rank_k, dest)
    dest_ref[...] = dest.astype(I32)

    blk_start = (lax.broadcasted_iota(I32, (nbp, LANES), 0) * MOE_TM).astype(F32)
    lane_b = lax.broadcasted_iota(I32, (nbp, LANES), 1)
    passed = jnp.where(jnp.logical_and(lane_b < N_EXPERTS, pend <= blk_start), 1.0, 0.0)
    expert = jnp.minimum(jnp.sum(passed, axis=1, keepdims=True), float(N_EXPERTS - 1))
    n_active = pend[:, N_EXPERTS - 1:N_EXPERTS] / MOE_TM
    tok_end = jnp.sum(jnp.where(lane_b.astype(F32) == expert, pstart + cnt, 0.0), axis=1, keepdims=True)
    valid = jnp.clip(tok_end - blk_start[:, 0:1], 0.0, float(MOE_TM))
    n_sub = jnp.ceil(valid / MOE_SB)
    be = jnp.where(lane_b == 0, expert, jnp.where(lane_b == 1, n_active, jnp.where(lane_b == 2, n_sub, 0.0)))
    be_ref[...] = be.astype(I32)


def _route_offsets(seli, cnt, n_blocks):
    t = seli.shape[0]
    nbp = -(-n_blocks // SUBLANES) * SUBLANES
    return pl.pallas_call(
        _route_kernel,
        grid=(t // ROUTE_TM,),
        in_specs=[pl.BlockSpec((ROUTE_TM, LANES), lambda i: (i, 0)),
                  pl.BlockSpec((SUBLANES, LANES), lambda i: (0, 0))],
        out_specs=[pl.BlockSpec((ROUTE_TM, LANES), lambda i: (i, 0)),
                   pl.BlockSpec((nbp, LANES), lambda i: (0, 0))],
        out_shape=[jax.ShapeDtypeStruct((t, LANES), I32),
                   jax.ShapeDtypeStruct((nbp, LANES), I32)],
        compiler_params=_cparams(("arbitrary",)),
        name="route_offsets",
    )(seli, cnt)


def _dispatch_kernel(dest_ref, xnp_ref, rows_in_ref, rows_ref, sem):
    del rows_in_ref
    tt = xnp_ref.shape[0] // X_SUB

    def tile_copy(t, d):
        src = xnp_ref.at[pl.ds(pl.multiple_of(t * X_SUB, X_SUB), X_SUB)]
        dst = rows_ref.at[pl.ds(pl.multiple_of(d * X_SUB, X_SUB), X_SUB)]
        return pltpu.make_async_copy(src, dst, sem)

    def issue(t, carry):
        for k in range(TOP_K):
            tile_copy(t, dest_ref[t * TOP_K + k]).start()
        return carry

    lax.fori_loop(0, tt, issue, 0, unroll=4)
    for _ in range(TOP_K):
        pltpu.make_async_copy(xnp_ref, rows_ref.at[pl.ds(0, tt * X_SUB)], sem).wait()


def _dispatch(dest_flat, xnp, n_rows):
    t = xnp.shape[0] // X_SUB
    rows0 = jnp.zeros((n_rows * X_SUB, LANES), U32)
    return pl.pallas_call(
        _dispatch_kernel,
        grid=(t // DISP_TT,),
        in_specs=[pl.BlockSpec((DISP_TT * TOP_K,), lambda i: (i,), memory_space=pltpu.SMEM),
                  pl.BlockSpec((DISP_TT * X_SUB, LANES), lambda i: (i, 0)),
                  pl.BlockSpec(memory_space=pl.ANY)],
        out_specs=pl.BlockSpec(memory_space=pl.ANY),
        out_shape=jax.ShapeDtypeStruct((n_rows * X_SUB, LANES), U32),
        scratch_shapes=[pltpu.SemaphoreType.DMA(())],
        input_output_aliases={2: 0},
        compiler_params=_cparams(("arbitrary",)),
        name="moe_dispatch",
    )(dest_flat, xnp, rows0)


def _moe_kernel(be_ref, nsb_ref, na_ref, x_ref, wgu_ref, bgu_ref, wd_ref, bd_ref, o_ref, xb, acc):
    m = pl.program_id(0)
    f = pl.program_id(1)
    nf = pl.num_programs(1)
    n_sub = MOE_TM // MOE_SB

    def ffn(nsb):
        rows = nsb * MOE_SB

        @pl.when(f == 0)
        def _():
            _unpack_bf16_pairs(x_ref, rows, xb)
            acc[0:rows, :] = jnp.broadcast_to(bd_ref[0], (rows, D_MODEL))

        wd = wd_ref[0].astype(BF16)
        for i in range(nsb):
            rs = slice(i * MOE_SB, (i + 1) * MOE_SB)
            h = _dot(xb[rs, :], wgu_ref[0]) + bgu_ref[0]
            glu = jnp.minimum(h[:, :MOE_TF], SWIGLU_LIMIT)
            lin = jnp.clip(h[:, MOE_TF:], -SWIGLU_LIMIT, SWIGLU_LIMIT)
            act = glu * _sigmoid(SWIGLU_ALPHA * glu) * (lin + 1.0)
            acc[rs, :] += _dot(act.astype(BF16), wd)

        @pl.when(f == nf - 1)
        def _():
            packed = _pack_bf16_pairs(acc[0:rows, :])
            for sub in range(X_SUB):
                o_ref[pl.ds(sub, rows, stride=X_SUB), :] = packed[:, sub * LANES:(sub + 1) * LANES]
            if nsb < n_sub:
                o_ref[rows * X_SUB:, :] = jnp.zeros(((MOE_TM - rows) * X_SUB, LANES), U32)

    active = m < na_ref[0]
    for nsb in range(1, n_sub + 1):
        pl.when(jnp.logical_and(active, nsb_ref[m] == nsb))(functools.partial(ffn, nsb))

    @pl.when(jnp.logical_and(jnp.logical_not(active), f == nf - 1))
    def _():
        o_ref[...] = jnp.zeros_like(o_ref)


def _moe_ffn(block_expert, block_nsub, n_active, x_rows, wgu_b, bgu_p, w_down, b_down, n_blocks):
    nf = D_FF // MOE_TF

    def blk(m, na):
        return jnp.maximum(jnp.minimum(m, na[0] - 1), 0)

    def ftile(m, f, na):
        return jnp.where(m < na[0], f, nf - 1)

    def expert(m, be, na):
        return be[blk(m, na)]

    grid_spec = pltpu.PrefetchScalarGridSpec(
        num_scalar_prefetch=3,
        grid=(n_blocks, nf),
        in_specs=[pl.BlockSpec((MOE_TM * X_SUB, LANES), lambda m, f, be, ns, na: (blk(m, na), 0)),
                  pl.BlockSpec((1, D_MODEL, 2 * MOE_TF),
                               lambda m, f, be, ns, na: (expert(m, be, na), 0, ftile(m, f, na))),
                  pl.BlockSpec((1, 1, 2 * MOE_TF), lambda m, f, be, ns, na: (expert(m, be, na), 0, ftile(m, f, na))),
                  pl.BlockSpec((1, MOE_TF, D_MODEL), lambda m, f, be, ns, na: (expert(m, be, na), ftile(m, f, na), 0)),
                  pl.BlockSpec((1, 1, D_MODEL), lambda m, f, be, ns, na: (expert(m, be, na), 0, 0))],
        out_specs=pl.BlockSpec((MOE_TM * X_SUB, LANES), lambda m, f, be, ns, na: (m, 0)),
        scratch_shapes=[pltpu.VMEM((MOE_TM, D_MODEL), BF16), pltpu.VMEM((MOE_TM, D_MODEL), F32)],
    )
    return pl.pallas_call(
        _moe_kernel,
        grid_spec=grid_spec,
        out_shape=jax.ShapeDtypeStruct((n_blocks * MOE_TM * X_SUB, LANES), U32),
        compiler_params=_cparams(("arbitrary", "arbitrary")),
        name="moe_ffn",
    )(block_expert, block_nsub, n_active, x_rows, wgu_b, bgu_p, w_down, b_down)


def _combine_kernel(dest_ref, dest_next_ref, h1_ref, selg_ref, fw_ref, rows_ref, o_ref, buf, h_scr, sem):
    tt = h1_ref.shape[0]
    i = pl.program_id(0)
    n = pl.num_programs(0)
    half = D_MODEL // 2

    def row_copy(slot, t, k, d):
        src = rows_ref.at[pl.ds(pl.multiple_of(d * X_SUB, X_SUB), X_SUB)]
        dst = buf.at[slot, k, pl.ds(pl.multiple_of(t * X_SUB, X_SUB), X_SUB)]
        return pltpu.make_async_copy(src, dst, sem.at[slot])

    def gather(slot, d_ref):
        def issue(t, carry):
            for k in range(TOP_K):
                row_copy(slot, t, k, d_ref[t * TOP_K + k]).start()
            return carry

        lax.fori_loop(0, tt, issue, 0, unroll=4)

    slot = i % 2

    @pl.when(i == 0)
    def _():
        gather(0, dest_ref)

    @pl.when(i + 1 < n)
    def _():
        gather(1 - slot, dest_next_ref)

    for k in range(TOP_K):
        pltpu.make_async_copy(rows_ref.at[pl.ds(0, tt * X_SUB)], buf.at[slot, k], sem.at[slot]).wait()

    gates = selg_ref[...]
    for sub in range(X_SUB):
        lo_cs = slice(sub * LANES, (sub + 1) * LANES)
        hi_cs = slice(half + sub * LANES, half + (sub + 1) * LANES)
        h_lo = h1_ref[:, lo_cs]
        h_hi = h1_ref[:, hi_cs]
        for k in range(TOP_K):
            u = buf[slot, k, pl.ds(sub, tt, stride=X_SUB), :]
            g = gates[:, k:k + 1]
            h_lo = h_lo + g * pltpu.bitcast(u << 16, F32)
            h_hi = h_hi + g * pltpu.bitcast(u & jnp.uint32(0xFFFF0000), F32)
        h_scr[:, lo_cs] = h_lo
        h_scr[:, hi_cs] = h_hi
    h = h_scr[...]
    o_ref[...] = h * lax.rsqrt(jnp.mean(h * h, axis=-1, keepdims=True) + EPS) * fw_ref[...]


def _combine(dest_flat, h1, selg, final_w, y_rows):
    t = h1.shape[0]
    tt = COMB_TT
    n = t // tt
    return pl.pallas_call(
        _combine_kernel,
        grid=(n,),
        in_specs=[pl.BlockSpec((tt * TOP_K,), lambda i: (i,), memory_space=pltpu.SMEM),
                  pl.BlockSpec((tt * TOP_K,), lambda i: (jnp.minimum(i + 1, n - 1),), memory_space=pltpu.SMEM),
                  pl.BlockSpec((tt, D_MODEL), lambda i: (i, 0)),
                  pl.BlockSpec((tt, LANES), lambda i: (i, 0)),
                  pl.BlockSpec((1, D_MODEL), lambda i: (0, 0)),
                  pl.BlockSpec(memory_space=pl.ANY)],
        out_specs=pl.BlockSpec((tt, D_MODEL), lambda i: (i, 0)),
        out_shape=jax.ShapeDtypeStruct((t, D_MODEL), F32),
        scratch_shapes=[pltpu.VMEM((2, TOP_K, tt * X_SUB, LANES), U32), pltpu.VMEM((tt, D_MODEL), F32),
                        pltpu.SemaphoreType.DMA((2,))],
        compiler_params=_cparams(("arbitrary",)),
        name="moe_combine",
    )(dest_flat, dest_flat, h1, selg, final_w, y_rows)


def _permute_gate_up_cols(a):
    lead = a.shape[:-1]
    return a.reshape(*lead, -1, MOE_TF, 2).swapaxes(-1, -2).reshape(*lead, -1)


def kernel(x, norm1_w, w_in, conv_w, conv_b, dt_bias, a_log, d_skip, ssd_norm_w, i_bias, f_bias, ml_norm_w,
           w_out, norm2_w, router_w, router_b, w_gate_up, b_gate_up, w_down, b_down, final_norm_w):
    b, s, d = x.shape
    t = b * s
    assert w_in.shape[0] == 1, "single-layer problem"
    h = x.reshape(t, d)
    for l in range(1):
        wi = w_in[l]
        o_z, o_xbc, o_dt = 0, D_SSD, D_SSD + D_SSD + 2 * SSD_GROUPS * SSD_STATE
        o_q = o_dt + SSD_HEADS
        o_k = o_q + ML_HEADS * ML_DQK
        o_v = o_k + ML_HEADS * ML_DQK
        o_o = o_v + D_ML
        o_i = o_o + D_ML
        o_f = o_i + ML_HEADS
        w_big = jnp.concatenate(
            [wi[:, o_z:o_z + D_SSD], wi[:, o_xbc:o_xbc + D_SSD], wi[:, o_v:o_v + D_ML], wi[:, o_o:o_o + D_ML],
             wi[:, o_xbc + D_SSD:o_dt], wi[:, o_q:o_k], wi[:, o_k:o_v]], axis=1).astype(BF16)
        w_small = jnp.concatenate(
            [wi[:, o_dt:o_q], wi[:, o_i:o_f], wi[:, o_f:o_f + ML_HEADS],
             jnp.zeros((d, LANES - SSD_HEADS - 2 * ML_HEADS), F32)], axis=1).astype(BF16)

        proj, small = _input_projection(h, norm1_w[l].reshape(1, d), w_big, w_small)
        proj3 = proj.reshape(b, s, D_BIG)
        small3 = small.reshape(b, s, LANES)
        y_ssd = _ssd_mixer(proj3, small3, conv_w[l], conv_b[l], dt_bias[l], a_log[l], d_skip[l], ssd_norm_w[l], b, s)
        y_ml = _mlstm_mixer(proj3, small3, i_bias[l], f_bias[l], ml_norm_w[l], b, s)

        rw = jnp.pad(router_w[l], ((0, 0), (0, LANES - N_EXPERTS)))
        rw_hi = rw.astype(BF16)
        rw = jnp.concatenate([rw_hi, (rw - rw_hi.astype(F32)).astype(BF16)], axis=1)
        rb = jnp.pad(router_b[l], (0, LANES - N_EXPERTS)).reshape(1, LANES)
        h1, xnp, seli, selg, cnt = _outproj_router(
            h, y_ssd.reshape(t, D_SSD), y_ml.reshape(t, D_ML), w_out[l].astype(BF16), norm2_w[l].reshape(1, d), rw, rb)

        n_blocks = (t * TOP_K) // MOE_TM + N_EXPERTS
        dest, table = _route_offsets(seli, cnt, n_blocks)
        dest_flat = dest[:, :TOP_K].reshape(-1)
        block_expert = table[:n_blocks, 0]
        n_active = table[0:1, 1]
        block_nsub = table[:n_blocks, 2]

        x_rows = _dispatch(dest_flat, xnp, n_blocks * MOE_TM)
        wgu_b = _deinterleave_gate_up(w_gate_up[l])
        bgu_p = _permute_gate_up_cols(b_gate_up[l]).reshape(N_EXPERTS, 1, 2 * D_FF)
        y_rows = _moe_ffn(block_expert, block_nsub, n_active, x_rows, wgu_b, bgu_p, w_down[l],
                          b_down[l].reshape(N_EXPERTS, 1, d), n_blocks)
        h = _combine(dest_flat, h1, selg, final_norm_w.reshape(1, d), y_rows)
    return h.reshape(b, s, d)
```

```python
import functools

import jax
import jax.numpy as jnp
from jax import lax
from jax.experimental import pallas as pl
from jax.experimental.pallas import tpu as pltpu

F32 = jnp.float32
BF16 = jnp.bfloat16
I32 = jnp.int32
U32 = jnp.uint32

LANES = 128
SUBLANES = 8
VMEM_LIMIT = 56 * 1024 * 1024

D_MODEL = 2048
EPS = 1e-5
SSD_HEAD_DIM = 64
SSD_HEADS = 32
D_SSD = 2048
SSD_GROUPS = 4
SSD_STATE = 128
SSD_CONV = 4
CHUNK = 128
ML_HEADS = 8
ML_DV = 256
ML_DQK = 128
D_ML = 2048
N_EXPERTS = 32
TOP_K = 4
D_FF = 2048
SWIGLU_ALPHA = 1.702
SWIGLU_LIMIT = 7.0

COL_Z, COL_X, COL_V, COL_O = 0, 2048, 4096, 6144
COL_BC, COL_Q, COL_K = 8192, 9216, 10240
D_BIG = 11264
LANE_DT, LANE_I, LANE_F = 0, 32, 40

IN_TM, IN_TN = 1024, 1024
OUT_TM = 512
ROUTE_TM = 2048
DISP_TT = 512
MOE_TM = 512
MOE_SB = 256
MOE_TF = 1024
PERM_W = 256
PREP_TN = 1024
COMB_TT = 256
X_SUB = D_MODEL // 2 // LANES


def _cparams(sem, vmem=VMEM_LIMIT):
    return pltpu.CompilerParams(dimension_semantics=sem, vmem_limit_bytes=vmem)


def _dot(a, b):
    return jnp.dot(a, b, preferred_element_type=F32)


def _dot_exact(a, b):
    return jnp.dot(a, b, preferred_element_type=F32, precision=lax.Precision.HIGHEST)


def _split3_dot(x, e_bf16):
    x0 = x.astype(BF16)
    r1 = x - x0.astype(F32)
    x1 = r1.astype(BF16)
    x2 = (r1 - x1.astype(F32)).astype(BF16)
    return _dot(x0, e_bf16) + _dot(x1, e_bf16) + _dot(x2, e_bf16)


def _sigmoid(x):
    return 1.0 / (1.0 + jnp.exp(-x))


def _softplus(x):
    return jnp.maximum(x, 0.0) + jnp.log1p(jnp.exp(-jnp.abs(x)))


def _tril(n, strict=False):
    r = lax.broadcasted_iota(I32, (n, n), 0)
    c = lax.broadcasted_iota(I32, (n, n), 1)
    return (r > c) if strict else (r >= c)


def _pack_bf16_pairs(y):
    half = D_MODEL // 2
    lo = pltpu.bitcast(y[:, :half].astype(BF16).astype(F32), U32)
    hi = pltpu.bitcast(y[:, half:].astype(BF16).astype(F32), U32)
    return hi | (lo >> 16)


def _unpack_bf16_pairs(src_ref, rows, dst_ref):
    half = D_MODEL // 2
    for sub in range(X_SUB):
        u = src_ref[pl.ds(sub, rows, stride=X_SUB), :]
        dst_ref[0:rows, sub * LANES:(sub + 1) * LANES] = pltpu.bitcast(u << 16, F32).astype(BF16)
        dst_ref[0:rows, half + sub * LANES:half + (sub + 1) * LANES] = (
            pltpu.bitcast(u & jnp.uint32(0xFFFF0000), F32).astype(BF16))


def _deinterleave_tile(w_ref, p_ref, glu_ref, lin_ref):
    hw = PERM_W // 2
    for s in range(w_ref.shape[2] // PERM_W):
        t = _dot(w_ref[0, :, s * PERM_W:(s + 1) * PERM_W].astype(BF16), p_ref[...])
        glu_ref[0, :, s * hw:(s + 1) * hw] = t[:, :hw].astype(BF16)
        lin_ref[0, :, s * hw:(s + 1) * hw] = t[:, hw:].astype(BF16)


def _deinterleave_perm():
    src = jnp.arange(PERM_W)
    dst = jnp.where(src % 2 == 0, src // 2, PERM_W // 2 + src // 2)
    return (dst[:, None] == jnp.arange(PERM_W)[None, :]).astype(BF16)


def _deinterleave_specs(w_gate_up, tile_index):
    e, k, n = w_gate_up.shape

    def at(*ids):
        ei, ci = tile_index(*ids)
        return ei, 0, ci

    in_specs = [pl.BlockSpec((1, k, PREP_TN), at), pl.BlockSpec((PERM_W, PERM_W), lambda *ids: (0, 0))]
    out_specs = [pl.BlockSpec((1, k, PREP_TN // 2), at)] * 2
    out_shape = [jax.ShapeDtypeStruct((e, k, n // 2), BF16)] * 2
    return in_specs, out_specs, out_shape


def _deinterleave_gate_up(w_gate_up):
    e, _, n = w_gate_up.shape
    in_specs, out_specs, out_shape = _deinterleave_specs(w_gate_up, lambda i, j: (i, j))
    return pl.pallas_call(
        _deinterleave_tile,
        grid=(e, n // PREP_TN),
        in_specs=in_specs,
        out_specs=out_specs,
        out_shape=out_shape,
        compiler_params=_cparams(("arbitrary", "arbitrary")),
        name="moe_deinterleave",
    )(w_gate_up, _deinterleave_perm())


def _inproj_kernel(x_ref, nw_ref, w_ref, ws_ref, o_ref, os_ref, n_scr):
    @pl.when(pl.program_id(1) == 0)
    def _():
        x = x_ref[...]
        n = x * lax.rsqrt(jnp.mean(x * x, axis=-1, keepdims=True) + EPS) * nw_ref[...]
        nb = n.astype(BF16)
        n_scr[...] = nb
        os_ref[...] = _dot(nb, ws_ref[...])

    o_ref[...] = _dot(n_scr[...], w_ref[...]).astype(BF16)


def _input_projection(xf, norm_w, w_big, w_small):
    t = xf.shape[0]
    return pl.pallas_call(
        _inproj_kernel,
        grid=(t // IN_TM, D_BIG // IN_TN),
        in_specs=[pl.BlockSpec((IN_TM, D_MODEL), lambda i, j: (i, 0)),
                  pl.BlockSpec((1, D_MODEL), lambda i, j: (0, 0)),
                  pl.BlockSpec((D_MODEL, IN_TN), lambda i, j: (0, j)),
                  pl.BlockSpec((D_MODEL, LANES), lambda i, j: (0, 0))],
        out_specs=[pl.BlockSpec((IN_TM, IN_TN), lambda i, j: (i, j)),
                   pl.BlockSpec((IN_TM, LANES), lambda i, j: (i, 0))],
        out_shape=[jax.ShapeDtypeStruct((t, D_BIG), BF16),
                   jax.ShapeDtypeStruct((t, LANES), F32)],
        scratch_shapes=[pltpu.VMEM((IN_TM, D_MODEL), BF16)],
        compiler_params=_cparams(("arbitrary", "arbitrary")),
        name="inproj",
    )(xf, norm_w, w_big, w_small)


def _ssd_kernel(z_ref, x_ref, bc_ref, sm_ref, cw_ref, cb_ref, shift_ref, dtb_ref, alog_ref,
                dskip_ref, nw_ref, e_ref, o_ref, ext, state, y_scr):
    q = CHUNK

    @pl.when(pl.program_id(1) == 0)
    def _():
        ext[0:q, :] = jnp.zeros((q, ext.shape[1]), BF16)
        state[...] = jnp.zeros_like(state)

    ext[q:2 * q, 0:D_SSD] = x_ref[0]
    ext[q:2 * q, D_SSD:] = bc_ref[0]
    shifted = _dot(shift_ref[...], ext[...])
    acc = cb_ref[...] + ext[q:2 * q, :].astype(F32) * cw_ref[SSD_CONV - 1:SSD_CONV, :]
    for j in range(SSD_CONV - 1):
        acc = acc + shifted[j * q:(j + 1) * q, :] * cw_ref[j:j + 1, :]
    xbc = acc * _sigmoid(acc)
    xs = xbc[:, :D_SSD]
    bcs = xbc[:, D_SSD:]
    ext[0:q, :] = ext[q:2 * q, :]

    lane = lax.broadcasted_iota(I32, (1, LANES), 1)
    g = sm_ref[0]
    dt = _softplus(g + dtb_ref[...])
    a = jnp.where(lane < SSD_HEADS, -jnp.exp(alog_ref[...]), 0.0)
    da = dt * a
    tri = _tril(q).astype(F32)
    a_cum = _dot_exact(tri, da)
    a_cum_t = a_cum.T
    a_end = a_cum[q - 1:q, :]
    stack = jnp.concatenate(
        [dt, jnp.exp(a_end - a_cum), jnp.exp(a_cum), jnp.broadcast_to(jnp.exp(a_end), (SUBLANES, LANES))], axis=0)
    ex = _split3_dot(stack, e_ref[...])
    dt_x = ex[0:q]
    to_end_x = ex[q:2 * q]
    from_start_x = ex[2 * q:3 * q]
    chunk_decay_x = ex[3 * q:3 * q + 1]

    xdt = xs * dt_x
    xdt_b = xdt.astype(BF16)
    xde_b = (xdt * to_end_x).astype(BF16)
    causal = _tril(q)
    lane_q = lax.broadcasted_iota(I32, (1, LANES), 1)
    gw = SSD_STATE
    hpg = SSD_HEADS // SSD_GROUPS
    cpg = D_SSD // SSD_GROUPS
    for gi in range(SSD_GROUPS):
        bg = bcs[:, gi * gw:(gi + 1) * gw]
        cg_b = bcs[:, SSD_GROUPS * gw + gi * gw:SSD_GROUPS * gw + (gi + 1) * gw].astype(BF16)
        cb = lax.dot_general(cg_b, bg.astype(BF16), (((1,), (1,)), ((), ())), preferred_element_type=F32)
        bg_t = bg.T.astype(BF16)
        cs = slice(gi * cpg, (gi + 1) * cpg)
        prev = state[:, cs]
        y_off = _dot(cg_b, prev.astype(BF16)) * from_start_x[:, cs]
        state[:, cs] = prev * chunk_decay_x[:, cs] + _dot(bg_t, xde_b[:, cs])
        for pr in range(hpg // 2):
            c0 = gi * cpg + pr * LANES
            xp = xdt_b[:, c0:c0 + LANES]
            acc = y_off[:, pr * LANES:(pr + 1) * LANES]
            for sub in range(2):
                h = gi * hpg + pr * 2 + sub
                seg = a_cum[:, h:h + 1] - a_cum_t[h:h + 1, :]
                decay = jnp.exp(jnp.where(causal, seg, -jnp.inf))
                m = (cb * decay).astype(BF16)
                keep = (lane_q < SSD_HEAD_DIM) if sub == 0 else (lane_q >= SSD_HEAD_DIM)
                acc = acc + _dot(m, jnp.where(keep, xp, jnp.zeros_like(xp)))
            y_scr[:, c0:c0 + LANES] = acc

    y = y_scr[...] + xs * dskip_ref[...]
    zf = z_ref[0].astype(F32)
    y = y * (zf * _sigmoid(zf))
    for gi in range(SSD_GROUPS):
        cs = slice(gi * cpg, (gi + 1) * cpg)
        yg = y[:, cs]
        yg = yg * lax.rsqrt(jnp.mean(yg * yg, axis=-1, keepdims=True) + EPS)
        o_ref[0, :, cs] = (yg * nw_ref[:, cs]).astype(BF16)


def _ssd_mixer(proj, small, conv_w, conv_b, dt_bias, a_log, d_skip, norm_w, b, s):
    nc = s // CHUNK
    pad = LANES - SSD_HEADS
    dtb = jnp.pad(dt_bias, (0, pad)).reshape(1, LANES)
    alog = jnp.pad(a_log, (0, pad)).reshape(1, LANES)
    dskip = jnp.repeat(d_skip, SSD_HEAD_DIM).reshape(1, D_SSD)
    expand = (jnp.arange(LANES)[:, None] == (jnp.arange(D_SSD) // SSD_HEAD_DIM)[None, :]).astype(BF16)
    d_bc = 2 * SSD_GROUPS * SSD_STATE
    d_conv = D_SSD + d_bc
    tt = jnp.arange((SSD_CONV - 1) * CHUNK)
    back = (SSD_CONV - 1) - tt // CHUNK
    shift = ((CHUNK + tt % CHUNK - back)[:, None] == jnp.arange(2 * CHUNK)[None, :]).astype(BF16)
    const = lambda shape: pl.BlockSpec(shape, lambda i, j: (0,) * len(shape))
    col = lambda width, off: pl.BlockSpec((1, CHUNK, width), lambda i, j: (i, j, off // width))
    return pl.pallas_call(
        _ssd_kernel,
        grid=(b, nc),
        in_specs=[col(D_SSD, COL_Z), col(D_SSD, COL_X), col(d_bc, COL_BC), col(LANES, 0),
                  const((SSD_CONV, d_conv)), const((1, d_conv)), const(((SSD_CONV - 1) * CHUNK, 2 * CHUNK)),
                  const((1, LANES)), const((1, LANES)), const((1, D_SSD)), const((1, D_SSD)),
                  const((LANES, D_SSD))],
        out_specs=pl.BlockSpec((1, CHUNK, D_SSD), lambda i, j: (i, j, 0)),
        out_shape=jax.ShapeDtypeStruct((b, s, D_SSD), BF16),
        scratch_shapes=[pltpu.VMEM((2 * CHUNK, d_conv), BF16),
                        pltpu.VMEM((SSD_STATE, D_SSD), F32),
                        pltpu.VMEM((CHUNK, D_SSD), F32)],
        compiler_params=_cparams(("arbitrary", "arbitrary")),
        name="ssd_mixer",
    )(proj, proj, proj, small, conv_w, conv_b.reshape(1, -1), shift, dtb, alog, dskip, norm_w.reshape(1, -1), expand)


def _mlstm_kernel(q_ref, k_ref, v_ref, o_ref, sm_ref, bias_ref, nw_ref, *rest, prep_gate_up):
    L = CHUNK
    if prep_gate_up:
        wg_ref, perm_ref, out_ref, glu_ref, lin_ref, c_scr, n_scr, m_scr = rest
        _deinterleave_tile(wg_ref, perm_ref, glu_ref, lin_ref)
    else:
        out_ref, c_scr, n_scr, m_scr = rest

    @pl.when(pl.program_id(1) == 0)
    def _():
        c_scr[...] = jnp.zeros_like(c_scr)
        n_scr[...] = jnp.zeros_like(n_scr)
        m_scr[...] = jnp.zeros_like(m_scr)

    g = sm_ref[0] + bias_ref[...]
    log_f = jnp.minimum(g, 0.0) - jnp.log1p(jnp.exp(-jnp.abs(g)))
    tri = _tril(L).astype(F32)
    bcum = _dot_exact(tri, log_f)
    g_t = g.T
    bcum_t = bcum.T
    causal = _tril(L)
    scale = ML_DQK ** -0.5
    for h in range(ML_HEADS):
        li, lf = LANE_I + h, LANE_F + h
        li_col, li_row = g[:, li:li + 1], g_t[li:li + 1, :]
        b_col, b_row = bcum[:, lf:lf + 1], bcum_t[lf:lf + 1, :]
        btot = bcum[L - 1:L, lf:lf + 1]
        m_prev = m_scr[h:h + 1, 0:1]
        c_prev = c_scr[h]
        n_prev = n_scr[h:h + 1, :]
        qb = q_ref[0, :, h * ML_DQK:(h + 1) * ML_DQK]
        kb = k_ref[0, :, h * ML_DQK:(h + 1) * ML_DQK]
        vb = v_ref[0, :, h * ML_DV:(h + 1) * ML_DV]
        w_end = btot - b_col + li_col
        m_loc = jnp.max(w_end, axis=0, keepdims=True)
        k_end = kb.astype(F32) * jnp.exp(w_end - m_loc)
        kv = lax.dot_general(k_end.astype(BF16), vb, (((0,), (0,)), ((), ())), preferred_element_type=F32)
        nk = jnp.sum(k_end, axis=0, keepdims=True)
        d_log = jnp.where(causal, b_col - b_row + li_row, -jnp.inf)
        m_inter = b_col + m_prev
        m_t = jnp.maximum(m_inter, jnp.max(d_log, axis=1, keepdims=True))
        s_mat = lax.dot_general(qb, kb, (((1,), (1,)), ((), ())), preferred_element_type=F32)
        s_mat = s_mat * (scale * jnp.exp(d_log - m_t))
        inter = scale * jnp.exp(m_inter - m_t)
        num = _dot(s_mat.astype(BF16), vb) + _dot(qb, c_prev.astype(BF16)) * inter
        qn = jnp.sum(qb.astype(F32) * n_prev, axis=1, keepdims=True)
        den = jnp.sum(s_mat, axis=1, keepdims=True) + qn * inter
        den = jnp.maximum(jnp.abs(den), jnp.exp(-m_t))
        ht = num / den
        ht = ht * lax.rsqrt(jnp.mean(ht * ht, axis=-1, keepdims=True) + EPS)
        vs = slice(h * ML_DV, (h + 1) * ML_DV)
        og = _sigmoid(o_ref[0, :, vs].astype(F32))
        out_ref[0, :, vs] = (ht * nw_ref[:, vs] * og).astype(BF16)
        m_new = jnp.maximum(btot + m_prev, m_loc)
        fa = jnp.exp(btot + m_prev - m_new)
        fg = jnp.exp(m_loc - m_new)
        c_scr[h] = fa * c_prev + fg * kv
        n_scr[h:h + 1, :] = fa * n_prev + fg * nk
        m_scr[h:h + 1, :] = jnp.broadcast_to(m_new, (1, LANES))


def _mlstm_mixer(proj, small, i_bias, f_bias, norm_w, b, s, w_gate_up):
    nc = s // CHUNK
    bias = jnp.zeros((LANES,), F32).at[LANE_I:LANE_I + ML_HEADS].set(i_bias).at[LANE_F:LANE_F + ML_HEADS].set(f_bias)
    d_qk = ML_HEADS * ML_DQK
    const = lambda shape: pl.BlockSpec(shape, lambda i, j: (0,) * len(shape))
    col = lambda width, off: pl.BlockSpec((1, CHUNK, width), lambda i, j: (i, j, off // width))
    e, _, n = w_gate_up.shape
    tiles_per_expert = n // PREP_TN
    prep = b * nc == e * tiles_per_expert
    in_specs = [col(d_qk, COL_Q), col(d_qk, COL_K), col(D_ML, COL_V), col(D_ML, COL_O), col(LANES, 0),
                const((1, LANES)), const((1, D_ML))]
    out_specs = [pl.BlockSpec((1, CHUNK, D_ML), lambda i, j: (i, j, 0))]
    out_shape = [jax.ShapeDtypeStruct((b, s, D_ML), BF16)]
    args = [proj, proj, proj, proj, small, bias.reshape(1, LANES), norm_w.reshape(1, -1)]
    if prep:
        w_in_specs, w_out_specs, w_out_shape = _deinterleave_specs(
            w_gate_up, lambda i, j: ((i * nc + j) // tiles_per_expert, (i * nc + j) % tiles_per_expert))
        in_specs += w_in_specs
        out_specs += w_out_specs
        out_shape += w_out_shape
        args += [w_gate_up, _deinterleave_perm()]
    outs = pl.pallas_call(
        functools.partial(_mlstm_kernel, prep_gate_up=prep),
        grid=(b, nc),
        in_specs=in_specs,
        out_specs=out_specs,
        out_shape=out_shape,
        scratch_shapes=[pltpu.VMEM((ML_HEADS, ML_DQK, ML_DV), F32),
                        pltpu.VMEM((ML_HEADS, ML_DQK), F32),
                        pltpu.VMEM((ML_HEADS, LANES), F32)],
        compiler_params=_cparams(("arbitrary", "arbitrary")),
        name="mlstm_mixer",
    )(*args)
    if prep:
        return outs[0], outs[1], outs[2]
    return (outs[0], *_deinterleave_gate_up(w_gate_up))


def _outproj_router_kernel(x_ref, ys_ref, ym_ref, wo_ref, n2w_ref, rw_ref, rb_ref,
                           h1_ref, xnp_ref, seli_ref, selg_ref, cnt_ref, carry):
    tm = x_ref.shape[0]

    @pl.when(pl.program_id(0) == 0)
    def _():
        carry[...] = jnp.zeros_like(carry)

    h1 = x_ref[...] + _dot(ys_ref[...], wo_ref[0:D_SSD, :]) + _dot(ym_ref[...], wo_ref[D_SSD:, :])
    h1_ref[...] = h1
    xn = h1 * lax.rsqrt(jnp.mean(h1 * h1, axis=-1, keepdims=True) + EPS) * n2w_ref[...]
    packed = _pack_bf16_pairs(xn)
    for sub in range(X_SUB):
        xnp_ref[pl.ds(sub, tm, stride=X_SUB), :] = packed[:, sub * LANES:(sub + 1) * LANES]

    lane = lax.broadcasted_iota(I32, (tm, LANES), 1)
    lane_f = lane.astype(F32)
    xn_hi = xn.astype(BF16)
    xn_lo = (xn - xn_hi.astype(F32)).astype(BF16)
    hh_hl = _dot(xn_hi, rw_ref[...])
    logits = hh_hl[:, :LANES] + hh_hl[:, LANES:] + _dot(xn_lo, rw_ref[:, :LANES]) + rb_ref[...]
    work = jnp.where(lane < N_EXPERTS, logits, -jnp.inf)
    vals, idxs = [], []
    chosen = jnp.zeros((tm, LANES), jnp.bool_)
    for _ in range(TOP_K):
        m = jnp.max(work, axis=1, keepdims=True)
        idx = jnp.min(jnp.where(work == m, lane_f, float(LANES)), axis=1, keepdims=True)
        sel = lane_f == idx
        vals.append(m)
        idxs.append(idx)
        chosen = jnp.logical_or(chosen, sel)
        work = jnp.where(sel, -jnp.inf, work)
    exps = [jnp.exp(v - vals[0]) for v in vals]
    denom = exps[0] + exps[1] + exps[2] + exps[3]

    onehot = jnp.where(chosen, 1.0, 0.0)
    ranks = _dot(_tril(tm, strict=True).astype(BF16), onehot.astype(BF16)) + carry[...]
    total = carry[...] + jnp.sum(onehot, axis=0, keepdims=True)
    carry[...] = total
    cnt_ref[...] = jnp.broadcast_to(total, cnt_ref.shape)

    seli = jnp.zeros((tm, LANES), F32)
    selg = jnp.zeros((tm, LANES), F32)
    for k in range(TOP_K):
        rank_k = jnp.sum(jnp.where(lane_f == idxs[k], ranks, 0.0), axis=1, keepdims=True)
        seli = jnp.where(lane == k, idxs[k], seli)
        seli = jnp.where(lane == TOP_K + k, rank_k, seli)
        selg = jnp.where(lane == k, exps[k] / denom, selg)
    seli_ref[...] = seli.astype(I32)
    selg_ref[...] = selg


def _outproj_router(xf, y_ssd, y_ml, w_out_b, norm2_w, router_w_pad, router_b_pad):
    t = xf.shape[0]
    tm = OUT_TM
    row = lambda width: pl.BlockSpec((tm, width), lambda i: (i, 0))
    const = lambda shape: pl.BlockSpec(shape, lambda i: (0,) * len(shape))
    return pl.pallas_call(
        _outproj_router_kernel,
        grid=(t // tm,),
        in_specs=[row(D_MODEL), row(D_SSD), row(D_ML),
                  pl.BlockSpec((D_SSD + D_ML, D_MODEL), lambda i: (0, 0), pipeline_mode=pl.Buffered(1)),
                  const((1, D_MODEL)),
                  const((D_MODEL, 2 * LANES)), const((1, LANES))],
        out_specs=[row(D_MODEL), pl.BlockSpec((tm * X_SUB, LANES), lambda i: (i, 0)), row(LANES), row(LANES),
                   const((SUBLANES, LANES))],
        out_shape=[jax.ShapeDtypeStruct((t, D_MODEL), F32),
                   jax.ShapeDtypeStruct((t * X_SUB, LANES), U32),
                   jax.ShapeDtypeStruct((t, LANES), I32),
                   jax.ShapeDtypeStruct((t, LANES), F32),
                   jax.ShapeDtypeStruct((SUBLANES, LANES), F32)],
        scratch_shapes=[pltpu.VMEM((1, LANES), F32)],
        compiler_params=_cparams(("arbitrary",)),
        name="outproj_router",
    )(xf, y_ssd, y_ml, w_out_b, norm2_w, router_w_pad, router_b_pad)


def _route_kernel(seli_ref, cnt_ref, dest_ref, be_ref):
    tm = seli_ref.shape[0]
    nbp = be_ref.shape[0]
    lane1 = lax.broadcasted_iota(I32, (1, LANES), 1)
    cnt = jnp.where(lane1 < N_EXPERTS, cnt_ref[0:1, :], 0.0)
    padded = jnp.ceil(cnt / MOE_TM) * MOE_TM
    r = lax.broadcasted_iota(I32, (LANES, LANES), 0)
    c = lax.broadcasted_iota(I32, (LANES, LANES), 1)
    upper = jnp.where(r <= c, 1.0, 0.0)
    pend = jnp.round(_dot_exact(jnp.broadcast_to(padded, (SUBLANES, LANES)), upper))[0:1, :]
    pstart = pend - padded

    lane = lax.broadcasted_iota(I32, (tm, LANES), 1)
    lane_f = lane.astype(F32)
    seli = seli_ref[...].astype(F32)
    dest = jnp.zeros((tm, LANES), F32)
    for k in range(TOP_K):
        idx_k = jnp.sum(jnp.where(lane == k, seli, 0.0), axis=1, keepdims=True)
        rank_k = jnp.sum(jnp.where(lane == TOP_K + k, seli, 0.0), axis=1, keepdims=True)
        start_k = jnp.sum(jnp.where(lane_f == idx_k, pstart, 0.0), axis=1, keepdims=True)
        dest = jnp.where(lane == k, start_k + rank_k, dest)
    dest_ref[...] = dest.astype(I32)

    blk_start = (lax.broadcasted_iota(I32, (nbp, LANES), 0) * MOE_TM).astype(F32)
    lane_b = lax.broadcasted_iota(I32, (nbp, LANES), 1)
    passed = jnp.where(jnp.logical_and(lane_b < N_EXPERTS, pend <= blk_start), 1.0, 0.0)
    expert = jnp.minimum(jnp.sum(passed, axis=1, keepdims=True), float(N_EXPERTS - 1))
    n_active = pend[:, N_EXPERTS - 1:N_EXPERTS] / MOE_TM
    tok_end = jnp.sum(jnp.where(lane_b.astype(F32) == expert, pstart + cnt, 0.0), axis=1, keepdims=True)
    valid = jnp.clip(tok_end - blk_start[:, 0:1], 0.0, float(MOE_TM))
    n_sub = jnp.ceil(valid / MOE_SB)
    be = jnp.where(lane_b == 0, expert, jnp.where(lane_b == 1, n_active, jnp.where(lane_b == 2, n_sub, 0.0)))
    be_ref[...] = be.astype(I32)


def _route_offsets(seli, cnt, n_blocks):
    t = seli.shape[0]
    nbp = -(-n_blocks // SUBLANES) * SUBLANES
    return pl.pallas_call(
        _route_kernel,
        grid=(t // ROUTE_TM,),
        in_specs=[pl.BlockSpec((ROUTE_TM, LANES), lambda i: (i, 0)),
                  pl.BlockSpec((SUBLANES, LANES), lambda i: (0, 0))],
        out_specs=[pl.BlockSpec((ROUTE_TM, LANES), lambda i: (i, 0)),
                   pl.BlockSpec((nbp, LANES), lambda i: (0, 0))],
        out_shape=[jax.ShapeDtypeStruct((t, LANES), I32),
                   jax.ShapeDtypeStruct((nbp, LANES), I32)],
        compiler_params=_cparams(("arbitrary",)),
        name="route_offsets",
    )(seli, cnt)


def _dispatch_kernel(dest_ref, xnp_ref, rows_in_ref, rows_ref, sem):
    del rows_in_ref
    tt = xnp_ref.shape[0] // X_SUB

    def tile_copy(t, d):
        src = xnp_ref.at[pl.ds(pl.multiple_of(t * X_SUB, X_SUB), X_SUB)]
        dst = rows_ref.at[pl.ds(pl.multiple_of(d * X_SUB, X_SUB), X_SUB)]
        return pltpu.make_async_copy(src, dst, sem)

    def issue(t, carry):
        for k in range(TOP_K):
            tile_copy(t, dest_ref[t * TOP_K + k]).start()
        return carry

    lax.fori_loop(0, tt, issue, 0, unroll=4)
    for _ in range(TOP_K):
        pltpu.make_async_copy(xnp_ref, rows_ref.at[pl.ds(0, tt * X_SUB)], sem).wait()


def _dispatch(dest_flat, xnp, n_rows):
    t = xnp.shape[0] // X_SUB
    rows0 = jnp.zeros((n_rows * X_SUB, LANES), U32)
    return pl.pallas_call(
        _dispatch_kernel,
        grid=(t // DISP_TT,),
        in_specs=[pl.BlockSpec((DISP_TT * TOP_K,), lambda i: (i,), memory_space=pltpu.SMEM),
                  pl.BlockSpec((DISP_TT * X_SUB, LANES), lambda i: (i, 0)),
                  pl.BlockSpec(memory_space=pl.ANY)],
        out_specs=pl.BlockSpec(memory_space=pl.ANY),
        out_shape=jax.ShapeDtypeStruct((n_rows * X_SUB, LANES), U32),
        scratch_shapes=[pltpu.SemaphoreType.DMA(())],
        input_output_aliases={2: 0},
        compiler_params=_cparams(("arbitrary",)),
        name="moe_dispatch",
    )(dest_flat, xnp, rows0)


def _moe_kernel(be_ref, nsb_ref, na_ref, x_ref, wglu_ref, wlin_ref, bglu_ref, blin_ref, wd_ref, bd_ref, o_ref,
                xb, acc):
    m = pl.program_id(0)
    f = pl.program_id(1)
    nf = pl.num_programs(1)
    n_sub = MOE_TM // MOE_SB

    def ffn(nsb):
        rows = nsb * MOE_SB

        @pl.when(f == 0)
        def _():
            _unpack_bf16_pairs(x_ref, rows, xb)
            acc[0:rows, :] = jnp.broadcast_to(bd_ref[0], (rows, D_MODEL))

        wd = wd_ref[0].astype(BF16)
        for i in range(nsb):
            rs = slice(i * MOE_SB, (i + 1) * MOE_SB)
            glu = jnp.minimum(_dot(xb[rs, :], wglu_ref[0]) + bglu_ref[0], SWIGLU_LIMIT)
            lin = jnp.clip(_dot(xb[rs, :], wlin_ref[0]) + blin_ref[0], -SWIGLU_LIMIT, SWIGLU_LIMIT)
            act = glu * _sigmoid(SWIGLU_ALPHA * glu) * (lin + 1.0)
            acc[rs, :] += _dot(act.astype(BF16), wd)

        @pl.when(f == nf - 1)
        def _():
            packed = _pack_bf16_pairs(acc[0:rows, :])
            for sub in range(X_SUB):
                o_ref[pl.ds(sub, rows, stride=X_SUB), :] = packed[:, sub * LANES:(sub + 1) * LANES]
            if nsb < n_sub:
                o_ref[rows * X_SUB:, :] = jnp.zeros(((MOE_TM - rows) * X_SUB, LANES), U32)

    active = m < na_ref[0]
    for nsb in range(1, n_sub + 1):
        pl.when(jnp.logical_and(active, nsb_ref[m] == nsb))(functools.partial(ffn, nsb))

    @pl.when(jnp.logical_and(jnp.logical_not(active), f == nf - 1))
    def _():
        o_ref[...] = jnp.zeros_like(o_ref)


def _moe_ffn(block_expert, block_nsub, n_active, x_rows, w_glu, w_lin, b_glu, b_lin, w_down, b_down, n_blocks):
    nf = D_FF // MOE_TF

    def blk(m, na):
        return jnp.maximum(jnp.minimum(m, na[0] - 1), 0)

    def ftile(m, f, na):
        return jnp.where(m < na[0], f, nf - 1)

    def expert(m, be, na):
        return be[blk(m, na)]

    col_tile = lambda m, f, be, ns, na: (expert(m, be, na), 0, ftile(m, f, na))
    grid_spec = pltpu.PrefetchScalarGridSpec(
        num_scalar_prefetch=3,
        grid=(n_blocks, nf),
        in_specs=[pl.BlockSpec((MOE_TM * X_SUB, LANES), lambda m, f, be, ns, na: (blk(m, na), 0)),
                  pl.BlockSpec((1, D_MODEL, MOE_TF), col_tile),
                  pl.BlockSpec((1, D_MODEL, MOE_TF), col_tile),
                  pl.BlockSpec((1, 1, MOE_TF), col_tile),
                  pl.BlockSpec((1, 1, MOE_TF), col_tile),
                  pl.BlockSpec((1, MOE_TF, D_MODEL), lambda m, f, be, ns, na: (expert(m, be, na), ftile(m, f, na), 0)),
                  pl.BlockSpec((1, 1, D_MODEL), lambda m, f, be, ns, na: (expert(m, be, na), 0, 0))],
        out_specs=pl.BlockSpec((MOE_TM * X_SUB, LANES), lambda m, f, be, ns, na: (m, 0)),
        scratch_shapes=[pltpu.VMEM((MOE_TM, D_MODEL), BF16), pltpu.VMEM((MOE_TM, D_MODEL), F32)],
    )
    return pl.pallas_call(
        _moe_kernel,
        grid_spec=grid_spec,
        out_shape=jax.ShapeDtypeStruct((n_blocks * MOE_TM * X_SUB, LANES), U32),
        compiler_params=_cparams(("arbitrary", "arbitrary")),
        name="moe_ffn",
    )(block_expert, block_nsub, n_active, x_rows, w_glu, w_lin, b_glu, b_lin, w_down, b_down)


def _combine_kernel(dest_ref, dest_next_ref, h1_ref, selg_ref, fw_ref, rows_ref, o_ref, buf, h_scr, sem):
    tt = h1_ref.shape[0]
    i = pl.program_id(0)
    n = pl.num_programs(0)
    half = D_MODEL // 2

    def row_copy(slot, t, k, d):
        src = rows_ref.at[pl.ds(pl.multiple_of(d * X_SUB, X_SUB), X_SUB)]
        dst = buf.at[slot, k, pl.ds(pl.multiple_of(t * X_SUB, X_SUB), X_SUB)]
        return pltpu.make_async_copy(src, dst, sem.at[slot])

    def gather(slot, d_ref):
        def issue(t, carry):
            for k in range(TOP_K):
                row_copy(slot, t, k, d_ref[t * TOP_K + k]).start()
            return carry

        lax.fori_loop(0, tt, issue, 0, unroll=4)

    slot = i % 2

    @pl.when(i == 0)
    def _():
        gather(0, dest_ref)

    @pl.when(i + 1 < n)
    def _():
        gather(1 - slot, dest_next_ref)

    for k in range(TOP_K):
        pltpu.make_async_copy(rows_ref.at[pl.ds(0, tt * X_SUB)], buf.at[slot, k], sem.at[slot]).wait()

    gates = selg_ref[...]
    for sub in range(X_SUB):
        lo_cs = slice(sub * LANES, (sub + 1) * LANES)
        hi_cs = slice(half + sub * LANES, half + (sub + 1) * LANES)
        h_lo = h1_ref[:, lo_cs]
        h_hi = h1_ref[:, hi_cs]
        for k in range(TOP_K):
            u = buf[slot, k, pl.ds(sub, tt, stride=X_SUB), :]
            g = gates[:, k:k + 1]
            h_lo = h_lo + g * pltpu.bitcast(u << 16, F32)
            h_hi = h_hi + g * pltpu.bitcast(u & jnp.uint32(0xFFFF0000), F32)
        h_scr[:, lo_cs] = h_lo
        h_scr[:, hi_cs] = h_hi
    h = h_scr[...]
    o_ref[...] = h * lax.rsqrt(jnp.mean(h * h, axis=-1, keepdims=True) + EPS) * fw_ref[...]


def _combine(dest_flat, h1, selg, final_w, y_rows):
    t = h1.shape[0]
    tt = COMB_TT
    n = t // tt
    return pl.pallas_call(
        _combine_kernel,
        grid=(n,),
        in_specs=[pl.BlockSpec((tt * TOP_K,), lambda i: (i,), memory_space=pltpu.SMEM),
                  pl.BlockSpec((tt * TOP_K,), lambda i: (jnp.minimum(i + 1, n - 1),), memory_space=pltpu.SMEM),
                  pl.BlockSpec((tt, D_MODEL), lambda i: (i, 0)),
                  pl.BlockSpec((tt, LANES), lambda i: (i, 0)),
                  pl.BlockSpec((1, D_MODEL), lambda i: (0, 0)),
                  pl.BlockSpec(memory_space=pl.ANY)],
        out_specs=pl.BlockSpec((tt, D_MODEL), lambda i: (i, 0)),
        out_shape=jax.ShapeDtypeStruct((t, D_MODEL), F32),
        scratch_shapes=[pltpu.VMEM((2, TOP_K, tt * X_SUB, LANES), U32), pltpu.VMEM((tt, D_MODEL), F32),
                        pltpu.SemaphoreType.DMA((2,))],
        compiler_params=_cparams(("arbitrary",)),
        name="moe_combine",
    )(dest_flat, dest_flat, h1, selg, final_w, y_rows)


def kernel(x, norm1_w, w_in, conv_w, conv_b, dt_bias, a_log, d_skip, ssd_norm_w, i_bias, f_bias, ml_norm_w,
           w_out, norm2_w, router_w, router_b, w_gate_up, b_gate_up, w_down, b_down, final_norm_w):
    b, s, d = x.shape
    t = b * s
    assert w_in.shape[0] == 1, "single-layer problem"
    h = x.reshape(t, d)
    for l in range(1):
        wi = w_in[l]
        o_z, o_xbc, o_dt = 0, D_SSD, D_SSD + D_SSD + 2 * SSD_GROUPS * SSD_STATE
        o_q = o_dt + SSD_HEADS
        o_k = o_q + ML_HEADS * ML_DQK
        o_v = o_k + ML_HEADS * ML_DQK
        o_o = o_v + D_ML
        o_i = o_o + D_ML
        o_f = o_i + ML_HEADS
        w_big = jnp.concatenate(
            [wi[:, o_z:o_z + D_SSD], wi[:, o_xbc:o_xbc + D_SSD], wi[:, o_v:o_v + D_ML], wi[:, o_o:o_o + D_ML],
             wi[:, o_xbc + D_SSD:o_dt], wi[:, o_q:o_k], wi[:, o_k:o_v]], axis=1).astype(BF16)
        w_small = jnp.concatenate(
            [wi[:, o_dt:o_q], wi[:, o_i:o_f], wi[:, o_f:o_f + ML_HEADS],
             jnp.zeros((d, LANES - SSD_HEADS - 2 * ML_HEADS), F32)], axis=1).astype(BF16)

        proj, small = _input_projection(h, norm1_w[l].reshape(1, d), w_big, w_small)
        proj3 = proj.reshape(b, s, D_BIG)
        small3 = small.reshape(b, s, LANES)
        y_ssd = _ssd_mixer(proj3, small3, conv_w[l], conv_b[l], dt_bias[l], a_log[l], d_skip[l], ssd_norm_w[l], b, s)
        y_ml, w_glu, w_lin = _mlstm_mixer(proj3, small3, i_bias[l], f_bias[l], ml_norm_w[l], b, s, w_gate_up[l])

        rw = jnp.pad(router_w[l], ((0, 0), (0, LANES - N_EXPERTS)))
        rw_hi = rw.astype(BF16)
        rw = jnp.concatenate([rw_hi, (rw - rw_hi.astype(F32)).astype(BF16)], axis=1)
        rb = jnp.pad(router_b[l], (0, LANES - N_EXPERTS)).reshape(1, LANES)
        h1, xnp, seli, selg, cnt = _outproj_router(
            h, y_ssd.reshape(t, D_SSD), y_ml.reshape(t, D_ML), w_out[l].astype(BF16), norm2_w[l].reshape(1, d), rw, rb)

        n_blocks = (t * TOP_K) // MOE_TM + N_EXPERTS
        dest, table = _route_offsets(seli, cnt, n_blocks)
        dest_flat = dest[:, :TOP_K].reshape(-1)
        block_expert = table[:n_blocks, 0]
        n_active = table[0:1, 1]
        block_nsub = table[:n_blocks, 2]

        x_rows = _dispatch(dest_flat, xnp, n_blocks * MOE_TM)
        b_glu = b_gate_up[l][:, 0::2].reshape(N_EXPERTS, 1, D_FF)
        b_lin = b_gate_up[l][:, 1::2].reshape(N_EXPERTS, 1, D_FF)
        y_rows = _moe_ffn(block_expert, block_nsub, n_active, x_rows, w_glu, w_lin, b_glu, b_lin, w_down[l],
                          b_down[l].reshape(N_EXPERTS, 1, d), n_blocks)
        h = _combine(dest_flat, h1, selg, final_norm_w.reshape(1, d), y_rows)
    return h.reshape(b, s, d)
```

```python
import functools

import jax
import jax.numpy as jnp
from jax import lax
from jax.experimental import pallas as pl
from jax.experimental.pallas import tpu as pltpu

F32 = jnp.float32
BF16 = jnp.bfloat16
I32 = jnp.int32
U32 = jnp.uint32

LANES = 128
SUBLANES = 8
VMEM_LIMIT = 56 * 1024 * 1024

D_MODEL = 2048
EPS = 1e-5
SSD_HEAD_DIM = 64
SSD_HEADS = 32
D_SSD = 2048
SSD_GROUPS = 4
SSD_STATE = 128
SSD_CONV = 4
CHUNK = 128
ML_HEADS = 8
ML_DV = 256
ML_DQK = 128
D_ML = 2048
N_EXPERTS = 32
TOP_K = 4
D_FF = 2048
SWIGLU_ALPHA = 1.702
SWIGLU_LIMIT = 7.0

COL_Z, COL_X, COL_V, COL_O = 0, 2048, 4096, 6144
COL_BC, COL_Q, COL_K = 8192, 9216, 10240
D_BIG = 11264
LANE_DT, LANE_I, LANE_F = 0, 32, 40

IN_TM, IN_TN = 1024, 1024
OUT_TM = 512
ROUTE_TM = 2048
DISP_TT = 512
MOE_TM = 512
MOE_SB = 256
MOE_TF = 1024
PERM_W = 256
PREP_TN = 1024
COMB_TT = 256
X_SUB = D_MODEL // 2 // LANES


def _cparams(sem, vmem=VMEM_LIMIT):
    return pltpu.CompilerParams(dimension_semantics=sem, vmem_limit_bytes=vmem)


def _dot(a, b):
    return jnp.dot(a, b, preferred_element_type=F32)


def _dot_exact(a, b):
    return jnp.dot(a, b, preferred_element_type=F32, precision=lax.Precision.HIGHEST)


def _split2_dot(x, e2_bf16):
    hi = x.astype(BF16)
    lo = (x - hi.astype(F32)).astype(BF16)
    return _dot(jnp.concatenate([hi, lo], axis=1), e2_bf16)


def _sigmoid(x):
    return 1.0 / (1.0 + jnp.exp(-x))


def _softplus(x):
    return jnp.maximum(x, 0.0) + jnp.log1p(jnp.exp(-jnp.abs(x)))


def _tril(n, strict=False):
    r = lax.broadcasted_iota(I32, (n, n), 0)
    c = lax.broadcasted_iota(I32, (n, n), 1)
    return (r > c) if strict else (r >= c)


def _pack_bf16_pairs(y):
    half = D_MODEL // 2
    lo = pltpu.bitcast(y[:, :half].astype(BF16).astype(F32), U32)
    hi = pltpu.bitcast(y[:, half:].astype(BF16).astype(F32), U32)
    return hi | (lo >> 16)


def _unpack_bf16_pairs(src_ref, rows, dst_ref):
    half = D_MODEL // 2
    for sub in range(X_SUB):
        u = src_ref[pl.ds(sub, rows, stride=X_SUB), :]
        dst_ref[0:rows, sub * LANES:(sub + 1) * LANES] = pltpu.bitcast(u << 16, F32).astype(BF16)
        dst_ref[0:rows, half + sub * LANES:half + (sub + 1) * LANES] = (
            pltpu.bitcast(u & jnp.uint32(0xFFFF0000), F32).astype(BF16))


def _deinterleave_tile(w_ref, p_ref, glu_ref, lin_ref):
    hw = PERM_W // 2
    for s in range(w_ref.shape[2] // PERM_W):
        t = _dot(w_ref[0, :, s * PERM_W:(s + 1) * PERM_W].astype(BF16), p_ref[...])
        glu_ref[0, :, s * hw:(s + 1) * hw] = t[:, :hw].astype(BF16)
        lin_ref[0, :, s * hw:(s + 1) * hw] = t[:, hw:].astype(BF16)


def _deinterleave_perm():
    src = jnp.arange(PERM_W)
    dst = jnp.where(src % 2 == 0, src // 2, PERM_W // 2 + src // 2)
    return (dst[:, None] == jnp.arange(PERM_W)[None, :]).astype(BF16)


def _deinterleave_specs(w_gate_up, tile_index):
    e, k, n = w_gate_up.shape

    def at(*ids):
        ei, ci = tile_index(*ids)
        return ei, 0, ci

    in_specs = [pl.BlockSpec((1, k, PREP_TN), at), pl.BlockSpec((PERM_W, PERM_W), lambda *ids: (0, 0))]
    out_specs = [pl.BlockSpec((1, k, PREP_TN // 2), at)] * 2
    out_shape = [jax.ShapeDtypeStruct((e, k, n // 2), BF16)] * 2
    return in_specs, out_specs, out_shape


def _deinterleave_gate_up(w_gate_up):
    e, _, n = w_gate_up.shape
    in_specs, out_specs, out_shape = _deinterleave_specs(w_gate_up, lambda i, j: (i, j))
    return pl.pallas_call(
        _deinterleave_tile,
        grid=(e, n // PREP_TN),
        in_specs=in_specs,
        out_specs=out_specs,
        out_shape=out_shape,
        compiler_params=_cparams(("arbitrary", "arbitrary")),
        name="moe_deinterleave",
    )(w_gate_up, _deinterleave_perm())


def _inproj_kernel(x_ref, nw_ref, w_ref, ws_ref, o_ref, os_ref, n_scr):
    @pl.when(pl.program_id(1) == 0)
    def _():
        x = x_ref[...]
        n = x * lax.rsqrt(jnp.mean(x * x, axis=-1, keepdims=True) + EPS) * nw_ref[...]
        nb = n.astype(BF16)
        n_scr[...] = nb
        os_ref[...] = _dot(nb, ws_ref[...])

    o_ref[...] = _dot(n_scr[...], w_ref[...]).astype(BF16)


def _input_projection(xf, norm_w, w_big, w_small):
    t = xf.shape[0]
    return pl.pallas_call(
        _inproj_kernel,
        grid=(t // IN_TM, D_BIG // IN_TN),
        in_specs=[pl.BlockSpec((IN_TM, D_MODEL), lambda i, j: (i, 0)),
                  pl.BlockSpec((1, D_MODEL), lambda i, j: (0, 0)),
                  pl.BlockSpec((D_MODEL, IN_TN), lambda i, j: (0, j)),
                  pl.BlockSpec((D_MODEL, LANES), lambda i, j: (0, 0))],
        out_specs=[pl.BlockSpec((IN_TM, IN_TN), lambda i, j: (i, j)),
                   pl.BlockSpec((IN_TM, LANES), lambda i, j: (i, 0))],
        out_shape=[jax.ShapeDtypeStruct((t, D_BIG), BF16),
                   jax.ShapeDtypeStruct((t, LANES), F32)],
        scratch_shapes=[pltpu.VMEM((IN_TM, D_MODEL), BF16)],
        compiler_params=_cparams(("arbitrary", "arbitrary")),
        name="inproj",
    )(xf, norm_w, w_big, w_small)


def _ssd_kernel(z_ref, x_ref, bc_ref, sm_ref, cw_ref, cb_ref, shift_ref, dtb_ref, alog_ref,
                dskip_ref, nw_ref, e_ref, o_ref, ext, y_scr, *states):
    q = CHUNK

    @pl.when(pl.program_id(1) == 0)
    def _():
        ext[0:q, :] = jnp.zeros((q, ext.shape[1]), BF16)
        for state in states:
            state[...] = jnp.zeros_like(state)

    ext[q:2 * q, 0:D_SSD] = x_ref[0]
    ext[q:2 * q, D_SSD:] = bc_ref[0]
    shifted = _dot(shift_ref[...], ext[...])
    acc = cb_ref[...] + ext[q:2 * q, :].astype(F32) * cw_ref[SSD_CONV - 1:SSD_CONV, :]
    for j in range(SSD_CONV - 1):
        acc = acc + shifted[j * q:(j + 1) * q, :] * cw_ref[j:j + 1, :]
    xbc = acc * _sigmoid(acc)
    xs = xbc[:, :D_SSD]
    bcs = xbc[:, D_SSD:]
    ext[0:q, :] = ext[q:2 * q, :]

    lane = lax.broadcasted_iota(I32, (1, LANES), 1)
    g = sm_ref[0]
    dt = _softplus(g + dtb_ref[...])
    a = jnp.where(lane < SSD_HEADS, -jnp.exp(alog_ref[...]), 0.0)
    da = dt * a
    tri = _tril(q).astype(F32)
    a_cum = _dot_exact(tri, da)
    a_cum_t = a_cum.T
    a_end = a_cum[q - 1:q, :]
    stack = jnp.concatenate(
        [dt, jnp.exp(a_end - a_cum), jnp.exp(a_cum), jnp.broadcast_to(jnp.exp(a_end), (SUBLANES, LANES))], axis=0)
    ex = _split2_dot(stack, e_ref[...])
    dt_x = ex[0:q]
    to_end_x = ex[q:2 * q]
    from_start_x = ex[2 * q:3 * q]
    chunk_decay_x = ex[3 * q:3 * q + 1]

    xdt = xs * dt_x
    xdt_b = xdt.astype(BF16)
    xde_b = (xdt * to_end_x).astype(BF16)
    causal = _tril(q)
    lane_q = lax.broadcasted_iota(I32, (1, LANES), 1)
    gw = SSD_STATE
    hpg = SSD_HEADS // SSD_GROUPS
    cpg = D_SSD // SSD_GROUPS
    for gi in range(SSD_GROUPS):
        bg = bcs[:, gi * gw:(gi + 1) * gw]
        cg_b = bcs[:, SSD_GROUPS * gw + gi * gw:SSD_GROUPS * gw + (gi + 1) * gw].astype(BF16)
        cb = lax.dot_general(cg_b, bg.astype(BF16), (((1,), (1,)), ((), ())), preferred_element_type=F32)
        bg_t = bg.T.astype(BF16)
        cs = slice(gi * cpg, (gi + 1) * cpg)
        prev = states[gi][...]
        y_off = _dot(cg_b, prev.astype(BF16)) * from_start_x[:, cs]
        states[gi][...] = prev * chunk_decay_x[:, cs] + _dot(bg_t, xde_b[:, cs])
        for pr in range(hpg // 2):
            c0 = gi * cpg + pr * LANES
            xp = xdt_b[:, c0:c0 + LANES]
            acc = y_off[:, pr * LANES:(pr + 1) * LANES]
            for sub in range(2):
                h = gi * hpg + pr * 2 + sub
                seg = a_cum[:, h:h + 1] - a_cum_t[h:h + 1, :]
                decay = jnp.exp(jnp.where(causal, seg, -jnp.inf))
                m = (cb * decay).astype(BF16)
                keep = (lane_q < SSD_HEAD_DIM) if sub == 0 else (lane_q >= SSD_HEAD_DIM)
                acc = acc + _dot(m, jnp.where(keep, xp, jnp.zeros_like(xp)))
            y_scr[:, c0:c0 + LANES] = acc

    y = y_scr[...] + xs * dskip_ref[...]
    zf = z_ref[0].astype(F32)
    y = y * (zf * _sigmoid(zf))
    for gi in range(SSD_GROUPS):
        cs = slice(gi * cpg, (gi + 1) * cpg)
        yg = y[:, cs]
        yg = yg * lax.rsqrt(jnp.mean(yg * yg, axis=-1, keepdims=True) + EPS)
        o_ref[0, :, cs] = (yg * nw_ref[:, cs]).astype(BF16)


def _ssd_mixer(proj, small, conv_w, conv_b, dt_bias, a_log, d_skip, norm_w, b, s):
    nc = s // CHUNK
    pad = LANES - SSD_HEADS
    dtb = jnp.pad(dt_bias, (0, pad)).reshape(1, LANES)
    alog = jnp.pad(a_log, (0, pad)).reshape(1, LANES)
    dskip = jnp.repeat(d_skip, SSD_HEAD_DIM).reshape(1, D_SSD)
    expand = (jnp.arange(LANES)[:, None] == (jnp.arange(D_SSD) // SSD_HEAD_DIM)[None, :]).astype(BF16)
    expand = jnp.concatenate([expand, expand], axis=0)
    d_bc = 2 * SSD_GROUPS * SSD_STATE
    d_conv = D_SSD + d_bc
    tt = jnp.arange((SSD_CONV - 1) * CHUNK)
    back = (SSD_CONV - 1) - tt // CHUNK
    shift = ((CHUNK + tt % CHUNK - back)[:, None] == jnp.arange(2 * CHUNK)[None, :]).astype(BF16)
    const = lambda shape: pl.BlockSpec(shape, lambda i, j: (0,) * len(shape))
    col = lambda width, off: pl.BlockSpec((1, CHUNK, width), lambda i, j: (i, j, off // width))
    return pl.pallas_call(
        _ssd_kernel,
        grid=(b, nc),
        in_specs=[col(D_SSD, COL_Z), col(D_SSD, COL_X), col(d_bc, COL_BC), col(LANES, 0),
                  const((SSD_CONV, d_conv)), const((1, d_conv)), const(((SSD_CONV - 1) * CHUNK, 2 * CHUNK)),
                  const((1, LANES)), const((1, LANES)), const((1, D_SSD)), const((1, D_SSD)),
                  const((2 * LANES, D_SSD))],
        out_specs=pl.BlockSpec((1, CHUNK, D_SSD), lambda i, j: (i, j, 0)),
        out_shape=jax.ShapeDtypeStruct((b, s, D_SSD), BF16),
        scratch_shapes=[pltpu.VMEM((2 * CHUNK, d_conv), BF16),
                        pltpu.VMEM((CHUNK, D_SSD), F32)]
        + [pltpu.VMEM((SSD_STATE, D_SSD // SSD_GROUPS), F32)] * SSD_GROUPS,
        compiler_params=_cparams(("arbitrary", "arbitrary")),
        name="ssd_mixer",
    )(proj, proj, proj, small, conv_w, conv_b.reshape(1, -1), shift, dtb, alog, dskip, norm_w.reshape(1, -1), expand)


def _mlstm_kernel(q_ref, k_ref, v_ref, o_ref, sm_ref, bias_ref, nw_ref, *rest, prep_gate_up):
    L = CHUNK
    if prep_gate_up:
        wg_ref, perm_ref, out_ref, glu_ref, lin_ref, n_scr, m_scr, *ct_scrs = rest
        _deinterleave_tile(wg_ref, perm_ref, glu_ref, lin_ref)
    else:
        out_ref, n_scr, m_scr, *ct_scrs = rest

    @pl.when(pl.program_id(1) == 0)
    def _():
        for ct_scr in ct_scrs:
            ct_scr[...] = jnp.zeros_like(ct_scr)
        n_scr[...] = jnp.zeros_like(n_scr)
        m_scr[...] = jnp.zeros_like(m_scr)

    n_all = n_scr[...]
    m_all = m_scr[...]
    n_new_all = n_all
    m_new_all = m_all
    head_row = lax.broadcasted_iota(I32, (ML_HEADS, LANES), 0)

    g = sm_ref[0] + bias_ref[...]
    log_f = jnp.minimum(g, 0.0) - jnp.log1p(jnp.exp(-jnp.abs(g)))
    tri = _tril(L).astype(F32)
    bcum = _dot_exact(tri, log_f)
    bcum_t = bcum.T
    keys_before = jnp.logical_not(_tril(L, strict=True))
    row16 = lax.broadcasted_iota(I32, (2 * SUBLANES, ML_DQK), 0)
    scale = ML_DQK ** -0.5
    nt = (((1,), (1,)), ((), ()))
    tn = (((0,), (0,)), ((), ()))
    for h in range(ML_HEADS):
        li, lf = LANE_I + h, LANE_F + h
        li_col, b_col, b_row = g[:, li:li + 1], bcum[:, lf:lf + 1], bcum_t[lf:lf + 1, :]
        btot = bcum[L - 1:L, lf:lf + 1]
        m_prev = m_all[h:h + 1, 0:1]
        ct_prev = ct_scrs[h][...]
        n_prev = n_all[h:h + 1, :]
        qb = q_ref[0, :, h * ML_DQK:(h + 1) * ML_DQK]
        kb = k_ref[0, :, h * ML_DQK:(h + 1) * ML_DQK]
        vb = v_ref[0, :, h * ML_DV:(h + 1) * ML_DV]
        w_end = btot - b_col + li_col
        m_loc = jnp.max(w_end, axis=0, keepdims=True)
        k_end = kb.astype(F32) * jnp.exp(w_end - m_loc)
        kv_t = lax.dot_general(vb, k_end.astype(BF16), tn, preferred_element_type=F32)
        nk = jnp.sum(k_end, axis=0, keepdims=True)
        d_log = jnp.where(keys_before, (li_col - b_col) + b_row, -jnp.inf)
        m_inter = b_row + m_prev
        m_t = jnp.maximum(m_inter, jnp.max(d_log, axis=0, keepdims=True))
        s_t = lax.dot_general(kb, qb, nt, preferred_element_type=F32)
        s_t = s_t * (scale * jnp.exp(d_log - m_t))
        inter = scale * jnp.exp(m_inter - m_t)
        num_t = (lax.dot_general(vb, s_t.astype(BF16), tn, preferred_element_type=F32)
                 + lax.dot_general(ct_prev.astype(BF16), qb, nt, preferred_element_type=F32) * inter)
        n_hi = n_prev.astype(BF16).astype(F32)
        n_rows = jnp.where(row16 == 0, n_hi, jnp.where(row16 == 1, n_prev - n_hi, 0.0)).astype(BF16)
        qn = lax.dot_general(n_rows, qb, nt, preferred_element_type=F32)
        den = jnp.sum(s_t, axis=0, keepdims=True) + (qn[0:1, :] + qn[1:2, :]) * inter
        den = jnp.maximum(jnp.abs(den), jnp.exp(-m_t))
        ht_t = num_t / den
        ht_t = ht_t * lax.rsqrt(jnp.mean(ht_t * ht_t, axis=0, keepdims=True) + EPS)
        vs = slice(h * ML_DV, (h + 1) * ML_DV)
        og = _sigmoid(o_ref[0, :, vs].astype(F32))
        out_ref[0, :, vs] = (ht_t.T * nw_ref[:, vs] * og).astype(BF16)
        m_new = jnp.maximum(btot + m_prev, m_loc)
        fa = jnp.exp(btot + m_prev - m_new)
        fg = jnp.exp(m_loc - m_new)
        ct_scrs[h][...] = fa * ct_prev + fg * kv_t
        n_new_all = jnp.where(head_row == h, fa * n_prev + fg * nk, n_new_all)
        m_new_all = jnp.where(head_row == h, m_new, m_new_all)
    n_scr[...] = n_new_all
    m_scr[...] = m_new_all


def _mlstm_mixer(proj, small, i_bias, f_bias, norm_w, b, s, w_gate_up):
    nc = s // CHUNK
    bias = jnp.zeros((LANES,), F32).at[LANE_I:LANE_I + ML_HEADS].set(i_bias).at[LANE_F:LANE_F + ML_HEADS].set(f_bias)
    d_qk = ML_HEADS * ML_DQK
    const = lambda shape: pl.BlockSpec(shape, lambda i, j: (0,) * len(shape))
    col = lambda width, off: pl.BlockSpec((1, CHUNK, width), lambda i, j: (i, j, off // width))
    e, _, n = w_gate_up.shape
    tiles_per_expert = n // PREP_TN
    prep = b * nc == e * tiles_per_expert
    in_specs = [col(d_qk, COL_Q), col(d_qk, COL_K), col(D_ML, COL_V), col(D_ML, COL_O), col(LANES, 0),
                const((1, LANES)), const((1, D_ML))]
    out_specs = [pl.BlockSpec((1, CHUNK, D_ML), lambda i, j: (i, j, 0))]
    out_shape = [jax.ShapeDtypeStruct((b, s, D_ML), BF16)]
    args = [proj, proj, proj, proj, small, bias.reshape(1, LANES), norm_w.reshape(1, -1)]
    if prep:
        w_in_specs, w_out_specs, w_out_shape = _deinterleave_specs(
            w_gate_up, lambda i, j: ((i * nc + j) // tiles_per_expert, (i * nc + j) % tiles_per_expert))
        in_specs += w_in_specs
        out_specs += w_out_specs
        out_shape += w_out_shape
        args += [w_gate_up, _deinterleave_perm()]
    outs = pl.pallas_call(
        functools.partial(_mlstm_kernel, prep_gate_up=prep),
        grid=(b, nc),
        in_specs=in_specs,
        out_specs=out_specs,
        out_shape=out_shape,
        scratch_shapes=[pltpu.VMEM((ML_HEADS, ML_DQK), F32), pltpu.VMEM((ML_HEADS, LANES), F32)]
        + [pltpu.VMEM((ML_DV, ML_DQK), F32)] * ML_HEADS,
        compiler_params=_cparams(("arbitrary", "arbitrary")),
        name="mlstm_mixer",
    )(*args)
    if prep:
        return outs[0], outs[1], outs[2]
    return (outs[0], *_deinterleave_gate_up(w_gate_up))


def _outproj_router_kernel(x_ref, ys_ref, ym_ref, wo_ref, n2w_ref, rw_ref, rb_ref,
                           h1_ref, xnp_ref, seli_ref, selg_ref, cnt_ref, carry):
    tm = x_ref.shape[0]

    @pl.when(pl.program_id(0) == 0)
    def _():
        carry[...] = jnp.zeros_like(carry)

    h1 = x_ref[...] + _dot(ys_ref[...], wo_ref[0:D_SSD, :]) + _dot(ym_ref[...], wo_ref[D_SSD:, :])
    h1_ref[...] = h1
    xn = h1 * lax.rsqrt(jnp.mean(h1 * h1, axis=-1, keepdims=True) + EPS) * n2w_ref[...]
    packed = _pack_bf16_pairs(xn)
    for sub in range(X_SUB):
        xnp_ref[pl.ds(sub, tm, stride=X_SUB), :] = packed[:, sub * LANES:(sub + 1) * LANES]

    lane = lax.broadcasted_iota(I32, (tm, LANES), 1)
    lane_f = lane.astype(F32)
    xn_hi = xn.astype(BF16)
    xn_lo = (xn - xn_hi.astype(F32)).astype(BF16)
    hh_hl = _dot(xn_hi, rw_ref[...])
    logits = hh_hl[:, :LANES] + hh_hl[:, LANES:] + _dot(xn_lo, rw_ref[:, :LANES]) + rb_ref[...]
    work = jnp.where(lane < N_EXPERTS, logits, -jnp.inf)
    vals, idxs = [], []
    chosen = jnp.zeros((tm, LANES), jnp.bool_)
    for _ in range(TOP_K):
        m = jnp.max(work, axis=1, keepdims=True)
        idx = jnp.min(jnp.where(work == m, lane_f, float(LANES)), axis=1, keepdims=True)
        sel = lane_f == idx
        vals.append(m)
        idxs.append(idx)
        chosen = jnp.logical_or(chosen, sel)
        work = jnp.where(sel, -jnp.inf, work)
    exps = [jnp.exp(v - vals[0]) for v in vals]
    denom = exps[0] + exps[1] + exps[2] + exps[3]

    onehot = jnp.where(chosen, 1.0, 0.0)
    ranks = _dot(_tril(tm, strict=True).astype(BF16), onehot.astype(BF16)) + carry[...]
    total = carry[...] + jnp.sum(onehot, axis=0, keepdims=True)
    carry[...] = total
    cnt_ref[...] = jnp.broadcast_to(total, cnt_ref.shape)

    seli = jnp.zeros((tm, LANES), F32)
    selg = jnp.zeros((tm, LANES), F32)
    for k in range(TOP_K):
        rank_k = jnp.sum(jnp.where(lane_f == idxs[k], ranks, 0.0), axis=1, keepdims=True)
        seli = jnp.where(lane == k, idxs[k], seli)
        seli = jnp.where(lane == TOP_K + k, rank_k, seli)
        selg = jnp.where(lane == k, exps[k] / denom, selg)
    seli_ref[...] = seli.astype(I32)
    selg_ref[...] = selg


def _outproj_router(xf, y_ssd, y_ml, w_out_b, norm2_w, router_w_pad, router_b_pad):
    t = xf.shape[0]
    tm = OUT_TM
    row = lambda width: pl.BlockSpec((tm, width), lambda i: (i, 0))
    const = lambda shape: pl.BlockSpec(shape, lambda i: (0,) * len(shape))
    return pl.pallas_call(
        _outproj_router_kernel,
        grid=(t // tm,),
        in_specs=[row(D_MODEL), row(D_SSD), row(D_ML),
                  pl.BlockSpec((D_SSD + D_ML, D_MODEL), lambda i: (0, 0), pipeline_mode=pl.Buffered(1)),
                  const((1, D_MODEL)),
                  const((D_MODEL, 2 * LANES)), const((1, LANES))],
        out_specs=[row(D_MODEL), pl.BlockSpec((tm * X_SUB, LANES), lambda i: (i, 0)), row(LANES), row(LANES),
                   const((SUBLANES, LANES))],
        out_shape=[jax.ShapeDtypeStruct((t, D_MODEL), F32),
                   jax.ShapeDtypeStruct((t * X_SUB, LANES), U32),
                   jax.ShapeDtypeStruct((t, LANES), I32),
                   jax.ShapeDtypeStruct((t, LANES), F32),
                   jax.ShapeDtypeStruct((SUBLANES, LANES), F32)],
        scratch_shapes=[pltpu.VMEM((1, LANES), F32)],
        compiler_params=_cparams(("arbitrary",)),
        name="outproj_router",
    )(xf, y_ssd, y_ml, w_out_b, norm2_w, router_w_pad, router_b_pad)


def _route_kernel(seli_ref, cnt_ref, dest_ref, be_ref):
    tm = seli_ref.shape[0]
    nbp = be_ref.shape[0]
    lane1 = lax.broadcasted_iota(I32, (1, LANES), 1)
    cnt = jnp.where(lane1 < N_EXPERTS, cnt_ref[0:1, :], 0.0)
    padded = jnp.ceil(cnt / MOE_TM) * MOE_TM
    r = lax.broadcasted_iota(I32, (LANES, LANES), 0)
    c = lax.broadcasted_iota(I32, (LANES, LANES), 1)
    upper = jnp.where(r <= c, 1.0, 0.0)
    pend = jnp.round(_dot_exact(jnp.broadcast_to(padded, (SUBLANES, LANES)), upper))[0:1, :]
    pstart = pend - padded

    lane = lax.broadcasted_iota(I32, (tm, LANES), 1)
    lane_f = lane.astype(F32)
    seli = seli_ref[...].astype(F32)
    dest = jnp.zeros((tm, LANES), F32)
    for k in range(TOP_K):
        idx_k = jnp.sum(jnp.where(lane == k, seli, 0.0), axis=1, keepdims=True)
        rank_k = jnp.sum(jnp.where(lane == TOP_K + k, seli, 0.0), axis=1, keepdims=True)
        start_k = jnp.sum(jnp.where(lane_f == idx_k, pstart, 0.0), axis=1, keepdims=True)
        dest = jnp.where(lane == k, start_k + rank_k, dest)
    dest_ref[...] = dest.astype(I32)

    blk_start = (lax.broadcasted_iota(I32, (nbp, LANES), 0) * MOE_TM).astype(F32)
    lane_b = lax.broadcasted_iota(I32, (nbp, LANES), 1)
    passed = jnp.where(jnp.logical_and(lane_b < N_EXPERTS, pend <= blk_start), 1.0, 0.0)
    expert = jnp.minimum(jnp.sum(passed, axis=1, keepdims=True), float(N_EXPERTS - 1))
    n_active = pend[:, N_EXPERTS - 1:N_EXPERTS] / MOE_TM
    tok_end = jnp.sum(jnp.where(lane_b.astype(F32) == expert, pstart + cnt, 0.0), axis=1, keepdims=True)
    valid = jnp.clip(tok_end - blk_start[:, 0:1], 0.0, float(MOE_TM))
    n_sub = jnp.ceil(valid / MOE_SB)
    be = jnp.where(lane_b == 0, expert, jnp.where(lane_b == 1, n_active, jnp.where(lane_b == 2, n_sub, 0.0)))
    be_ref[...] = be.astype(I32)


def _route_offsets(seli, cnt, n_blocks):
    t = seli.shape[0]
    nbp = -(-n_blocks // SUBLANES) * SUBLANES
    return pl.pallas_call(
        _route_kernel,
        grid=(t // ROUTE_TM,),
        in_specs=[pl.BlockSpec((ROUTE_TM, LANES), lambda i: (i, 0)),
                  pl.BlockSpec((SUBLANES, LANES), lambda i: (0, 0))],
        out_specs=[pl.BlockSpec((ROUTE_TM, LANES), lambda i: (i, 0)),
                   pl.BlockSpec((nbp, LANES), lambda i: (0, 0))],
        out_shape=[jax.ShapeDtypeStruct((t, LANES), I32),
                   jax.ShapeDtypeStruct((nbp, LANES), I32)],
        compiler_params=_cparams(("arbitrary",)),
        name="route_offsets",
    )(seli, cnt)


def _dispatch_kernel(dest_ref, xnp_ref, rows_in_ref, rows_ref, sem):
    del rows_in_ref
    tt = xnp_ref.shape[0] // X_SUB

    def tile_copy(t, d):
        src = xnp_ref.at[pl.ds(pl.multiple_of(t * X_SUB, X_SUB), X_SUB)]
        dst = rows_ref.at[pl.ds(pl.multiple_of(d * X_SUB, X_SUB), X_SUB)]
        return pltpu.make_async_copy(src, dst, sem)

    def issue(t, carry):
        for k in range(TOP_K):
            tile_copy(t, dest_ref[t * TOP_K + k]).start(priority=k % 2)
        return carry

    lax.fori_loop(0, tt, issue, 0, unroll=4)
    for _ in range(TOP_K):
        pltpu.make_async_copy(xnp_ref, rows_ref.at[pl.ds(0, tt * X_SUB)], sem).wait()


def _dispatch(dest_flat, xnp, n_rows):
    t = xnp.shape[0] // X_SUB
    rows0 = jnp.zeros((n_rows * X_SUB, LANES), U32)
    return pl.pallas_call(
        _dispatch_kernel,
        grid=(t // DISP_TT,),
        in_specs=[pl.BlockSpec((DISP_TT * TOP_K,), lambda i: (i,), memory_space=pltpu.SMEM),
                  pl.BlockSpec((DISP_TT * X_SUB, LANES), lambda i: (i, 0)),
                  pl.BlockSpec(memory_space=pl.ANY)],
        out_specs=pl.BlockSpec(memory_space=pl.ANY),
        out_shape=jax.ShapeDtypeStruct((n_rows * X_SUB, LANES), U32),
        scratch_shapes=[pltpu.SemaphoreType.DMA(())],
        input_output_aliases={2: 0},
        compiler_params=_cparams(("arbitrary",)),
        name="moe_dispatch",
    )(dest_flat, xnp, rows0)


def _moe_kernel(be_ref, nsb_ref, na_ref, x_ref, wglu_ref, wlin_ref, bglu_ref, blin_ref, wd_ref, bd_ref, o_ref,
                xb, *accs):
    m = pl.program_id(0)
    f = pl.program_id(1)
    nf = pl.num_programs(1)
    n_sub = MOE_TM // MOE_SB

    def ffn(nsb):
        rows = nsb * MOE_SB

        @pl.when(f == 0)
        def _():
            _unpack_bf16_pairs(x_ref, rows, xb)
            for i in range(nsb):
                accs[i][...] = jnp.broadcast_to(bd_ref[0], (MOE_SB, D_MODEL))

        wd = wd_ref[0].astype(BF16)
        for i in range(nsb):
            rs = slice(i * MOE_SB, (i + 1) * MOE_SB)
            glu = jnp.minimum(_dot(xb[rs, :], wglu_ref[0]) + bglu_ref[0], SWIGLU_LIMIT)
            lin = jnp.clip(_dot(xb[rs, :], wlin_ref[0]) + blin_ref[0], -SWIGLU_LIMIT, SWIGLU_LIMIT)
            act = glu * _sigmoid(SWIGLU_ALPHA * glu) * (lin + 1.0)
            accs[i][...] += _dot(act.astype(BF16), wd)

        @pl.when(f == nf - 1)
        def _():
            for i in range(nsb):
                packed = _pack_bf16_pairs(accs[i][...])
                for sub in range(X_SUB):
                    o_ref[pl.ds(i * MOE_SB * X_SUB + sub, MOE_SB, stride=X_SUB), :] = (
                        packed[:, sub * LANES:(sub + 1) * LANES])
            if nsb < n_sub:
                o_ref[rows * X_SUB:, :] = jnp.zeros(((MOE_TM - rows) * X_SUB, LANES), U32)

    active = m < na_ref[0]
    for nsb in range(1, n_sub + 1):
        pl.when(jnp.logical_and(active, nsb_ref[m] == nsb))(functools.partial(ffn, nsb))

    @pl.when(jnp.logical_and(jnp.logical_not(active), f == nf - 1))
    def _():
        o_ref[...] = jnp.zeros_like(o_ref)


def _moe_ffn(block_expert, block_nsub, n_active, x_rows, w_glu, w_lin, b_glu, b_lin, w_down, b_down, n_blocks):
    nf = D_FF // MOE_TF

    def blk(m, na):
        return jnp.maximum(jnp.minimum(m, na[0] - 1), 0)

    def ftile(m, f, na):
        return jnp.where(m < na[0], f, nf - 1)

    def expert(m, be, na):
        return be[blk(m, na)]

    col_tile = lambda m, f, be, ns, na: (expert(m, be, na), 0, ftile(m, f, na))
    grid_spec = pltpu.PrefetchScalarGridSpec(
        num_scalar_prefetch=3,
        grid=(n_blocks, nf),
        in_specs=[pl.BlockSpec((MOE_TM * X_SUB, LANES), lambda m, f, be, ns, na: (blk(m, na), 0)),
                  pl.BlockSpec((1, D_MODEL, MOE_TF), col_tile),
                  pl.BlockSpec((1, D_MODEL, MOE_TF), col_tile),
                  pl.BlockSpec((1, 1, MOE_TF), col_tile),
                  pl.BlockSpec((1, 1, MOE_TF), col_tile),
                  pl.BlockSpec((1, MOE_TF, D_MODEL), lambda m, f, be, ns, na: (expert(m, be, na), ftile(m, f, na), 0)),
                  pl.BlockSpec((1, 1, D_MODEL), lambda m, f, be, ns, na: (expert(m, be, na), 0, 0))],
        out_specs=pl.BlockSpec((MOE_TM * X_SUB, LANES), lambda m, f, be, ns, na: (m, 0)),
        scratch_shapes=[pltpu.VMEM((MOE_TM, D_MODEL), BF16)]
        + [pltpu.VMEM((MOE_SB, D_MODEL), F32)] * (MOE_TM // MOE_SB),
    )
    return pl.pallas_call(
        _moe_kernel,
        grid_spec=grid_spec,
        out_shape=jax.ShapeDtypeStruct((n_blocks * MOE_TM * X_SUB, LANES), U32),
        compiler_params=_cparams(("arbitrary", "arbitrary")),
        name="moe_ffn",
    )(block_expert, block_nsub, n_active, x_rows, w_glu, w_lin, b_glu, b_lin, w_down, b_down)


def _combine_kernel(dest_ref, dest_next_ref, h1_ref, selg_ref, fw_ref, rows_ref, o_ref, buf, h_scr, sem):
    tt = h1_ref.shape[0]
    i = pl.program_id(0)
    n = pl.num_programs(0)
    half = D_MODEL // 2

    def row_copy(slot, t, k, d):
        src = rows_ref.at[pl.ds(pl.multiple_of(d * X_SUB, X_SUB), X_SUB)]
        dst = buf.at[slot, k, pl.ds(pl.multiple_of(t * X_SUB, X_SUB), X_SUB)]
        return pltpu.make_async_copy(src, dst, sem.at[slot])

    def gather(slot, d_ref):
        def issue(t, carry):
            for k in range(TOP_K):
                row_copy(slot, t, k, d_ref[t * TOP_K + k]).start(priority=k % 2)
            return carry

        lax.fori_loop(0, tt, issue, 0, unroll=4)

    slot = i % 2

    @pl.when(i == 0)
    def _():
        gather(0, dest_ref)

    @pl.when(i + 1 < n)
    def _():
        gather(1 - slot, dest_next_ref)

    for k in range(TOP_K):
        pltpu.make_async_copy(rows_ref.at[pl.ds(0, tt * X_SUB)], buf.at[slot, k], sem.at[slot]).wait()

    gates = selg_ref[...]
    for sub in range(X_SUB):
        lo_cs = slice(sub * LANES, (sub + 1) * LANES)
        hi_cs = slice(half + sub * LANES, half + (sub + 1) * LANES)
        h_lo = h1_ref[:, lo_cs]
        h_hi = h1_ref[:, hi_cs]
        for k in range(TOP_K):
            u = buf[slot, k, pl.ds(sub, tt, stride=X_SUB), :]
            g = gates[:, k:k + 1]
            h_lo = h_lo + g * pltpu.bitcast(u << 16, F32)
            h_hi = h_hi + g * pltpu.bitcast(u & jnp.uint32(0xFFFF0000), F32)
        h_scr[:, lo_cs] = h_lo
        h_scr[:, hi_cs] = h_hi
    h = h_scr[...]
    o_ref[...] = h * lax.rsqrt(jnp.mean(h * h, axis=-1, keepdims=True) + EPS) * fw_ref[...]


def _combine(dest_flat, h1, selg, final_w, y_rows):
    t = h1.shape[0]
    tt = COMB_TT
    n = t // tt
    return pl.pallas_call(
        _combine_kernel,
        grid=(n,),
        in_specs=[pl.BlockSpec((tt * TOP_K,), lambda i: (i,), memory_space=pltpu.SMEM),
                  pl.BlockSpec((tt * TOP_K,), lambda i: (jnp.minimum(i + 1, n - 1),), memory_space=pltpu.SMEM),
                  pl.BlockSpec((tt, D_MODEL), lambda i: (i, 0)),
                  pl.BlockSpec((tt, LANES), lambda i: (i, 0)),
                  pl.BlockSpec((1, D_MODEL), lambda i: (0, 0)),
                  pl.BlockSpec(memory_space=pl.ANY)],
        out_specs=pl.BlockSpec((tt, D_MODEL), lambda i: (i, 0)),
        out_shape=jax.ShapeDtypeStruct((t, D_MODEL), F32),
        scratch_shapes=[pltpu.VMEM((2, TOP_K, tt * X_SUB, LANES), U32), pltpu.VMEM((tt, D_MODEL), F32),
                        pltpu.SemaphoreType.DMA((2,))],
        compiler_params=_cparams(("arbitrary",)),
        name="moe_combine",
    )(dest_flat, dest_flat, h1, selg, final_w, y_rows)


def kernel(x, norm1_w, w_in, conv_w, conv_b, dt_bias, a_log, d_skip, ssd_norm_w, i_bias, f_bias, ml_norm_w,
           w_out, norm2_w, router_w, router_b, w_gate_up, b_gate_up, w_down, b_down, final_norm_w):
    b, s, d = x.shape
    t = b * s
    assert w_in.shape[0] == 1, "single-layer problem"
    h = x.reshape(t, d)
    for l in range(1):
        wi = w_in[l]
        o_z, o_xbc, o_dt = 0, D_SSD, D_SSD + D_SSD + 2 * SSD_GROUPS * SSD_STATE
        o_q = o_dt + SSD_HEADS
        o_k = o_q + ML_HEADS * ML_DQK
        o_v = o_k + ML_HEADS * ML_DQK
        o_o = o_v + D_ML
        o_i = o_o + D_ML
        o_f = o_i + ML_HEADS
        w_big = jnp.concatenate(
            [wi[:, o_z:o_z + D_SSD], wi[:, o_xbc:o_xbc + D_SSD], wi[:, o_v:o_v + D_ML], wi[:, o_o:o_o + D_ML],
             wi[:, o_xbc + D_SSD:o_dt], wi[:, o_q:o_k], wi[:, o_k:o_v]], axis=1).astype(BF16)
        w_small = jnp.concatenate(
            [wi[:, o_dt:o_q], wi[:, o_i:o_f], wi[:, o_f:o_f + ML_HEADS],
             jnp.zeros((d, LANES - SSD_HEADS - 2 * ML_HEADS), F32)], axis=1).astype(BF16)

        proj, small = _input_projection(h, norm1_w[l].reshape(1, d), w_big, w_small)
        proj3 = proj.reshape(b, s, D_BIG)
        small3 = small.reshape(b, s, LANES)
        y_ssd = _ssd_mixer(proj3, small3, conv_w[l], conv_b[l], dt_bias[l], a_log[l], d_skip[l], ssd_norm_w[l], b, s)
        y_ml, w_glu, w_lin = _mlstm_mixer(proj3, small3, i_bias[l], f_bias[l], ml_norm_w[l], b, s, w_gate_up[l])

        rw = jnp.pad(router_w[l], ((0, 0), (0, LANES - N_EXPERTS)))
        rw_hi = rw.astype(BF16)
        rw = jnp.concatenate([rw_hi, (rw - rw_hi.astype(F32)).astype(BF16)], axis=1)
        rb = jnp.pad(router_b[l], (0, LANES - N_EXPERTS)).reshape(1, LANES)
        h1, xnp, seli, selg, cnt = _outproj_router(
            h, y_ssd.reshape(t, D_SSD), y_ml.reshape(t, D_ML), w_out[l].astype(BF16), norm2_w[l].reshape(1, d), rw, rb)

        n_blocks = (t * TOP_K) // MOE_TM + N_EXPERTS
        dest, table = _route_offsets(seli, cnt, n_blocks)
        dest_flat = dest[:, :TOP_K].reshape(-1)
        block_expert = table[:n_blocks, 0]
        n_active = table[0:1, 1]
        block_nsub = table[:n_blocks, 2]

        x_rows = _dispatch(dest_flat, xnp, n_blocks * MOE_TM)
        b_glu = b_gate_up[l][:, 0::2].reshape(N_EXPERTS, 1, D_FF)
        b_lin = b_gate_up[l][:, 1::2].reshape(N_EXPERTS, 1, D_FF)
        y_rows = _moe_ffn(block_expert, block_nsub, n_active, x_rows, w_glu, w_lin, b_glu, b_lin, w_down[l],
                          b_down[l].reshape(N_EXPERTS, 1, d), n_blocks)
        h = _combine(dest_flat, h1, selg, final_norm_w.reshape(1, d), y_rows)
    return h.reshape(b, s, d)
```

```python
import functools

import jax
import jax.numpy as jnp
from jax import lax
from jax.experimental import pallas as pl
from jax.experimental.pallas import tpu as pltpu

F32 = jnp.float32
BF16 = jnp.bfloat16
I32 = jnp.int32
U32 = jnp.uint32

LANES = 128
SUBLANES = 8
VMEM_LIMIT = 56 * 1024 * 1024

D_MODEL = 2048
EPS = 1e-5
SSD_HEAD_DIM = 64
SSD_HEADS = 32
D_SSD = 2048
SSD_GROUPS = 4
SSD_STATE = 128
SSD_CONV = 4
CHUNK = 128
ML_HEADS = 8
ML_DV = 256
ML_DQK = 128
D_ML = 2048
N_EXPERTS = 32
TOP_K = 4
D_FF = 2048
SWIGLU_ALPHA = 1.702
SWIGLU_LIMIT = 7.0

COL_Z, COL_X, COL_BC, COL_Q, COL_K, COL_V, COL_O = 0, 2048, 4096, 5120, 6144, 7168, 9216
COL_W = 1024
D_BIG = 11264
LANE_DT, LANE_I, LANE_F = 0, 32, 40

IN_TM, IN_TN = 1024, 1024
OUT_TM = 512
ROUTE_TM = 2048
DISP_TT = 512
MOE_TM = 512
MOE_SB = 256
MOE_TF = 1024
PERM_W = 256
PREP_TN = 1024
COMB_TT = 256
X_SUB = D_MODEL // 2 // LANES


def _cparams(sem, vmem=VMEM_LIMIT):
    return pltpu.CompilerParams(dimension_semantics=sem, vmem_limit_bytes=vmem)


def _dot(a, b):
    return jnp.dot(a, b, preferred_element_type=F32)


def _dot_exact(a, b):
    return jnp.dot(a, b, preferred_element_type=F32, precision=lax.Precision.HIGHEST)


def _split2_dot(x, e2_bf16):
    hi = x.astype(BF16)
    lo = (x - hi.astype(F32)).astype(BF16)
    return _dot(jnp.concatenate([hi, lo], axis=1), e2_bf16)


def _sigmoid(x):
    return 1.0 / (1.0 + jnp.exp(-x))


def _softplus(x):
    return jnp.maximum(x, 0.0) + jnp.log1p(jnp.exp(-jnp.abs(x)))


def _tril(n, strict=False):
    r = lax.broadcasted_iota(I32, (n, n), 0)
    c = lax.broadcasted_iota(I32, (n, n), 1)
    return (r > c) if strict else (r >= c)


def _pack_bf16_pairs(y):
    half = D_MODEL // 2
    lo = pltpu.bitcast(y[:, :half].astype(BF16).astype(F32), U32)
    hi = pltpu.bitcast(y[:, half:].astype(BF16).astype(F32), U32)
    return hi | (lo >> 16)


def _unpack_bf16_pairs(src_ref, rows, dst_ref):
    half = D_MODEL // 2
    for sub in range(X_SUB):
        u = src_ref[pl.ds(sub, rows, stride=X_SUB), :]
        dst_ref[0:rows, sub * LANES:(sub + 1) * LANES] = pltpu.bitcast(u << 16, F32).astype(BF16)
        dst_ref[0:rows, half + sub * LANES:half + (sub + 1) * LANES] = (
            pltpu.bitcast(u & jnp.uint32(0xFFFF0000), F32).astype(BF16))


def _deinterleave_tile(w_ref, p_ref, glu_ref, lin_ref):
    hw = PERM_W // 2
    for s in range(w_ref.shape[2] // PERM_W):
        t = _dot(w_ref[0, :, s * PERM_W:(s + 1) * PERM_W].astype(BF16), p_ref[...])
        glu_ref[0, :, s * hw:(s + 1) * hw] = t[:, :hw].astype(BF16)
        lin_ref[0, :, s * hw:(s + 1) * hw] = t[:, hw:].astype(BF16)


def _deinterleave_perm():
    src = jnp.arange(PERM_W)
    dst = jnp.where(src % 2 == 0, src // 2, PERM_W // 2 + src // 2)
    return (dst[:, None] == jnp.arange(PERM_W)[None, :]).astype(BF16)


def _deinterleave_specs(w_gate_up, tile_index):
    e, k, n = w_gate_up.shape

    def at(*ids):
        ei, ci = tile_index(*ids)
        return ei, 0, ci

    in_specs = [pl.BlockSpec((1, k, PREP_TN), at), pl.BlockSpec((PERM_W, PERM_W), lambda *ids: (0, 0))]
    out_specs = [pl.BlockSpec((1, k, PREP_TN // 2), at)] * 2
    out_shape = [jax.ShapeDtypeStruct((e, k, n // 2), BF16)] * 2
    return in_specs, out_specs, out_shape


def _deinterleave_gate_up(w_gate_up):
    e, _, n = w_gate_up.shape
    in_specs, out_specs, out_shape = _deinterleave_specs(w_gate_up, lambda i, j: (i, j))
    return pl.pallas_call(
        _deinterleave_tile,
        grid=(e, n // PREP_TN),
        in_specs=in_specs,
        out_specs=out_specs,
        out_shape=out_shape,
        compiler_params=_cparams(("arbitrary", "arbitrary")),
        name="moe_deinterleave",
    )(w_gate_up, _deinterleave_perm())


def _inproj_kernel(x_ref, nw_ref, w_ref, ws_ref, o_ref, os_ref, n_scr):
    @pl.when(pl.program_id(1) == 0)
    def _():
        x = x_ref[...]
        n = x * lax.rsqrt(jnp.mean(x * x, axis=-1, keepdims=True) + EPS) * nw_ref[...]
        nb = n.astype(BF16)
        n_scr[...] = nb
        os_ref[...] = _dot(nb, ws_ref[...])

    o_ref[...] = _dot(n_scr[...], w_ref[...]).astype(BF16)


def _input_projection(xf, norm_w, w_big, w_small):
    t = xf.shape[0]
    return pl.pallas_call(
        _inproj_kernel,
        grid=(t // IN_TM, D_BIG // IN_TN),
        in_specs=[pl.BlockSpec((IN_TM, D_MODEL), lambda i, j: (i, 0)),
                  pl.BlockSpec((1, D_MODEL), lambda i, j: (0, 0)),
                  pl.BlockSpec((D_MODEL, IN_TN), lambda i, j: (0, j)),
                  pl.BlockSpec((D_MODEL, LANES), lambda i, j: (0, 0))],
        out_specs=[pl.BlockSpec((IN_TM, IN_TN), lambda i, j: (i, j)),
                   pl.BlockSpec((IN_TM, LANES), lambda i, j: (i, 0))],
        out_shape=[jax.ShapeDtypeStruct((t, D_BIG), BF16),
                   jax.ShapeDtypeStruct((t, LANES), F32)],
        scratch_shapes=[pltpu.VMEM((IN_TM, D_MODEL), BF16)],
        compiler_params=_cparams(("arbitrary", "arbitrary")),
        name="inproj",
    )(xf, norm_w, w_big, w_small)


def _ssd_kernel(z_ref, x_ref, bc_ref, sm_ref, cw_ref, cb_ref, shift_ref, dtb_ref, alog_ref,
                dskip_ref, nw_ref, e_ref, o_ref, ext, y_scr, *states):
    q = CHUNK

    @pl.when(pl.program_id(1) == 0)
    def _():
        ext[0:q, :] = jnp.zeros((q, ext.shape[1]), BF16)
        for state in states:
            state[...] = jnp.zeros_like(state)

    ext[q:2 * q, 0:D_SSD] = x_ref[0]
    ext[q:2 * q, D_SSD:] = bc_ref[0]
    shifted = _dot(shift_ref[...], ext[...])
    acc = cb_ref[...] + ext[q:2 * q, :].astype(F32) * cw_ref[SSD_CONV - 1:SSD_CONV, :]
    for j in range(SSD_CONV - 1):
        acc = acc + shifted[j * q:(j + 1) * q, :] * cw_ref[j:j + 1, :]
    xbc = acc * _sigmoid(acc)
    xs = xbc[:, :D_SSD]
    bcs = xbc[:, D_SSD:]
    ext[0:q, :] = ext[q:2 * q, :]

    lane = lax.broadcasted_iota(I32, (1, LANES), 1)
    g = sm_ref[0]
    dt = _softplus(g + dtb_ref[...])
    a = jnp.where(lane < SSD_HEADS, -jnp.exp(alog_ref[...]), 0.0)
    da = dt * a
    tri = _tril(q).astype(F32)
    a_cum = _dot_exact(tri, da)
    a_cum_t = a_cum.T
    a_end = a_cum[q - 1:q, :]
    stack = jnp.concatenate(
        [dt, jnp.exp(a_end - a_cum), jnp.exp(a_cum), jnp.broadcast_to(jnp.exp(a_end), (SUBLANES, LANES))], axis=0)
    ex = _split2_dot(stack, e_ref[...])
    dt_x = ex[0:q]
    to_end_x = ex[q:2 * q]
    from_start_x = ex[2 * q:3 * q]
    chunk_decay_x = ex[3 * q:3 * q + 1]

    xdt = xs * dt_x
    xdt_b = xdt.astype(BF16)
    xde_b = (xdt * to_end_x).astype(BF16)
    causal = _tril(q)
    lane_q = lax.broadcasted_iota(I32, (1, LANES), 1)
    gw = SSD_STATE
    hpg = SSD_HEADS // SSD_GROUPS
    cpg = D_SSD // SSD_GROUPS
    for gi in range(SSD_GROUPS):
        bg = bcs[:, gi * gw:(gi + 1) * gw]
        cg_b = bcs[:, SSD_GROUPS * gw + gi * gw:SSD_GROUPS * gw + (gi + 1) * gw].astype(BF16)
        cb = lax.dot_general(cg_b, bg.astype(BF16), (((1,), (1,)), ((), ())), preferred_element_type=F32)
        bg_t = bg.T.astype(BF16)
        cs = slice(gi * cpg, (gi + 1) * cpg)
        prev = states[gi][...]
        y_off = _dot(cg_b, prev.astype(BF16)) * from_start_x[:, cs]
        states[gi][...] = prev * chunk_decay_x[:, cs] + _dot(bg_t, xde_b[:, cs])
        for pr in range(hpg // 2):
            c0 = gi * cpg + pr * LANES
            xp = xdt_b[:, c0:c0 + LANES]
            acc = y_off[:, pr * LANES:(pr + 1) * LANES]
            for sub in range(2):
                h = gi * hpg + pr * 2 + sub
                seg = a_cum[:, h:h + 1] - a_cum_t[h:h + 1, :]
                decay = jnp.exp(jnp.where(causal, seg, -jnp.inf))
                m = (cb * decay).astype(BF16)
                keep = (lane_q < SSD_HEAD_DIM) if sub == 0 else (lane_q >= SSD_HEAD_DIM)
                acc = acc + _dot(m, jnp.where(keep, xp, jnp.zeros_like(xp)))
            y_scr[:, c0:c0 + LANES] = acc

    y = y_scr[...] + xs * dskip_ref[...]
    zf = z_ref[0].astype(F32)
    y = y * (zf * _sigmoid(zf))
    for gi in range(SSD_GROUPS):
        cs = slice(gi * cpg, (gi + 1) * cpg)
        yg = y[:, cs]
        yg = yg * lax.rsqrt(jnp.mean(yg * yg, axis=-1, keepdims=True) + EPS)
        o_ref[0, :, cs] = (yg * nw_ref[:, cs]).astype(BF16)


def _ssd_mixer(proj, small, conv_w, conv_b, dt_bias, a_log, d_skip, norm_w, b, s):
    nc = s // CHUNK
    pad = LANES - SSD_HEADS
    dtb = jnp.pad(dt_bias, (0, pad)).reshape(1, LANES)
    alog = jnp.pad(a_log, (0, pad)).reshape(1, LANES)
    dskip = jnp.repeat(d_skip, SSD_HEAD_DIM).reshape(1, D_SSD)
    expand = (jnp.arange(LANES)[:, None] == (jnp.arange(D_SSD) // SSD_HEAD_DIM)[None, :]).astype(BF16)
    expand = jnp.concatenate([expand, expand], axis=0)
    d_bc = 2 * SSD_GROUPS * SSD_STATE
    d_conv = D_SSD + d_bc
    tt = jnp.arange((SSD_CONV - 1) * CHUNK)
    back = (SSD_CONV - 1) - tt // CHUNK
    shift = ((CHUNK + tt % CHUNK - back)[:, None] == jnp.arange(2 * CHUNK)[None, :]).astype(BF16)
    const = lambda shape: pl.BlockSpec(shape, lambda i, j: (0,) * len(shape))
    col = lambda width, off: pl.BlockSpec((1, CHUNK, width), lambda i, j: (i, j, off // width))
    return pl.pallas_call(
        _ssd_kernel,
        grid=(b, nc),
        in_specs=[col(D_SSD, COL_Z), col(D_SSD, COL_X), col(d_bc, COL_BC), col(LANES, 0),
                  const((SSD_CONV, d_conv)), const((1, d_conv)), const(((SSD_CONV - 1) * CHUNK, 2 * CHUNK)),
                  const((1, LANES)), const((1, LANES)), const((1, D_SSD)), const((1, D_SSD)),
                  const((2 * LANES, D_SSD))],
        out_specs=pl.BlockSpec((1, CHUNK, D_SSD), lambda i, j: (i, j, 0)),
        out_shape=jax.ShapeDtypeStruct((b, s, D_SSD), BF16),
        scratch_shapes=[pltpu.VMEM((2 * CHUNK, d_conv), BF16),
                        pltpu.VMEM((CHUNK, D_SSD), F32)]
        + [pltpu.VMEM((SSD_STATE, D_SSD // SSD_GROUPS), F32)] * SSD_GROUPS,
        compiler_params=_cparams(("arbitrary", "arbitrary")),
        name="ssd_mixer",
    )(proj, proj, proj, small, conv_w, conv_b.reshape(1, -1), shift, dtb, alog, dskip, norm_w.reshape(1, -1), expand)


def _mlstm_kernel(q_ref, k_ref, v0_ref, v1_ref, o0_ref, o1_ref, sm_ref, bias_ref, nw_ref, *rest, prep_gate_up):
    L = CHUNK
    heads_per_ref = COL_W // ML_DV

    def head_cols(refs, h):
        c0 = (h % heads_per_ref) * ML_DV
        return refs[h // heads_per_ref][0, :, c0:c0 + ML_DV]

    if prep_gate_up:
        wg_ref, perm_ref, out_ref, glu_ref, lin_ref, n_scr, m_scr, *ct_scrs = rest
        _deinterleave_tile(wg_ref, perm_ref, glu_ref, lin_ref)
    else:
        out_ref, n_scr, m_scr, *ct_scrs = rest

    @pl.when(pl.program_id(1) == 0)
    def _():
        for ct_scr in ct_scrs:
            ct_scr[...] = jnp.zeros_like(ct_scr)
        n_scr[...] = jnp.zeros_like(n_scr)
        m_scr[...] = jnp.zeros_like(m_scr)

    n_all = n_scr[...]
    m_all = m_scr[...]
    n_new_all = n_all
    m_new_all = m_all
    head_row = lax.broadcasted_iota(I32, (ML_HEADS, LANES), 0)

    g = sm_ref[0] + bias_ref[...]
    log_f = jnp.minimum(g, 0.0) - jnp.log1p(jnp.exp(-jnp.abs(g)))
    tri = _tril(L).astype(F32)
    bcum = _dot_exact(tri, log_f)
    bcum_t = bcum.T
    keys_before = jnp.logical_not(_tril(L, strict=True))
    row16 = lax.broadcasted_iota(I32, (2 * SUBLANES, ML_DQK), 0)
    scale = ML_DQK ** -0.5
    nt = (((1,), (1,)), ((), ()))
    tn = (((0,), (0,)), ((), ()))
    for h in range(ML_HEADS):
        li, lf = LANE_I + h, LANE_F + h
        li_col, b_col, b_row = g[:, li:li + 1], bcum[:, lf:lf + 1], bcum_t[lf:lf + 1, :]
        btot = bcum[L - 1:L, lf:lf + 1]
        m_prev = m_all[h:h + 1, 0:1]
        ct_prev = ct_scrs[h][...]
        n_prev = n_all[h:h + 1, :]
        qb = q_ref[0, :, h * ML_DQK:(h + 1) * ML_DQK]
        kb = k_ref[0, :, h * ML_DQK:(h + 1) * ML_DQK]
        vb = head_cols((v0_ref, v1_ref), h)
        w_end = btot - b_col + li_col
        m_loc = jnp.max(w_end, axis=0, keepdims=True)
        k_end = kb.astype(F32) * jnp.exp(w_end - m_loc)
        kv_t = lax.dot_general(vb, k_end.astype(BF16), tn, preferred_element_type=F32)
        nk = jnp.sum(k_end, axis=0, keepdims=True)
        d_log = jnp.where(keys_before, (li_col - b_col) + b_row, -jnp.inf)
        m_inter = b_row + m_prev
        m_t = jnp.maximum(m_inter, jnp.max(d_log, axis=0, keepdims=True))
        s_t = lax.dot_general(kb, qb, nt, preferred_element_type=F32)
        s_t = s_t * (scale * jnp.exp(d_log - m_t))
        inter = scale * jnp.exp(m_inter - m_t)
        num_t = (lax.dot_general(vb, s_t.astype(BF16), tn, preferred_element_type=F32)
                 + lax.dot_general(ct_prev.astype(BF16), qb, nt, preferred_element_type=F32) * inter)
        n_hi = n_prev.astype(BF16).astype(F32)
        n_rows = jnp.where(row16 == 0, n_hi, jnp.where(row16 == 1, n_prev - n_hi, 0.0)).astype(BF16)
        qn = lax.dot_general(n_rows, qb, nt, preferred_element_type=F32)
        den = jnp.sum(s_t, axis=0, keepdims=True) + (qn[0:1, :] + qn[1:2, :]) * inter
        den = jnp.maximum(jnp.abs(den), jnp.exp(-m_t))
        ht_t = num_t / den
        ht_t = ht_t * lax.rsqrt(jnp.mean(ht_t * ht_t, axis=0, keepdims=True) + EPS)
        vs = slice(h * ML_DV, (h + 1) * ML_DV)
        og = _sigmoid(head_cols((o0_ref, o1_ref), h).astype(F32))
        out_ref[0, :, vs] = (ht_t.T * nw_ref[:, vs] * og).astype(BF16)
        m_new = jnp.maximum(btot + m_prev, m_loc)
        fa = jnp.exp(btot + m_prev - m_new)
        fg = jnp.exp(m_loc - m_new)
        ct_scrs[h][...] = fa * ct_prev + fg * kv_t
        n_new_all = jnp.where(head_row == h, fa * n_prev + fg * nk, n_new_all)
        m_new_all = jnp.where(head_row == h, m_new, m_new_all)
    n_scr[...] = n_new_all
    m_scr[...] = m_new_all


def _mlstm_mixer(proj, small, i_bias, f_bias, norm_w, b, s, w_gate_up):
    nc = s // CHUNK
    bias = jnp.zeros((LANES,), F32).at[LANE_I:LANE_I + ML_HEADS].set(i_bias).at[LANE_F:LANE_F + ML_HEADS].set(f_bias)
    d_qk = ML_HEADS * ML_DQK
    const = lambda shape: pl.BlockSpec(shape, lambda i, j: (0,) * len(shape))
    col = lambda width, off: pl.BlockSpec((1, CHUNK, width), lambda i, j: (i, j, off // width))
    e, _, n = w_gate_up.shape
    tiles_per_expert = n // PREP_TN
    prep = b * nc == e * tiles_per_expert
    in_specs = [col(d_qk, COL_Q), col(d_qk, COL_K), col(COL_W, COL_V), col(COL_W, COL_V + COL_W),
                col(COL_W, COL_O), col(COL_W, COL_O + COL_W), col(LANES, 0), const((1, LANES)), const((1, D_ML))]
    out_specs = [pl.BlockSpec((1, CHUNK, D_ML), lambda i, j: (i, j, 0))]
    out_shape = [jax.ShapeDtypeStruct((b, s, D_ML), BF16)]
    args = [proj] * 6 + [small, bias.reshape(1, LANES), norm_w.reshape(1, -1)]
    if prep:
        w_in_specs, w_out_specs, w_out_shape = _deinterleave_specs(
            w_gate_up, lambda i, j: ((i * nc + j) // tiles_per_expert, (i * nc + j) % tiles_per_expert))
        in_specs += w_in_specs
        out_specs += w_out_specs
        out_shape += w_out_shape
        args += [w_gate_up, _deinterleave_perm()]
    outs = pl.pallas_call(
        functools.partial(_mlstm_kernel, prep_gate_up=prep),
        grid=(b, nc),
        in_specs=in_specs,
        out_specs=out_specs,
        out_shape=out_shape,
        scratch_shapes=[pltpu.VMEM((ML_HEADS, ML_DQK), F32), pltpu.VMEM((ML_HEADS, LANES), F32)]
        + [pltpu.VMEM((ML_DV, ML_DQK), F32)] * ML_HEADS,
        compiler_params=_cparams(("arbitrary", "arbitrary")),
        name="mlstm_mixer",
    )(*args)
    if prep:
        return outs[0], outs[1], outs[2]
    return (outs[0], *_deinterleave_gate_up(w_gate_up))


def _outproj_router_kernel(x_ref, ys_ref, ym_ref, wo_ref, n2w_ref, rw_ref, rb_ref,
                           h1_ref, xnp_ref, seli_ref, selg_ref, cnt_ref, carry):
    tm = x_ref.shape[0]

    @pl.when(pl.program_id(0) == 0)
    def _():
        carry[...] = jnp.zeros_like(carry)

    h1 = x_ref[...] + _dot(ys_ref[...], wo_ref[0:D_SSD, :]) + _dot(ym_ref[...], wo_ref[D_SSD:, :])
    h1_ref[...] = h1
    xn = h1 * lax.rsqrt(jnp.mean(h1 * h1, axis=-1, keepdims=True) + EPS) * n2w_ref[...]
    packed = _pack_bf16_pairs(xn)
    for sub in range(X_SUB):
        xnp_ref[pl.ds(sub, tm, stride=X_SUB), :] = packed[:, sub * LANES:(sub + 1) * LANES]

    lane = lax.broadcasted_iota(I32, (tm, LANES), 1)
    lane_f = lane.astype(F32)
    xn_hi = xn.astype(BF16)
    xn_lo = (xn - xn_hi.astype(F32)).astype(BF16)
    hh_hl = _dot(xn_hi, rw_ref[...])
    logits = hh_hl[:, :LANES] + hh_hl[:, LANES:] + _dot(xn_lo, rw_ref[:, :LANES]) + rb_ref[...]
    work = jnp.where(lane < N_EXPERTS, logits, -jnp.inf)
    vals, idxs = [], []
    chosen = jnp.zeros((tm, LANES), jnp.bool_)
    for _ in range(TOP_K):
        m = jnp.max(work, axis=1, keepdims=True)
        idx = jnp.min(jnp.where(work == m, lane_f, float(LANES)), axis=1, keepdims=True)
        sel = lane_f == idx
        vals.append(m)
        idxs.append(idx)
        chosen = jnp.logical_or(chosen, sel)
        work = jnp.where(sel, -jnp.inf, work)
    exps = [jnp.exp(v - vals[0]) for v in vals]
    denom = exps[0] + exps[1] + exps[2] + exps[3]

    onehot = jnp.where(chosen, 1.0, 0.0)
    ranks = _dot(_tril(tm, strict=True).astype(BF16), onehot.astype(BF16)) + carry[...]
    total = carry[...] + jnp.sum(onehot, axis=0, keepdims=True)
    carry[...] = total
    cnt_ref[...] = jnp.broadcast_to(total, cnt_ref.shape)

    seli = jnp.zeros((tm, LANES), F32)
    selg = jnp.zeros((tm, LANES), F32)
    for k in range(TOP_K):
        rank_k = jnp.sum(jnp.where(lane_f == idxs[k], ranks, 0.0), axis=1, keepdims=True)
        seli = jnp.where(lane == k, idxs[k], seli)
        seli = jnp.where(lane == TOP_K + k, rank_k, seli)
        selg = jnp.where(lane == k, exps[k] / denom, selg)
    seli_ref[...] = seli.astype(I32)
    selg_ref[...] = selg


def _outproj_router(xf, y_ssd, y_ml, w_out_b, norm2_w, router_w_pad, router_b_pad):
    t = xf.shape[0]
    tm = OUT_TM
    row = lambda width: pl.BlockSpec((tm, width), lambda i: (i, 0))
    const = lambda shape: pl.BlockSpec(shape, lambda i: (0,) * len(shape))
    return pl.pallas_call(
        _outproj_router_kernel,
        grid=(t // tm,),
        in_specs=[row(D_MODEL), row(D_SSD), row(D_ML),
                  pl.BlockSpec((D_SSD + D_ML, D_MODEL), lambda i: (0, 0), pipeline_mode=pl.Buffered(1)),
                  const((1, D_MODEL)),
                  const((D_MODEL, 2 * LANES)), const((1, LANES))],
        out_specs=[row(D_MODEL), pl.BlockSpec((tm * X_SUB, LANES), lambda i: (i, 0)), row(LANES), row(LANES),
                   const((SUBLANES, LANES))],
        out_shape=[jax.ShapeDtypeStruct((t, D_MODEL), F32),
                   jax.ShapeDtypeStruct((t * X_SUB, LANES), U32),
                   jax.ShapeDtypeStruct((t, LANES), I32),
                   jax.ShapeDtypeStruct((t, LANES), F32),
                   jax.ShapeDtypeStruct((SUBLANES, LANES), F32)],
        scratch_shapes=[pltpu.VMEM((1, LANES), F32)],
        compiler_params=_cparams(("arbitrary",)),
        name="outproj_router",
    )(xf, y_ssd, y_ml, w_out_b, norm2_w, router_w_pad, router_b_pad)


def _route_kernel(seli_ref, cnt_ref, dest_ref, be_ref):
    tm = seli_ref.shape[0]
    nbp = be_ref.shape[0]
    lane1 = lax.broadcasted_iota(I32, (1, LANES), 1)
    cnt = jnp.where(lane1 < N_EXPERTS, cnt_ref[0:1, :], 0.0)
    padded = jnp.ceil(cnt / MOE_TM) * MOE_TM
    r = lax.broadcasted_iota(I32, (LANES, LANES), 0)
    c = lax.broadcasted_iota(I32, (LANES, LANES), 1)
    upper = jnp.where(r <= c, 1.0, 0.0)
    pend = jnp.round(_dot_exact(jnp.broadcast_to(padded, (SUBLANES, LANES)), upper))[0:1, :]
    pstart = pend - padded

    lane = lax.broadcasted_iota(I32, (tm, LANES), 1)
    lane_f = lane.astype(F32)
    seli = seli_ref[...].astype(F32)
    dest = jnp.zeros((tm, LANES), F32)
    for k in range(TOP_K):
        idx_k = jnp.sum(jnp.where(lane == k, seli, 0.0), axis=1, keepdims=True)
        rank_k = jnp.sum(jnp.where(lane == TOP_K + k, seli, 0.0), axis=1, keepdims=True)
        start_k = jnp.sum(jnp.where(lane_f == idx_k, pstart, 0.0), axis=1, keepdims=True)
        dest = jnp.where(lane == k, start_k + rank_k, dest)
    dest_ref[...] = dest.astype(I32)

    blk_start = (lax.broadcasted_iota(I32, (nbp, LANES), 0) * MOE_TM).astype(F32)
    lane_b = lax.broadcasted_iota(I32, (nbp, LANES), 1)
    passed = jnp.where(jnp.logical_and(lane_b < N_EXPERTS, pend <= blk_start), 1.0, 0.0)
    expert = jnp.minimum(jnp.sum(passed, axis=1, keepdims=True), float(N_EXPERTS - 1))
    n_active = pend[:, N_EXPERTS - 1:N_EXPERTS] / MOE_TM
    tok_end = jnp.sum(jnp.where(lane_b.astype(F32) == expert, pstart + cnt, 0.0), axis=1, keepdims=True)
    valid = jnp.clip(tok_end - blk_start[:, 0:1], 0.0, float(MOE_TM))
    n_sub = jnp.ceil(valid / MOE_SB)
    own = lane_b == lax.broadcasted_iota(I32, (nbp, LANES), 0)
    pad_start = jnp.sum(jnp.where(own, pstart + cnt, 0.0), axis=1, keepdims=True)
    pad_cnt = jnp.sum(jnp.where(own, padded - cnt, 0.0), axis=1, keepdims=True)
    cols = (expert, n_active, n_sub, pad_start, pad_cnt, pend[:, N_EXPERTS - 1:N_EXPERTS])
    be = jnp.zeros((nbp, LANES), F32)
    for j, v in enumerate(cols):
        be = jnp.where(lane_b == j, v, be)
    be_ref[...] = be.astype(I32)


def _route_offsets(seli, cnt, n_blocks):
    t = seli.shape[0]
    nbp = -(-n_blocks // SUBLANES) * SUBLANES
    return pl.pallas_call(
        _route_kernel,
        grid=(t // ROUTE_TM,),
        in_specs=[pl.BlockSpec((ROUTE_TM, LANES), lambda i: (i, 0)),
                  pl.BlockSpec((SUBLANES, LANES), lambda i: (0, 0))],
        out_specs=[pl.BlockSpec((ROUTE_TM, LANES), lambda i: (i, 0)),
                   pl.BlockSpec((nbp, LANES), lambda i: (0, 0))],
        out_shape=[jax.ShapeDtypeStruct((t, LANES), I32),
                   jax.ShapeDtypeStruct((nbp, LANES), I32)],
        compiler_params=_cparams(("arbitrary",)),
        name="route_offsets",
    )(seli, cnt)


def _dispatch_kernel(pad_start_ref, pad_cnt_ref, tail_ref, dest_ref, xnp_ref, rows_ref, zbuf, sem, zsem, *, n_zero):
    tt = xnp_ref.shape[0] // X_SUB
    n_rows = rows_ref.shape[0] // X_SUB
    first = pl.program_id(0) == 0

    def zero_rows(start_row, n):
        dst = rows_ref.at[pl.ds(pl.multiple_of(start_row * X_SUB, X_SUB), n * X_SUB)]
        return pltpu.make_async_copy(zbuf.at[pl.ds(0, n * X_SUB)], dst, zsem)

    @pl.when(first)
    def _():
        zbuf[...] = jnp.zeros_like(zbuf)

        def expert_padding(e, carry):
            start, count = pad_start_ref[e], pad_cnt_ref[e]
            for k in reversed(range(MOE_TM.bit_length() - 1)):
                run = 1 << k
                pl.when((count & run) != 0)(
                    lambda run=run: zero_rows(start + (count & ~(2 * run - 1)), run).start())
            return carry

        lax.fori_loop(0, N_EXPERTS, expert_padding, 0)

        def trailing_block(j, carry):
            start = tail_ref[0] + j * MOE_TM
            pl.when(start < n_rows)(lambda: zero_rows(start, MOE_TM).start())
            return carry

        lax.fori_loop(0, N_EXPERTS, trailing_block, 0)

    def tile_copy(t, d):
        src = xnp_ref.at[pl.ds(pl.multiple_of(t * X_SUB, X_SUB), X_SUB)]
        dst = rows_ref.at[pl.ds(pl.multiple_of(d * X_SUB, X_SUB), X_SUB)]
        return pltpu.make_async_copy(src, dst, sem)

    def issue(t, carry):
        for k in range(TOP_K):
            tile_copy(t, dest_ref[t * TOP_K + k]).start(priority=k % 2)
        return carry

    lax.fori_loop(0, tt, issue, 0, unroll=4)
    for _ in range(TOP_K):
        pltpu.make_async_copy(xnp_ref, rows_ref.at[pl.ds(0, tt * X_SUB)], sem).wait()

    @pl.when(first)
    def _():
        span = rows_ref.at[pl.ds(0, n_zero * X_SUB)]
        pltpu.make_async_copy(span, span, zsem).wait()


def _dispatch(pad_start, pad_cnt, tail_start, dest_flat, xnp, n_rows):
    t = xnp.shape[0] // X_SUB
    grid_spec = pltpu.PrefetchScalarGridSpec(
        num_scalar_prefetch=3,
        grid=(t // DISP_TT,),
        in_specs=[pl.BlockSpec((DISP_TT * TOP_K,), lambda i, *_: (i,), memory_space=pltpu.SMEM),
                  pl.BlockSpec((DISP_TT * X_SUB, LANES), lambda i, *_: (i, 0))],
        out_specs=pl.BlockSpec(memory_space=pl.ANY),
        scratch_shapes=[pltpu.VMEM((MOE_TM * X_SUB, LANES), U32), pltpu.SemaphoreType.DMA(()),
                        pltpu.SemaphoreType.DMA(())],
    )
    return pl.pallas_call(
        functools.partial(_dispatch_kernel, n_zero=n_rows - t * TOP_K),
        grid_spec=grid_spec,
        out_shape=jax.ShapeDtypeStruct((n_rows * X_SUB, LANES), U32),
        compiler_params=_cparams(("arbitrary",)),
        name="moe_dispatch",
    )(pad_start, pad_cnt, tail_start, dest_flat, xnp)


def _moe_kernel(be_ref, nsb_ref, na_ref, x_ref, wglu_ref, wlin_ref, bglu_ref, blin_ref, wd_ref, bd_ref, o_ref,
                xb, *accs):
    m = pl.program_id(0)
    f = pl.program_id(1)
    nf = pl.num_programs(1)
    n_sub = MOE_TM // MOE_SB

    def ffn(nsb):
        rows = nsb * MOE_SB

        @pl.when(f == 0)
        def _():
            _unpack_bf16_pairs(x_ref, rows, xb)
            for i in range(nsb):
                accs[i][...] = jnp.broadcast_to(bd_ref[0], (MOE_SB, D_MODEL))

        wd = wd_ref[0].astype(BF16)
        for i in range(nsb):
            rs = slice(i * MOE_SB, (i + 1) * MOE_SB)
            glu = jnp.minimum(_dot(xb[rs, :], wglu_ref[0]) + bglu_ref[0], SWIGLU_LIMIT)
            lin = jnp.clip(_dot(xb[rs, :], wlin_ref[0]) + blin_ref[0], -SWIGLU_LIMIT, SWIGLU_LIMIT)
            act = glu * _sigmoid(SWIGLU_ALPHA * glu) * (lin + 1.0)
            accs[i][...] += _dot(act.astype(BF16), wd)

        @pl.when(f == nf - 1)
        def _():
            for i in range(nsb):
                packed = _pack_bf16_pairs(accs[i][...])
                for sub in range(X_SUB):
                    o_ref[pl.ds(i * MOE_SB * X_SUB + sub, MOE_SB, stride=X_SUB), :] = (
                        packed[:, sub * LANES:(sub + 1) * LANES])
            if nsb < n_sub:
                o_ref[rows * X_SUB:, :] = jnp.zeros(((MOE_TM - rows) * X_SUB, LANES), U32)

    active = m < na_ref[0]
    for nsb in range(1, n_sub + 1):
        pl.when(jnp.logical_and(active, nsb_ref[m] == nsb))(functools.partial(ffn, nsb))

    @pl.when(jnp.logical_and(jnp.logical_not(active), f == nf - 1))
    def _():
        o_ref[...] = jnp.zeros_like(o_ref)


def _moe_ffn(block_expert, block_nsub, n_active, x_rows, w_glu, w_lin, b_glu, b_lin, w_down, b_down, n_blocks):
    nf = D_FF // MOE_TF

    def blk(m, na):
        return jnp.maximum(jnp.minimum(m, na[0] - 1), 0)

    def ftile(m, f, na):
        return jnp.where(m < na[0], f, nf - 1)

    def expert(m, be, na):
        return be[blk(m, na)]

    col_tile = lambda m, f, be, ns, na: (expert(m, be, na), 0, ftile(m, f, na))
    grid_spec = pltpu.PrefetchScalarGridSpec(
        num_scalar_prefetch=3,
        grid=(n_blocks, nf),
        in_specs=[pl.BlockSpec((MOE_TM * X_SUB, LANES), lambda m, f, be, ns, na: (blk(m, na), 0)),
                  pl.BlockSpec((1, D_MODEL, MOE_TF), col_tile),
                  pl.BlockSpec((1, D_MODEL, MOE_TF), col_tile),
                  pl.BlockSpec((1, 1, MOE_TF), col_tile),
                  pl.BlockSpec((1, 1, MOE_TF), col_tile),
                  pl.BlockSpec((1, MOE_TF, D_MODEL), lambda m, f, be, ns, na: (expert(m, be, na), ftile(m, f, na), 0)),
                  pl.BlockSpec((1, 1, D_MODEL), lambda m, f, be, ns, na: (expert(m, be, na), 0, 0))],
        out_specs=pl.BlockSpec((MOE_TM * X_SUB, LANES), lambda m, f, be, ns, na: (m, 0)),
        scratch_shapes=[pltpu.VMEM((MOE_TM, D_MODEL), BF16)]
        + [pltpu.VMEM((MOE_SB, D_MODEL), F32)] * (MOE_TM // MOE_SB),
    )
    return pl.pallas_call(
        _moe_kernel,
        grid_spec=grid_spec,
        out_shape=jax.ShapeDtypeStruct((n_blocks * MOE_TM * X_SUB, LANES), U32),
        compiler_params=_cparams(("arbitrary", "arbitrary")),
        name="moe_ffn",
    )(block_expert, block_nsub, n_active, x_rows, w_glu, w_lin, b_glu, b_lin, w_down, b_down)


def _combine_kernel(dest_ref, dest_next_ref, h1_ref, selg_ref, fw_ref, rows_ref, o_ref, buf, h_scr, sem):
    tt = h1_ref.shape[0]
    i = pl.program_id(0)
    n = pl.num_programs(0)
    half = D_MODEL // 2

    def row_copy(slot, t, k, d):
        src = rows_ref.at[pl.ds(pl.multiple_of(d * X_SUB, X_SUB), X_SUB)]
        dst = buf.at[slot, k, pl.ds(pl.multiple_of(t * X_SUB, X_SUB), X_SUB)]
        return pltpu.make_async_copy(src, dst, sem.at[slot])

    def gather(slot, d_ref):
        def issue(t, carry):
            for k in range(TOP_K):
                row_copy(slot, t, k, d_ref[t * TOP_K + k]).start(priority=k % 2)
            return carry

        lax.fori_loop(0, tt, issue, 0, unroll=4)

    slot = i % 2

    @pl.when(i == 0)
    def _():
        gather(0, dest_ref)

    @pl.when(i + 1 < n)
    def _():
        gather(1 - slot, dest_next_ref)

    for k in range(TOP_K):
        pltpu.make_async_copy(rows_ref.at[pl.ds(0, tt * X_SUB)], buf.at[slot, k], sem.at[slot]).wait()

    gates = selg_ref[...]
    for sub in range(X_SUB):
        lo_cs = slice(sub * LANES, (sub + 1) * LANES)
        hi_cs = slice(half + sub * LANES, half + (sub + 1) * LANES)
        h_lo = h1_ref[:, lo_cs]
        h_hi = h1_ref[:, hi_cs]
        for k in range(TOP_K):
            u = buf[slot, k, pl.ds(sub, tt, stride=X_SUB), :]
            g = gates[:, k:k + 1]
            h_lo = h_lo + g * pltpu.bitcast(u << 16, F32)
            h_hi = h_hi + g * pltpu.bitcast(u & jnp.uint32(0xFFFF0000), F32)
        h_scr[:, lo_cs] = h_lo
        h_scr[:, hi_cs] = h_hi
    h = h_scr[...]
    o_ref[...] = h * lax.rsqrt(jnp.mean(h * h, axis=-1, keepdims=True) + EPS) * fw_ref[...]


def _combine(dest_flat, h1, selg, final_w, y_rows):
    t = h1.shape[0]
    tt = COMB_TT
    n = t // tt
    return pl.pallas_call(
        _combine_kernel,
        grid=(n,),
        in_specs=[pl.BlockSpec((tt * TOP_K,), lambda i: (i,), memory_space=pltpu.SMEM),
                  pl.BlockSpec((tt * TOP_K,), lambda i: (jnp.minimum(i + 1, n - 1),), memory_space=pltpu.SMEM),
                  pl.BlockSpec((tt, D_MODEL), lambda i: (i, 0)),
                  pl.BlockSpec((tt, LANES), lambda i: (i, 0)),
                  pl.BlockSpec((1, D_MODEL), lambda i: (0, 0)),
                  pl.BlockSpec(memory_space=pl.ANY)],
        out_specs=pl.BlockSpec((tt, D_MODEL), lambda i: (i, 0)),
        out_shape=jax.ShapeDtypeStruct((t, D_MODEL), F32),
        scratch_shapes=[pltpu.VMEM((2, TOP_K, tt * X_SUB, LANES), U32), pltpu.VMEM((tt, D_MODEL), F32),
                        pltpu.SemaphoreType.DMA((2,))],
        compiler_params=_cparams(("arbitrary",)),
        name="moe_combine",
    )(dest_flat, dest_flat, h1, selg, final_w, y_rows)


def kernel(x, norm1_w, w_in, conv_w, conv_b, dt_bias, a_log, d_skip, ssd_norm_w, i_bias, f_bias, ml_norm_w,
           w_out, norm2_w, router_w, router_b, w_gate_up, b_gate_up, w_down, b_down, final_norm_w):
    b, s, d = x.shape
    t = b * s
    assert w_in.shape[0] == 1, "single-layer problem"
    h = x.reshape(t, d)
    for l in range(1):
        wi = w_in[l]
        o_z, o_xbc, o_dt = 0, D_SSD, D_SSD + D_SSD + 2 * SSD_GROUPS * SSD_STATE
        o_q = o_dt + SSD_HEADS
        o_k = o_q + ML_HEADS * ML_DQK
        o_v = o_k + ML_HEADS * ML_DQK
        o_o = o_v + D_ML
        o_i = o_o + D_ML
        o_f = o_i + ML_HEADS
        w_big = jnp.concatenate([wi[:, o_z:o_dt], wi[:, o_q:o_i]], axis=1).astype(BF16)
        w_small = jnp.concatenate(
            [wi[:, o_dt:o_q], wi[:, o_i:o_f], wi[:, o_f:o_f + ML_HEADS],
             jnp.zeros((d, LANES - SSD_HEADS - 2 * ML_HEADS), F32)], axis=1).astype(BF16)

        proj, small = _input_projection(h, norm1_w[l].reshape(1, d), w_big, w_small)
        proj3 = proj.reshape(b, s, D_BIG)
        small3 = small.reshape(b, s, LANES)
        y_ssd = _ssd_mixer(proj3, small3, conv_w[l], conv_b[l], dt_bias[l], a_log[l], d_skip[l], ssd_norm_w[l], b, s)
        y_ml, w_glu, w_lin = _mlstm_mixer(proj3, small3, i_bias[l], f_bias[l], ml_norm_w[l], b, s, w_gate_up[l])

        rw = jnp.pad(router_w[l], ((0, 0), (0, LANES - N_EXPERTS)))
        rw_hi = rw.astype(BF16)
        rw = jnp.concatenate([rw_hi, (rw - rw_hi.astype(F32)).astype(BF16)], axis=1)
        rb = jnp.pad(router_b[l], (0, LANES - N_EXPERTS)).reshape(1, LANES)
        h1, xnp, seli, selg, cnt = _outproj_router(
            h, y_ssd.reshape(t, D_SSD), y_ml.reshape(t, D_ML), w_out[l].astype(BF16), norm2_w[l].reshape(1, d), rw, rb)

        n_blocks = (t * TOP_K) // MOE_TM + N_EXPERTS
        dest, table = _route_offsets(seli, cnt, n_blocks)
        dest_flat = dest[:, :TOP_K].reshape(-1)
        block_expert = table[:n_blocks, 0]
        n_active = table[0:1, 1]
        block_nsub = table[:n_blocks, 2]

        x_rows = _dispatch(table[:N_EXPERTS, 3], table[:N_EXPERTS, 4], table[0:1, 5], dest_flat, xnp,
                           n_blocks * MOE_TM)
        b_glu = b_gate_up[l][:, 0::2].reshape(N_EXPERTS, 1, D_FF)
        b_lin = b_gate_up[l][:, 1::2].reshape(N_EXPERTS, 1, D_FF)
        y_rows = _moe_ffn(block_expert, block_nsub, n_active, x_rows, w_glu, w_lin, b_glu, b_lin, w_down[l],
                          b_down[l].reshape(N_EXPERTS, 1, d), n_blocks)
        h = _combine(dest_flat, h1, selg, final_norm_w.reshape(1, d), y_rows)
    return h.reshape(b, s, d)
```

```python
import functools

import jax
import jax.numpy as jnp
from jax import lax
from jax.experimental import pallas as pl
from jax.experimental.pallas import tpu as pltpu

F32 = jnp.float32
BF16 = jnp.bfloat16
I32 = jnp.int32
U32 = jnp.uint32

LANES = 128
SUBLANES = 8
VMEM_LIMIT = 56 * 1024 * 1024

D_MODEL = 2048
EPS = 1e-5
SSD_HEAD_DIM = 64
SSD_HEADS = 32
D_SSD = 2048
SSD_GROUPS = 4
SSD_STATE = 128
SSD_CONV = 4
CHUNK = 128
ML_HEADS = 8
ML_DV = 256
ML_DQK = 128
D_ML = 2048
N_EXPERTS = 32
TOP_K = 4
D_FF = 2048
SWIGLU_ALPHA = 1.702
SWIGLU_LIMIT = 7.0

COL_Z, COL_X, COL_BC, COL_Q, COL_K, COL_V, COL_O = 0, 2048, 4096, 5120, 6144, 7168, 9216
COL_W = 1024
D_BIG = 11264
LANE_DT, LANE_I, LANE_F = 0, 32, 40

IN_TM, IN_TN = 1024, 1024
OUT_TM = 512
ROUTE_TM = 2048
DISP_TT = 512
MOE_TM = 512
MOE_SB = 256
MOE_TF = 1024
PERM_W = 256
PREP_TN = 1024
COMB_TT = 256
X_SUB = D_MODEL // 2 // LANES


def _cparams(sem, vmem=VMEM_LIMIT):
    return pltpu.CompilerParams(dimension_semantics=sem, vmem_limit_bytes=vmem)


def _dot(a, b):
    return jnp.dot(a, b, preferred_element_type=F32)


def _dot_exact(a, b):
    return jnp.dot(a, b, preferred_element_type=F32, precision=lax.Precision.HIGHEST)


def _split2_dot(x, e2_bf16):
    hi = x.astype(BF16)
    lo = (x - hi.astype(F32)).astype(BF16)
    return _dot(jnp.concatenate([hi, lo], axis=1), e2_bf16)


def _sigmoid(x):
    return 1.0 / (1.0 + jnp.exp(-x))


def _softplus(x):
    return jnp.maximum(x, 0.0) + jnp.log1p(jnp.exp(-jnp.abs(x)))


def _tril(n, strict=False):
    r = lax.broadcasted_iota(I32, (n, n), 0)
    c = lax.broadcasted_iota(I32, (n, n), 1)
    return (r > c) if strict else (r >= c)


def _pack_bf16_pairs(y):
    half = D_MODEL // 2
    lo = pltpu.bitcast(y[:, :half].astype(BF16).astype(F32), U32)
    hi = pltpu.bitcast(y[:, half:].astype(BF16).astype(F32), U32)
    return hi | (lo >> 16)


def _unpack_bf16_pairs(src_ref, rows, dst_ref):
    half = D_MODEL // 2
    for sub in range(X_SUB):
        u = src_ref[pl.ds(sub, rows, stride=X_SUB), :]
        dst_ref[0:rows, sub * LANES:(sub + 1) * LANES] = pltpu.bitcast(u << 16, F32).astype(BF16)
        dst_ref[0:rows, half + sub * LANES:half + (sub + 1) * LANES] = (
            pltpu.bitcast(u & jnp.uint32(0xFFFF0000), F32).astype(BF16))


def _deinterleave_tile(w_ref, p_ref, glu_ref, lin_ref):
    hw = PERM_W // 2
    for s in range(w_ref.shape[2] // PERM_W):
        t = _dot(w_ref[0, :, s * PERM_W:(s + 1) * PERM_W].astype(BF16), p_ref[...])
        glu_ref[0, :, s * hw:(s + 1) * hw] = t[:, :hw].astype(BF16)
        lin_ref[0, :, s * hw:(s + 1) * hw] = t[:, hw:].astype(BF16)


def _deinterleave_perm():
    src = jnp.arange(PERM_W)
    dst = jnp.where(src % 2 == 0, src // 2, PERM_W // 2 + src // 2)
    return (dst[:, None] == jnp.arange(PERM_W)[None, :]).astype(BF16)


def _deinterleave_specs(w_gate_up, tile_index):
    e, k, n = w_gate_up.shape

    def at(*ids):
        ei, ci = tile_index(*ids)
        return ei, 0, ci

    in_specs = [pl.BlockSpec((1, k, PREP_TN), at), pl.BlockSpec((PERM_W, PERM_W), lambda *ids: (0, 0))]
    out_specs = [pl.BlockSpec((1, k, PREP_TN // 2), at)] * 2
    out_shape = [jax.ShapeDtypeStruct((e, k, n // 2), BF16)] * 2
    return in_specs, out_specs, out_shape


def _deinterleave_gate_up(w_gate_up):
    e, _, n = w_gate_up.shape
    in_specs, out_specs, out_shape = _deinterleave_specs(w_gate_up, lambda i, j: (i, j))
    return pl.pallas_call(
        _deinterleave_tile,
        grid=(e, n // PREP_TN),
        in_specs=in_specs,
        out_specs=out_specs,
        out_shape=out_shape,
        compiler_params=_cparams(("arbitrary", "arbitrary")),
        name="moe_deinterleave",
    )(w_gate_up, _deinterleave_perm())


def _inproj_kernel(x_ref, nw_ref, wh_ref, wt_ref, ws_ref, o_ref, os_ref, n_scr, *, head_tiles):
    j = pl.program_id(1)

    @pl.when(j == 0)
    def _():
        x = x_ref[...]
        n = x * lax.rsqrt(jnp.mean(x * x, axis=-1, keepdims=True) + EPS) * nw_ref[...]
        nb = n.astype(BF16)
        n_scr[...] = nb
        os_ref[...] = _dot(nb, ws_ref[...])

    @pl.when(j < head_tiles)
    def _():
        o_ref[...] = _dot(n_scr[...], wh_ref[...].astype(BF16)).astype(BF16)

    @pl.when(j >= head_tiles)
    def _():
        o_ref[...] = _dot(n_scr[...], wt_ref[...]).astype(BF16)


def _input_projection(xf, norm_w, w_in, w_tail, w_small, head_cols):
    t = xf.shape[0]
    head_tiles = head_cols // IN_TN
    return pl.pallas_call(
        functools.partial(_inproj_kernel, head_tiles=head_tiles),
        grid=(t // IN_TM, D_BIG // IN_TN),
        in_specs=[pl.BlockSpec((IN_TM, D_MODEL), lambda i, j: (i, 0)),
                  pl.BlockSpec((1, D_MODEL), lambda i, j: (0, 0)),
                  pl.BlockSpec((D_MODEL, IN_TN), lambda i, j: (0, jnp.minimum(j, head_tiles - 1))),
                  pl.BlockSpec((D_MODEL, IN_TN), lambda i, j: (0, jnp.maximum(j - head_tiles, 0))),
                  pl.BlockSpec((D_MODEL, LANES), lambda i, j: (0, 0))],
        out_specs=[pl.BlockSpec((IN_TM, IN_TN), lambda i, j: (i, j)),
                   pl.BlockSpec((IN_TM, LANES), lambda i, j: (i, 0))],
        out_shape=[jax.ShapeDtypeStruct((t, D_BIG), BF16),
                   jax.ShapeDtypeStruct((t, LANES), F32)],
        scratch_shapes=[pltpu.VMEM((IN_TM, D_MODEL), BF16)],
        compiler_params=_cparams(("arbitrary", "arbitrary")),
        name="inproj",
    )(xf, norm_w, w_in, w_tail, w_small)


def _ssd_kernel(z_ref, x_ref, bc_ref, sm_ref, cw_ref, cb_ref, shift_ref, dtb_ref, alog_ref,
                dskip_ref, nw_ref, e_ref, o_ref, ext, y_scr, *states):
    q = CHUNK

    @pl.when(pl.program_id(1) == 0)
    def _():
        ext[0:q, :] = jnp.zeros((q, ext.shape[1]), BF16)
        for state in states:
            state[...] = jnp.zeros_like(state)

    ext[q:2 * q, 0:D_SSD] = x_ref[0]
    ext[q:2 * q, D_SSD:] = bc_ref[0]
    shifted = _dot(shift_ref[...], ext[...])
    acc = cb_ref[...] + ext[q:2 * q, :].astype(F32) * cw_ref[SSD_CONV - 1:SSD_CONV, :]
    for j in range(SSD_CONV - 1):
        acc = acc + shifted[j * q:(j + 1) * q, :] * cw_ref[j:j + 1, :]
    xbc = acc * _sigmoid(acc)
    xs = xbc[:, :D_SSD]
    bcs = xbc[:, D_SSD:]
    ext[0:q, :] = ext[q:2 * q, :]

    lane = lax.broadcasted_iota(I32, (1, LANES), 1)
    g = sm_ref[0]
    dt = _softplus(g + dtb_ref[...])
    a = jnp.where(lane < SSD_HEADS, -jnp.exp(alog_ref[...]), 0.0)
    da = dt * a
    tri = _tril(q).astype(F32)
    a_cum = _dot_exact(tri, da)
    a_cum_t = a_cum.T
    a_end = a_cum[q - 1:q, :]
    stack = jnp.concatenate(
        [dt, jnp.exp(a_end - a_cum), jnp.exp(a_cum), jnp.broadcast_to(jnp.exp(a_end), (SUBLANES, LANES))], axis=0)
    ex = _split2_dot(stack, e_ref[...])
    dt_x = ex[0:q]
    to_end_x = ex[q:2 * q]
    from_start_x = ex[2 * q:3 * q]
    chunk_decay_x = ex[3 * q:3 * q + 1]

    xdt = xs * dt_x
    xdt_b = xdt.astype(BF16)
    xde_b = (xdt * to_end_x).astype(BF16)
    causal = _tril(q)
    lane_q = lax.broadcasted_iota(I32, (1, LANES), 1)
    gw = SSD_STATE
    hpg = SSD_HEADS // SSD_GROUPS
    cpg = D_SSD // SSD_GROUPS
    for gi in range(SSD_GROUPS):
        bg = bcs[:, gi * gw:(gi + 1) * gw]
        cg_b = bcs[:, SSD_GROUPS * gw + gi * gw:SSD_GROUPS * gw + (gi + 1) * gw].astype(BF16)
        cb = lax.dot_general(cg_b, bg.astype(BF16), (((1,), (1,)), ((), ())), preferred_element_type=F32)
        bg_t = bg.T.astype(BF16)
        cs = slice(gi * cpg, (gi + 1) * cpg)
        prev = states[gi][...]
        y_off = _dot(cg_b, prev.astype(BF16)) * from_start_x[:, cs]
        states[gi][...] = prev * chunk_decay_x[:, cs] + _dot(bg_t, xde_b[:, cs])
        for pr in range(hpg // 2):
            c0 = gi * cpg + pr * LANES
            xp = xdt_b[:, c0:c0 + LANES]
            acc = y_off[:, pr * LANES:(pr + 1) * LANES]
            for sub in range(2):
                h = gi * hpg + pr * 2 + sub
                seg = a_cum[:, h:h + 1] - a_cum_t[h:h + 1, :]
                decay = jnp.exp(jnp.where(causal, seg, -jnp.inf))
                m = (cb * decay).astype(BF16)
                keep = (lane_q < SSD_HEAD_DIM) if sub == 0 else (lane_q >= SSD_HEAD_DIM)
                acc = acc + _dot(m, jnp.where(keep, xp, jnp.zeros_like(xp)))
            y_scr[:, c0:c0 + LANES] = acc

    y = y_scr[...] + xs * dskip_ref[...]
    zf = z_ref[0].astype(F32)
    y = y * (zf * _sigmoid(zf))
    for gi in range(SSD_GROUPS):
        cs = slice(gi * cpg, (gi + 1) * cpg)
        yg = y[:, cs]
        yg = yg * lax.rsqrt(jnp.mean(yg * yg, axis=-1, keepdims=True) + EPS)
        o_ref[0, :, cs] = (yg * nw_ref[:, cs]).astype(BF16)


def _ssd_mixer(proj, small, conv_w, conv_b, dt_bias, a_log, d_skip, norm_w, b, s):
    nc = s // CHUNK
    pad = LANES - SSD_HEADS
    dtb = jnp.pad(dt_bias, (0, pad)).reshape(1, LANES)
    alog = jnp.pad(a_log, (0, pad)).reshape(1, LANES)
    dskip = jnp.repeat(d_skip, SSD_HEAD_DIM).reshape(1, D_SSD)
    expand = (jnp.arange(LANES)[:, None] == (jnp.arange(D_SSD) // SSD_HEAD_DIM)[None, :]).astype(BF16)
    expand = jnp.concatenate([expand, expand], axis=0)
    d_bc = 2 * SSD_GROUPS * SSD_STATE
    d_conv = D_SSD + d_bc
    tt = jnp.arange((SSD_CONV - 1) * CHUNK)
    back = (SSD_CONV - 1) - tt // CHUNK
    shift = ((CHUNK + tt % CHUNK - back)[:, None] == jnp.arange(2 * CHUNK)[None, :]).astype(BF16)
    const = lambda shape: pl.BlockSpec(shape, lambda i, j: (0,) * len(shape))
    col = lambda width, off: pl.BlockSpec((1, CHUNK, width), lambda i, j: (i, j, off // width))
    return pl.pallas_call(
        _ssd_kernel,
        grid=(b, nc),
        in_specs=[col(D_SSD, COL_Z), col(D_SSD, COL_X), col(d_bc, COL_BC), col(LANES, 0),
                  const((SSD_CONV, d_conv)), const((1, d_conv)), const(((SSD_CONV - 1) * CHUNK, 2 * CHUNK)),
                  const((1, LANES)), const((1, LANES)), const((1, D_SSD)), const((1, D_SSD)),
                  const((2 * LANES, D_SSD))],
        out_specs=pl.BlockSpec((1, CHUNK, D_SSD), lambda i, j: (i, j, 0)),
        out_shape=jax.ShapeDtypeStruct((b, s, D_SSD), BF16),
        scratch_shapes=[pltpu.VMEM((2 * CHUNK, d_conv), BF16),
                        pltpu.VMEM((CHUNK, D_SSD), F32)]
        + [pltpu.VMEM((SSD_STATE, D_SSD // SSD_GROUPS), F32)] * SSD_GROUPS,
        compiler_params=_cparams(("arbitrary", "arbitrary")),
        name="ssd_mixer",
    )(proj, proj, proj, small, conv_w, conv_b.reshape(1, -1), shift, dtb, alog, dskip, norm_w.reshape(1, -1), expand)


def _mlstm_kernel(q_ref, k_ref, v0_ref, v1_ref, o0_ref, o1_ref, sm_ref, bias_ref, nw_ref, *rest, prep_gate_up):
    L = CHUNK
    heads_per_ref = COL_W // ML_DV

    def head_cols(refs, h):
        c0 = (h % heads_per_ref) * ML_DV
        return refs[h // heads_per_ref][0, :, c0:c0 + ML_DV]

    if prep_gate_up:
        wg_ref, perm_ref, out_ref, glu_ref, lin_ref, n_scr, m_scr, *ct_scrs = rest
        _deinterleave_tile(wg_ref, perm_ref, glu_ref, lin_ref)
    else:
        out_ref, n_scr, m_scr, *ct_scrs = rest

    @pl.when(pl.program_id(1) == 0)
    def _():
        for ct_scr in ct_scrs:
            ct_scr[...] = jnp.zeros_like(ct_scr)
        n_scr[...] = jnp.zeros_like(n_scr)
        m_scr[...] = jnp.zeros_like(m_scr)

    n_all = n_scr[...]
    m_all = m_scr[...]
    n_new_all = n_all
    m_new_all = m_all
    head_row = lax.broadcasted_iota(I32, (ML_HEADS, LANES), 0)

    g = sm_ref[0] + bias_ref[...]
    log_f = jnp.minimum(g, 0.0) - jnp.log1p(jnp.exp(-jnp.abs(g)))
    tri = _tril(L).astype(F32)
    bcum = _dot_exact(tri, log_f)
    bcum_t = bcum.T
    keys_before = jnp.logical_not(_tril(L, strict=True))
    row16 = lax.broadcasted_iota(I32, (2 * SUBLANES, ML_DQK), 0)
    scale = ML_DQK ** -0.5
    nt = (((1,), (1,)), ((), ()))
    tn = (((0,), (0,)), ((), ()))
    for h in range(ML_HEADS):
        li, lf = LANE_I + h, LANE_F + h
        li_col, b_col, b_row = g[:, li:li + 1], bcum[:, lf:lf + 1], bcum_t[lf:lf + 1, :]
        btot = bcum[L - 1:L, lf:lf + 1]
        m_prev = m_all[h:h + 1, 0:1]
        ct_prev = ct_scrs[h][...]
        n_prev = n_all[h:h + 1, :]
        qb = q_ref[0, :, h * ML_DQK:(h + 1) * ML_DQK]
        kb = k_ref[0, :, h * ML_DQK:(h + 1) * ML_DQK]
        vb = head_cols((v0_ref, v1_ref), h)
        w_end = btot - b_col + li_col
        m_loc = jnp.max(w_end, axis=0, keepdims=True)
        k_end = kb.astype(F32) * jnp.exp(w_end - m_loc)
        kv_t = lax.dot_general(vb, k_end.astype(BF16), tn, preferred_element_type=F32)
        nk = jnp.sum(k_end, axis=0, keepdims=True)
        d_log = jnp.where(keys_before, (li_col - b_col) + b_row, -jnp.inf)
        m_inter = b_row + m_prev
        m_t = jnp.maximum(m_inter, jnp.max(d_log, axis=0, keepdims=True))
        s_t = lax.dot_general(kb, qb, nt, preferred_element_type=F32)
        s_t = s_t * (scale * jnp.exp(d_log - m_t))
        inter = scale * jnp.exp(m_inter - m_t)
        num_t = (lax.dot_general(vb, s_t.astype(BF16), tn, preferred_element_type=F32)
                 + lax.dot_general(ct_prev.astype(BF16), qb, nt, preferred_element_type=F32) * inter)
        n_hi = n_prev.astype(BF16).astype(F32)
        n_rows = jnp.where(row16 == 0, n_hi, jnp.where(row16 == 1, n_prev - n_hi, 0.0)).astype(BF16)
        qn = lax.dot_general(n_rows, qb, nt, preferred_element_type=F32)
        den = jnp.sum(s_t, axis=0, keepdims=True) + (qn[0:1, :] + qn[1:2, :]) * inter
        den = jnp.maximum(jnp.abs(den), jnp.exp(-m_t))
        ht_t = num_t / den
        ht_t = ht_t * lax.rsqrt(jnp.mean(ht_t * ht_t, axis=0, keepdims=True) + EPS)
        vs = slice(h * ML_DV, (h + 1) * ML_DV)
        og = _sigmoid(head_cols((o0_ref, o1_ref), h).astype(F32))
        out_ref[0, :, vs] = (ht_t.T * nw_ref[:, vs] * og).astype(BF16)
        m_new = jnp.maximum(btot + m_prev, m_loc)
        fa = jnp.exp(btot + m_prev - m_new)
        fg = jnp.exp(m_loc - m_new)
        ct_scrs[h][...] = fa * ct_prev + fg * kv_t
        n_new_all = jnp.where(head_row == h, fa * n_prev + fg * nk, n_new_all)
        m_new_all = jnp.where(head_row == h, m_new, m_new_all)
    n_scr[...] = n_new_all
    m_scr[...] = m_new_all


def _mlstm_mixer(proj, small, i_bias, f_bias, norm_w, b, s, w_gate_up):
    nc = s // CHUNK
    bias = jnp.zeros((LANES,), F32).at[LANE_I:LANE_I + ML_HEADS].set(i_bias).at[LANE_F:LANE_F + ML_HEADS].set(f_bias)
    d_qk = ML_HEADS * ML_DQK
    const = lambda shape: pl.BlockSpec(shape, lambda i, j: (0,) * len(shape))
    col = lambda width, off: pl.BlockSpec((1, CHUNK, width), lambda i, j: (i, j, off // width))
    e, _, n = w_gate_up.shape
    tiles_per_expert = n // PREP_TN
    prep = b * nc == e * tiles_per_expert
    in_specs = [col(d_qk, COL_Q), col(d_qk, COL_K), col(COL_W, COL_V), col(COL_W, COL_V + COL_W),
                col(COL_W, COL_O), col(COL_W, COL_O + COL_W), col(LANES, 0), const((1, LANES)), const((1, D_ML))]
    out_specs = [pl.BlockSpec((1, CHUNK, D_ML), lambda i, j: (i, j, 0))]
    out_shape = [jax.ShapeDtypeStruct((b, s, D_ML), BF16)]
    args = [proj] * 6 + [small, bias.reshape(1, LANES), norm_w.reshape(1, -1)]
    if prep:
        w_in_specs, w_out_specs, w_out_shape = _deinterleave_specs(
            w_gate_up, lambda i, j: ((i * nc + j) // tiles_per_expert, (i * nc + j) % tiles_per_expert))
        in_specs += w_in_specs
        out_specs += w_out_specs
        out_shape += w_out_shape
        args += [w_gate_up, _deinterleave_perm()]
    outs = pl.pallas_call(
        functools.partial(_mlstm_kernel, prep_gate_up=prep),
        grid=(b, nc),
        in_specs=in_specs,
        out_specs=out_specs,
        out_shape=out_shape,
        scratch_shapes=[pltpu.VMEM((ML_HEADS, ML_DQK), F32), pltpu.VMEM((ML_HEADS, LANES), F32)]
        + [pltpu.VMEM((ML_DV, ML_DQK), F32)] * ML_HEADS,
        compiler_params=_cparams(("arbitrary", "arbitrary")),
        name="mlstm_mixer",
    )(*args)
    if prep:
        return outs[0], outs[1], outs[2]
    return (outs[0], *_deinterleave_gate_up(w_gate_up))


def _outproj_router_kernel(x_ref, ys_ref, ym_ref, wo_ref, n2w_ref, rw_ref, rb_ref,
                           h1_ref, xnp_ref, seli_ref, selg_ref, cnt_ref, carry):
    tm = x_ref.shape[0]

    @pl.when(pl.program_id(0) == 0)
    def _():
        carry[...] = jnp.zeros_like(carry)

    h1 = x_ref[...] + _dot(ys_ref[...], wo_ref[0:D_SSD, :]) + _dot(ym_ref[...], wo_ref[D_SSD:, :])
    h1_ref[...] = h1
    xn = h1 * lax.rsqrt(jnp.mean(h1 * h1, axis=-1, keepdims=True) + EPS) * n2w_ref[...]
    packed = _pack_bf16_pairs(xn)
    for sub in range(X_SUB):
        xnp_ref[pl.ds(sub, tm, stride=X_SUB), :] = packed[:, sub * LANES:(sub + 1) * LANES]

    lane = lax.broadcasted_iota(I32, (tm, LANES), 1)
    lane_f = lane.astype(F32)
    xn_hi = xn.astype(BF16)
    xn_lo = (xn - xn_hi.astype(F32)).astype(BF16)
    hh_hl = _dot(xn_hi, rw_ref[...])
    logits = hh_hl[:, :LANES] + hh_hl[:, LANES:] + _dot(xn_lo, rw_ref[:, :LANES]) + rb_ref[...]
    work = jnp.where(lane < N_EXPERTS, logits, -jnp.inf)
    vals, idxs = [], []
    chosen = jnp.zeros((tm, LANES), jnp.bool_)
    for _ in range(TOP_K):
        m = jnp.max(work, axis=1, keepdims=True)
        idx = jnp.min(jnp.where(work == m, lane_f, float(LANES)), axis=1, keepdims=True)
        sel = lane_f == idx
        vals.append(m)
        idxs.append(idx)
        chosen = jnp.logical_or(chosen, sel)
        work = jnp.where(sel, -jnp.inf, work)
    exps = [jnp.exp(v - vals[0]) for v in vals]
    denom = exps[0] + exps[1] + exps[2] + exps[3]

    onehot = jnp.where(chosen, 1.0, 0.0)
    ranks = _dot(_tril(tm, strict=True).astype(BF16), onehot.astype(BF16)) + carry[...]
    total = carry[...] + jnp.sum(onehot, axis=0, keepdims=True)
    carry[...] = total
    cnt_ref[...] = jnp.broadcast_to(total, cnt_ref.shape)

    seli = jnp.zeros((tm, LANES), F32)
    selg = jnp.zeros((tm, LANES), F32)
    for k in range(TOP_K):
        rank_k = jnp.sum(jnp.where(lane_f == idxs[k], ranks, 0.0), axis=1, keepdims=True)
        seli = jnp.where(lane == k, idxs[k], seli)
        seli = jnp.where(lane == TOP_K + k, rank_k, seli)
        selg = jnp.where(lane == k, exps[k] / denom, selg)
    seli_ref[...] = seli.astype(I32)
    selg_ref[...] = selg


def _outproj_router(xf, y_ssd, y_ml, w_out_b, norm2_w, router_w_pad, router_b_pad):
    t = xf.shape[0]
    tm = OUT_TM
    row = lambda width: pl.BlockSpec((tm, width), lambda i: (i, 0))
    const = lambda shape: pl.BlockSpec(shape, lambda i: (0,) * len(shape))
    return pl.pallas_call(
        _outproj_router_kernel,
        grid=(t // tm,),
        in_specs=[row(D_MODEL), row(D_SSD), row(D_ML),
                  pl.BlockSpec((D_SSD + D_ML, D_MODEL), lambda i: (0, 0), pipeline_mode=pl.Buffered(1)),
                  const((1, D_MODEL)),
                  const((D_MODEL, 2 * LANES)), const((1, LANES))],
        out_specs=[row(D_MODEL), pl.BlockSpec((tm * X_SUB, LANES), lambda i: (i, 0)), row(LANES), row(LANES),
                   const((SUBLANES, LANES))],
        out_shape=[jax.ShapeDtypeStruct((t, D_MODEL), F32),
                   jax.ShapeDtypeStruct((t * X_SUB, LANES), U32),
                   jax.ShapeDtypeStruct((t, LANES), I32),
                   jax.ShapeDtypeStruct((t, LANES), F32),
                   jax.ShapeDtypeStruct((SUBLANES, LANES), F32)],
        scratch_shapes=[pltpu.VMEM((1, LANES), F32)],
        compiler_params=_cparams(("arbitrary",)),
        name="outproj_router",
    )(xf, y_ssd, y_ml, w_out_b, norm2_w, router_w_pad, router_b_pad)


def _route_kernel(seli_ref, cnt_ref, dest_ref, be_ref):
    tm = seli_ref.shape[0]
    nbp = be_ref.shape[0]
    lane1 = lax.broadcasted_iota(I32, (1, LANES), 1)
    cnt = jnp.where(lane1 < N_EXPERTS, cnt_ref[0:1, :], 0.0)
    padded = jnp.ceil(cnt / MOE_TM) * MOE_TM
    r = lax.broadcasted_iota(I32, (LANES, LANES), 0)
    c = lax.broadcasted_iota(I32, (LANES, LANES), 1)
    upper = jnp.where(r <= c, 1.0, 0.0)
    pend = jnp.round(_dot_exact(jnp.broadcast_to(padded, (SUBLANES, LANES)), upper))[0:1, :]
    pstart = pend - padded

    lane = lax.broadcasted_iota(I32, (tm, LANES), 1)
    lane_f = lane.astype(F32)
    seli = seli_ref[...].astype(F32)
    dest = jnp.zeros((tm, LANES), F32)
    for k in range(TOP_K):
        idx_k = jnp.sum(jnp.where(lane == k, seli, 0.0), axis=1, keepdims=True)
        rank_k = jnp.sum(jnp.where(lane == TOP_K + k, seli, 0.0), axis=1, keepdims=True)
        start_k = jnp.sum(jnp.where(lane_f == idx_k, pstart, 0.0), axis=1, keepdims=True)
        dest = jnp.where(lane == k, start_k + rank_k, dest)
    dest_ref[...] = dest.astype(I32)

    blk_start = (lax.broadcasted_iota(I32, (nbp, LANES), 0) * MOE_TM).astype(F32)
    lane_b = lax.broadcasted_iota(I32, (nbp, LANES), 1)
    passed = jnp.where(jnp.logical_and(lane_b < N_EXPERTS, pend <= blk_start), 1.0, 0.0)
    expert = jnp.minimum(jnp.sum(passed, axis=1, keepdims=True), float(N_EXPERTS - 1))
    n_active = pend[:, N_EXPERTS - 1:N_EXPERTS] / MOE_TM
    tok_end = jnp.sum(jnp.where(lane_b.astype(F32) == expert, pstart + cnt, 0.0), axis=1, keepdims=True)
    valid = jnp.clip(tok_end - blk_start[:, 0:1], 0.0, float(MOE_TM))
    n_sub = jnp.ceil(valid / MOE_SB)
    own = lane_b == lax.broadcasted_iota(I32, (nbp, LANES), 0)
    pad_start = jnp.sum(jnp.where(own, pstart + cnt, 0.0), axis=1, keepdims=True)
    pad_cnt = jnp.sum(jnp.where(own, padded - cnt, 0.0), axis=1, keepdims=True)
    cols = (expert, n_active, n_sub, pad_start, pad_cnt, pend[:, N_EXPERTS - 1:N_EXPERTS])
    be = jnp.zeros((nbp, LANES), F32)
    for j, v in enumerate(cols):
        be = jnp.where(lane_b == j, v, be)
    be_ref[...] = be.astype(I32)


def _route_offsets(seli, cnt, n_blocks):
    t = seli.shape[0]
    nbp = -(-n_blocks // SUBLANES) * SUBLANES
    return pl.pallas_call(
        _route_kernel,
        grid=(t // ROUTE_TM,),
        in_specs=[pl.BlockSpec((ROUTE_TM, LANES), lambda i: (i, 0)),
                  pl.BlockSpec((SUBLANES, LANES), lambda i: (0, 0))],
        out_specs=[pl.BlockSpec((ROUTE_TM, LANES), lambda i: (i, 0)),
                   pl.BlockSpec((nbp, LANES), lambda i: (0, 0))],
        out_shape=[jax.ShapeDtypeStruct((t, LANES), I32),
                   jax.ShapeDtypeStruct((nbp, LANES), I32)],
        compiler_params=_cparams(("arbitrary",)),
        name="route_offsets",
    )(seli, cnt)


def _dispatch_kernel(pad_start_ref, pad_cnt_ref, tail_ref, dest_ref, xnp_ref, rows_ref, zbuf, sem, zsem, *, n_zero):
    tt = xnp_ref.shape[0] // X_SUB
    n_rows = rows_ref.shape[0] // X_SUB
    first = pl.program_id(0) == 0

    def zero_rows(start_row, n):
        dst = rows_ref.at[pl.ds(pl.multiple_of(start_row * X_SUB, X_SUB), n * X_SUB)]
        return pltpu.make_async_copy(zbuf.at[pl.ds(0, n * X_SUB)], dst, zsem)

    @pl.when(first)
    def _():
        zbuf[...] = jnp.zeros_like(zbuf)

        def expert_padding(e, carry):
            start, count = pad_start_ref[e], pad_cnt_ref[e]
            for k in reversed(range(MOE_TM.bit_length() - 1)):
                run = 1 << k
                pl.when((count & run) != 0)(
                    lambda run=run: zero_rows(start + (count & ~(2 * run - 1)), run).start())
            return carry

        lax.fori_loop(0, N_EXPERTS, expert_padding, 0)

        def trailing_block(j, carry):
            start = tail_ref[0] + j * MOE_TM
            pl.when(start < n_rows)(lambda: zero_rows(start, MOE_TM).start())
            return carry

        lax.fori_loop(0, N_EXPERTS, trailing_block, 0)

    def tile_copy(t, d):
        src = xnp_ref.at[pl.ds(pl.multiple_of(t * X_SUB, X_SUB), X_SUB)]
        dst = rows_ref.at[pl.ds(pl.multiple_of(d * X_SUB, X_SUB), X_SUB)]
        return pltpu.make_async_copy(src, dst, sem)

    def issue(t, carry):
        for k in range(TOP_K):
            tile_copy(t, dest_ref[t * TOP_K + k]).start(priority=k % 2)
        return carry

    lax.fori_loop(0, tt, issue, 0, unroll=4)
    for _ in range(TOP_K):
        pltpu.make_async_copy(xnp_ref, rows_ref.at[pl.ds(0, tt * X_SUB)], sem).wait()

    @pl.when(first)
    def _():
        span = rows_ref.at[pl.ds(0, n_zero * X_SUB)]
        pltpu.make_async_copy(span, span, zsem).wait()


def _dispatch(pad_start, pad_cnt, tail_start, dest_flat, xnp, n_rows):
    t = xnp.shape[0] // X_SUB
    grid_spec = pltpu.PrefetchScalarGridSpec(
        num_scalar_prefetch=3,
        grid=(t // DISP_TT,),
        in_specs=[pl.BlockSpec((DISP_TT * TOP_K,), lambda i, *_: (i,), memory_space=pltpu.SMEM),
                  pl.BlockSpec((DISP_TT * X_SUB, LANES), lambda i, *_: (i, 0))],
        out_specs=pl.BlockSpec(memory_space=pl.ANY),
        scratch_shapes=[pltpu.VMEM((MOE_TM * X_SUB, LANES), U32), pltpu.SemaphoreType.DMA(()),
                        pltpu.SemaphoreType.DMA(())],
    )
    return pl.pallas_call(
        functools.partial(_dispatch_kernel, n_zero=n_rows - t * TOP_K),
        grid_spec=grid_spec,
        out_shape=jax.ShapeDtypeStruct((n_rows * X_SUB, LANES), U32),
        compiler_params=_cparams(("arbitrary",)),
        name="moe_dispatch",
    )(pad_start, pad_cnt, tail_start, dest_flat, xnp)


def _moe_kernel(be_ref, nsb_ref, na_ref, x_ref, wglu_ref, wlin_ref, bglu_ref, blin_ref, wd_ref, bd_ref, o_ref,
                xb, *accs):
    m = pl.program_id(0)
    f = pl.program_id(1)
    nf = pl.num_programs(1)
    n_sub = MOE_TM // MOE_SB

    def ffn(nsb):
        rows = nsb * MOE_SB

        @pl.when(f == 0)
        def _():
            _unpack_bf16_pairs(x_ref, rows, xb)
            for i in range(nsb):
                accs[i][...] = jnp.broadcast_to(bd_ref[0], (MOE_SB, D_MODEL))

        wd = wd_ref[0].astype(BF16)
        for i in range(nsb):
            rs = slice(i * MOE_SB, (i + 1) * MOE_SB)
            glu = jnp.minimum(_dot(xb[rs, :], wglu_ref[0]) + bglu_ref[0], SWIGLU_LIMIT)
            lin = jnp.clip(_dot(xb[rs, :], wlin_ref[0]) + blin_ref[0], -SWIGLU_LIMIT, SWIGLU_LIMIT)
            act = glu * _sigmoid(SWIGLU_ALPHA * glu) * (lin + 1.0)
            accs[i][...] += _dot(act.astype(BF16), wd)

        @pl.when(f == nf - 1)
        def _():
            for i in range(nsb):
                packed = _pack_bf16_pairs(accs[i][...])
                for sub in range(X_SUB):
                    o_ref[pl.ds(i * MOE_SB * X_SUB + sub, MOE_SB, stride=X_SUB), :] = (
                        packed[:, sub * LANES:(sub + 1) * LANES])
            if nsb < n_sub:
                o_ref[rows * X_SUB:, :] = jnp.zeros(((MOE_TM - rows) * X_SUB, LANES), U32)

    active = m < na_ref[0]
    for nsb in range(1, n_sub + 1):
        pl.when(jnp.logical_and(active, nsb_ref[m] == nsb))(functools.partial(ffn, nsb))

    @pl.when(jnp.logical_and(jnp.logical_not(active), f == nf - 1))
    def _():
        o_ref[...] = jnp.zeros_like(o_ref)


def _moe_ffn(block_expert, block_nsub, n_active, x_rows, w_glu, w_lin, b_glu, b_lin, w_down, b_down, n_blocks):
    nf = D_FF // MOE_TF

    def blk(m, na):
        return jnp.maximum(jnp.minimum(m, na[0] - 1), 0)

    def ftile(m, f, na):
        return jnp.where(m < na[0], f, nf - 1)

    def expert(m, be, na):
        return be[blk(m, na)]

    col_tile = lambda m, f, be, ns, na: (expert(m, be, na), 0, ftile(m, f, na))
    grid_spec = pltpu.PrefetchScalarGridSpec(
        num_scalar_prefetch=3,
        grid=(n_blocks, nf),
        in_specs=[pl.BlockSpec((MOE_TM * X_SUB, LANES), lambda m, f, be, ns, na: (blk(m, na), 0)),
                  pl.BlockSpec((1, D_MODEL, MOE_TF), col_tile),
                  pl.BlockSpec((1, D_MODEL, MOE_TF), col_tile),
                  pl.BlockSpec((1, 1, MOE_TF), col_tile),
                  pl.BlockSpec((1, 1, MOE_TF), col_tile),
                  pl.BlockSpec((1, MOE_TF, D_MODEL), lambda m, f, be, ns, na: (expert(m, be, na), ftile(m, f, na), 0)),
                  pl.BlockSpec((1, 1, D_MODEL), lambda m, f, be, ns, na: (expert(m, be, na), 0, 0))],
        out_specs=pl.BlockSpec((MOE_TM * X_SUB, LANES), lambda m, f, be, ns, na: (m, 0)),
        scratch_shapes=[pltpu.VMEM((MOE_TM, D_MODEL), BF16)]
        + [pltpu.VMEM((MOE_SB, D_MODEL), F32)] * (MOE_TM // MOE_SB),
    )
    return pl.pallas_call(
        _moe_kernel,
        grid_spec=grid_spec,
        out_shape=jax.ShapeDtypeStruct((n_blocks * MOE_TM * X_SUB, LANES), U32),
        compiler_params=_cparams(("arbitrary", "arbitrary")),
        name="moe_ffn",
    )(block_expert, block_nsub, n_active, x_rows, w_glu, w_lin, b_glu, b_lin, w_down, b_down)


def _combine_kernel(dest_ref, dest_next_ref, h1_ref, selg_ref, fw_ref, rows_ref, o_ref, buf, h_scr, sem):
    tt = h1_ref.shape[0]
    i = pl.program_id(0)
    n = pl.num_programs(0)
    half = D_MODEL // 2

    def row_copy(slot, t, k, d):
        src = rows_ref.at[pl.ds(pl.multiple_of(d * X_SUB, X_SUB), X_SUB)]
        dst = buf.at[slot, k, pl.ds(pl.multiple_of(t * X_SUB, X_SUB), X_SUB)]
        return pltpu.make_async_copy(src, dst, sem.at[slot])

    def gather(slot, d_ref):
        def issue(t, carry):
            for k in range(TOP_K):
                row_copy(slot, t, k, d_ref[t * TOP_K + k]).start(priority=k % 2)
            return carry

        lax.fori_loop(0, tt, issue, 0, unroll=4)

    slot = i % 2

    @pl.when(i == 0)
    def _():
        gather(0, dest_ref)

    @pl.when(i + 1 < n)
    def _():
        gather(1 - slot, dest_next_ref)

    for k in range(TOP_K):
        pltpu.make_async_copy(rows_ref.at[pl.ds(0, tt * X_SUB)], buf.at[slot, k], sem.at[slot]).wait()

    gates = selg_ref[...]
    for sub in range(X_SUB):
        lo_cs = slice(sub * LANES, (sub + 1) * LANES)
        hi_cs = slice(half + sub * LANES, half + (sub + 1) * LANES)
        h_lo = h1_ref[:, lo_cs]
        h_hi = h1_ref[:, hi_cs]
        for k in range(TOP_K):
            u = buf[slot, k, pl.ds(sub, tt, stride=X_SUB), :]
            g = gates[:, k:k + 1]
            h_lo = h_lo + g * pltpu.bitcast(u << 16, F32)
            h_hi = h_hi + g * pltpu.bitcast(u & jnp.uint32(0xFFFF0000), F32)
        h_scr[:, lo_cs] = h_lo
        h_scr[:, hi_cs] = h_hi
    h = h_scr[...]
    o_ref[...] = h * lax.rsqrt(jnp.mean(h * h, axis=-1, keepdims=True) + EPS) * fw_ref[...]


def _combine(dest_flat, h1, selg, final_w, y_rows):
    t = h1.shape[0]
    tt = COMB_TT
    n = t // tt
    return pl.pallas_call(
        _combine_kernel,
        grid=(n,),
        in_specs=[pl.BlockSpec((tt * TOP_K,), lambda i: (i,), memory_space=pltpu.SMEM),
                  pl.BlockSpec((tt * TOP_K,), lambda i: (jnp.minimum(i + 1, n - 1),), memory_space=pltpu.SMEM),
                  pl.BlockSpec((tt, D_MODEL), lambda i: (i, 0)),
                  pl.BlockSpec((tt, LANES), lambda i: (i, 0)),
                  pl.BlockSpec((1, D_MODEL), lambda i: (0, 0)),
                  pl.BlockSpec(memory_space=pl.ANY)],
        out_specs=pl.BlockSpec((tt, D_MODEL), lambda i: (i, 0)),
        out_shape=jax.ShapeDtypeStruct((t, D_MODEL), F32),
        scratch_shapes=[pltpu.VMEM((2, TOP_K, tt * X_SUB, LANES), U32), pltpu.VMEM((tt, D_MODEL), F32),
                        pltpu.SemaphoreType.DMA((2,))],
        compiler_params=_cparams(("arbitrary",)),
        name="moe_combine",
    )(dest_flat, dest_flat, h1, selg, final_w, y_rows)


def kernel(x, norm1_w, w_in, conv_w, conv_b, dt_bias, a_log, d_skip, ssd_norm_w, i_bias, f_bias, ml_norm_w,
           w_out, norm2_w, router_w, router_b, w_gate_up, b_gate_up, w_down, b_down, final_norm_w):
    b, s, d = x.shape
    t = b * s
    assert w_in.shape[0] == 1, "single-layer problem"
    h = x.reshape(t, d)
    for l in range(1):
        wi = w_in[l]
        o_z, o_xbc, o_dt = 0, D_SSD, D_SSD + D_SSD + 2 * SSD_GROUPS * SSD_STATE
        o_q = o_dt + SSD_HEADS
        o_k = o_q + ML_HEADS * ML_DQK
        o_v = o_k + ML_HEADS * ML_DQK
        o_o = o_v + D_ML
        o_i = o_o + D_ML
        o_f = o_i + ML_HEADS
        w_tail = wi[:, o_q:o_i].astype(BF16)
        w_small = jnp.concatenate(
            [wi[:, o_dt:o_q], wi[:, o_i:o_f], wi[:, o_f:o_f + ML_HEADS],
             jnp.zeros((d, LANES - SSD_HEADS - 2 * ML_HEADS), F32)], axis=1).astype(BF16)

        proj, small = _input_projection(h, norm1_w[l].reshape(1, d), wi, w_tail, w_small, o_dt)
        proj3 = proj.reshape(b, s, D_BIG)
        small3 = small.reshape(b, s, LANES)
        y_ssd = _ssd_mixer(proj3, small3, conv_w[l], conv_b[l], dt_bias[l], a_log[l], d_skip[l], ssd_norm_w[l], b, s)
        y_ml, w_glu, w_lin = _mlstm_mixer(proj3, small3, i_bias[l], f_bias[l], ml_norm_w[l], b, s, w_gate_up[l])

        rw = jnp.pad(router_w[l], ((0, 0), (0, LANES - N_EXPERTS)))
        rw_hi = rw.astype(BF16)
        rw = jnp.concatenate([rw_hi, (rw - rw_hi.astype(F32)).astype(BF16)], axis=1)
        rb = jnp.pad(router_b[l], (0, LANES - N_EXPERTS)).reshape(1, LANES)
        h1, xnp, seli, selg, cnt = _outproj_router(
            h, y_ssd.reshape(t, D_SSD), y_ml.reshape(t, D_ML), w_out[l].astype(BF16), norm2_w[l].reshape(1, d), rw, rb)

        n_blocks = (t * TOP_K) // MOE_TM + N_EXPERTS
        dest, table = _route_offsets(seli, cnt, n_blocks)
        dest_flat = dest[:, :TOP_K].reshape(-1)
        block_expert = table[:n_blocks, 0]
        n_active = table[0:1, 1]
        block_nsub = table[:n_blocks, 2]

        x_rows = _dispatch(table[:N_EXPERTS, 3], table[:N_EXPERTS, 4], table[0:1, 5], dest_flat, xnp,
                           n_blocks * MOE_TM)
        b_glu = b_gate_up[l][:, 0::2].reshape(N_EXPERTS, 1, D_FF)
        b_lin = b_gate_up[l][:, 1::2].reshape(N_EXPERTS, 1, D_FF)
        y_rows = _moe_ffn(block_expert, block_nsub, n_active, x_rows, w_glu, w_lin, b_glu, b_lin, w_down[l],
                          b_down[l].reshape(N_EXPERTS, 1, d), n_blocks)
        h = _combine(dest_flat, h1, selg, final_norm_w.reshape(1, d), y_rows)
    return h.reshape(b, s, d)
```

```python
import functools

import jax
import jax.numpy as jnp
from jax import lax
from jax.experimental import pallas as pl
from jax.experimental.pallas import tpu as pltpu

F32 = jnp.float32
BF16 = jnp.bfloat16
I32 = jnp.int32
U32 = jnp.uint32

LANES = 128
SUBLANES = 8
VMEM_LIMIT = 56 * 1024 * 1024

D_MODEL = 2048
EPS = 1e-5
SSD_HEAD_DIM = 64
SSD_HEADS = 32
D_SSD = 2048
SSD_GROUPS = 4
SSD_STATE = 128
SSD_CONV = 4
CHUNK = 128
ML_HEADS = 8
ML_DV = 256
ML_DQK = 128
D_ML = 2048
N_EXPERTS = 32
TOP_K = 4
D_FF = 2048
SWIGLU_ALPHA = 1.702
SWIGLU_LIMIT = 7.0

COL_Z, COL_X, COL_BC, COL_Q, COL_K, COL_V, COL_O = 0, 2048, 4096, 5120, 6144, 7168, 9216
COL_W = 1024
D_BIG = 11264
LANE_DT, LANE_I, LANE_F = 0, 32, 40

IN_TM, IN_TN = 1024, 1024
OUT_TM = 512
ROUTE_TM = 2048
DISP_TT = 512
MOE_TM = 512
MOE_SB = 256
MOE_TF = 1024
PERM_W = 256
PREP_TN = 1024
COMB_TT = 256
X_SUB = D_MODEL // 2 // LANES


def _cparams(sem, vmem=VMEM_LIMIT):
    return pltpu.CompilerParams(dimension_semantics=sem, vmem_limit_bytes=vmem)


def _dot(a, b):
    return jnp.dot(a, b, preferred_element_type=F32)


def _dot_exact(a, b):
    return jnp.dot(a, b, preferred_element_type=F32, precision=lax.Precision.HIGHEST)


def _split2_dot(x, e2_bf16):
    hi = x.astype(BF16)
    lo = (x - hi.astype(F32)).astype(BF16)
    return _dot(jnp.concatenate([hi, lo], axis=1), e2_bf16)


def _sigmoid(x):
    return 1.0 / (1.0 + jnp.exp(-x))


def _softplus(x):
    return jnp.maximum(x, 0.0) + jnp.log1p(jnp.exp(-jnp.abs(x)))


def _tril(n, strict=False):
    r = lax.broadcasted_iota(I32, (n, n), 0)
    c = lax.broadcasted_iota(I32, (n, n), 1)
    return (r > c) if strict else (r >= c)


def _pack_bf16_pairs(y):
    half = D_MODEL // 2
    lo = pltpu.bitcast(y[:, :half].astype(BF16).astype(F32), U32)
    hi = pltpu.bitcast(y[:, half:].astype(BF16).astype(F32), U32)
    return hi | (lo >> 16)


def _unpack_bf16_pairs(src_ref, rows, dst_ref):
    half = D_MODEL // 2
    for sub in range(X_SUB):
        u = src_ref[pl.ds(sub, rows, stride=X_SUB), :]
        dst_ref[0:rows, sub * LANES:(sub + 1) * LANES] = pltpu.bitcast(u << 16, F32).astype(BF16)
        dst_ref[0:rows, half + sub * LANES:half + (sub + 1) * LANES] = (
            pltpu.bitcast(u & jnp.uint32(0xFFFF0000), F32).astype(BF16))


def _deinterleave_tile(w_ref, p_ref, glu_ref, lin_ref):
    hw = PERM_W // 2
    for s in range(w_ref.shape[2] // PERM_W):
        t = _dot(w_ref[0, :, s * PERM_W:(s + 1) * PERM_W].astype(BF16), p_ref[...])
        glu_ref[0, :, s * hw:(s + 1) * hw] = t[:, :hw].astype(BF16)
        lin_ref[0, :, s * hw:(s + 1) * hw] = t[:, hw:].astype(BF16)


def _deinterleave_perm():
    src = jnp.arange(PERM_W)
    dst = jnp.where(src % 2 == 0, src // 2, PERM_W // 2 + src // 2)
    return (dst[:, None] == jnp.arange(PERM_W)[None, :]).astype(BF16)


def _deinterleave_specs(w_gate_up, tile_index):
    e, k, n = w_gate_up.shape

    def at(*ids):
        ei, ci = tile_index(*ids)
        return ei, 0, ci

    in_specs = [pl.BlockSpec((1, k, PREP_TN), at), pl.BlockSpec((PERM_W, PERM_W), lambda *ids: (0, 0))]
    out_specs = [pl.BlockSpec((1, k, PREP_TN // 2), at)] * 2
    out_shape = [jax.ShapeDtypeStruct((e, k, n // 2), BF16)] * 2
    return in_specs, out_specs, out_shape


def _deinterleave_gate_up(w_gate_up):
    e, _, n = w_gate_up.shape
    in_specs, out_specs, out_shape = _deinterleave_specs(w_gate_up, lambda i, j: (i, j))
    return pl.pallas_call(
        _deinterleave_tile,
        grid=(e, n // PREP_TN),
        in_specs=in_specs,
        out_specs=out_specs,
        out_shape=out_shape,
        compiler_params=_cparams(("arbitrary", "arbitrary")),
        name="moe_deinterleave",
    )(w_gate_up, _deinterleave_perm())


def _regroup_kernel(a_ref, b_ref, o_ref, *, head_tiles, shift):
    j = pl.program_id(0)
    tn = o_ref.shape[1]

    @pl.when(j < head_tiles)
    def _():
        o_ref[...] = a_ref[...].astype(BF16)

    @pl.when(j >= head_tiles)
    def _():
        col = lax.broadcasted_iota(I32, a_ref.shape, 1)
        a = pltpu.roll(a_ref[...], tn - shift, 1)
        b = pltpu.roll(b_ref[...], tn - shift, 1)
        o_ref[...] = jnp.where(col < tn - shift, a, b).astype(BF16)


def _regroup_in_weights(w_in3, head_cols, skip_cols):
    _, d, n = w_in3.shape
    tn = IN_TN
    head_tiles = head_cols // tn
    last_window = (n - 1) // tn
    return pl.pallas_call(
        functools.partial(_regroup_kernel, head_tiles=head_tiles, shift=skip_cols),
        grid=(D_BIG // tn,),
        in_specs=[pl.BlockSpec((None, d, tn), lambda j: (0, 0, j)),
                  pl.BlockSpec((None, d, tn),
                               lambda j: (0, 0, jnp.where(j >= head_tiles, jnp.minimum(j + 1, last_window), head_tiles)))],
        out_specs=pl.BlockSpec((d, tn), lambda j: (0, j)),
        out_shape=jax.ShapeDtypeStruct((d, D_BIG), BF16),
        compiler_params=_cparams(("arbitrary",)),
        name="regroup_w_in",
    )(w_in3, w_in3)


def _inproj_kernel(x_ref, nw_ref, w_ref, ws_ref, o_ref, os_ref, n_scr):
    @pl.when(pl.program_id(1) == 0)
    def _():
        x = x_ref[...]
        n = x * lax.rsqrt(jnp.mean(x * x, axis=-1, keepdims=True) + EPS) * nw_ref[...]
        nb = n.astype(BF16)
        n_scr[...] = nb
        os_ref[...] = _dot(nb, ws_ref[...])

    o_ref[...] = _dot(n_scr[...], w_ref[...]).astype(BF16)


def _input_projection(xf, norm_w, w_big, w_small):
    t = xf.shape[0]
    return pl.pallas_call(
        _inproj_kernel,
        grid=(t // IN_TM, D_BIG // IN_TN),
        in_specs=[pl.BlockSpec((IN_TM, D_MODEL), lambda i, j: (i, 0)),
                  pl.BlockSpec((1, D_MODEL), lambda i, j: (0, 0)),
                  pl.BlockSpec((D_MODEL, IN_TN), lambda i, j: (0, j)),
                  pl.BlockSpec((D_MODEL, LANES), lambda i, j: (0, 0))],
        out_specs=[pl.BlockSpec((IN_TM, IN_TN), lambda i, j: (i, j)),
                   pl.BlockSpec((IN_TM, LANES), lambda i, j: (i, 0))],
        out_shape=[jax.ShapeDtypeStruct((t, D_BIG), BF16),
                   jax.ShapeDtypeStruct((t, LANES), F32)],
        scratch_shapes=[pltpu.VMEM((IN_TM, D_MODEL), BF16)],
        compiler_params=_cparams(("arbitrary", "arbitrary")),
        name="inproj",
    )(xf, norm_w, w_big, w_small)


def _ssd_kernel(z_ref, x_ref, bc_ref, sm_ref, cw_ref, cb_ref, shift_ref, dtb_ref, alog_ref,
                dskip_ref, nw_ref, e_ref, o_ref, ext, y_scr, *states):
    q = CHUNK

    @pl.when(pl.program_id(1) == 0)
    def _():
        ext[0:q, :] = jnp.zeros((q, ext.shape[1]), BF16)
        for state in states:
            state[...] = jnp.zeros_like(state)

    ext[q:2 * q, 0:D_SSD] = x_ref[0]
    ext[q:2 * q, D_SSD:] = bc_ref[0]
    shifted = _dot(shift_ref[...], ext[...])
    acc = cb_ref[...] + ext[q:2 * q, :].astype(F32) * cw_ref[SSD_CONV - 1:SSD_CONV, :]
    for j in range(SSD_CONV - 1):
        acc = acc + shifted[j * q:(j + 1) * q, :] * cw_ref[j:j + 1, :]
    xbc = acc * _sigmoid(acc)
    xs = xbc[:, :D_SSD]
    bcs = xbc[:, D_SSD:]
    ext[0:q, :] = ext[q:2 * q, :]

    lane = lax.broadcasted_iota(I32, (1, LANES), 1)
    g = sm_ref[0]
    dt = _softplus(g + dtb_ref[...])
    a = jnp.where(lane < SSD_HEADS, -jnp.exp(alog_ref[...]), 0.0)
    da = dt * a
    tri = _tril(q).astype(F32)
    a_cum = _dot_exact(tri, da)
    a_cum_t = a_cum.T
    a_end = a_cum[q - 1:q, :]
    stack = jnp.concatenate(
        [dt, jnp.exp(a_end - a_cum), jnp.exp(a_cum), jnp.broadcast_to(jnp.exp(a_end), (SUBLANES, LANES))], axis=0)
    ex = _split2_dot(stack, e_ref[...])
    dt_x = ex[0:q]
    to_end_x = ex[q:2 * q]
    from_start_x = ex[2 * q:3 * q]
    chunk_decay_x = ex[3 * q:3 * q + 1]

    xdt = xs * dt_x
    xdt_b = xdt.astype(BF16)
    xde_b = (xdt * to_end_x).astype(BF16)
    causal = _tril(q)
    lane_q = lax.broadcasted_iota(I32, (1, LANES), 1)
    gw = SSD_STATE
    hpg = SSD_HEADS // SSD_GROUPS
    cpg = D_SSD // SSD_GROUPS
    for gi in range(SSD_GROUPS):
        bg = bcs[:, gi * gw:(gi + 1) * gw]
        cg_b = bcs[:, SSD_GROUPS * gw + gi * gw:SSD_GROUPS * gw + (gi + 1) * gw].astype(BF16)
        cb = lax.dot_general(cg_b, bg.astype(BF16), (((1,), (1,)), ((), ())), preferred_element_type=F32)
        bg_t = bg.T.astype(BF16)
        cs = slice(gi * cpg, (gi + 1) * cpg)
        prev = states[gi][...]
        y_off = _dot(cg_b, prev.astype(BF16)) * from_start_x[:, cs]
        states[gi][...] = prev * chunk_decay_x[:, cs] + _dot(bg_t, xde_b[:, cs])
        for pr in range(hpg // 2):
            c0 = gi * cpg + pr * LANES
            xp = xdt_b[:, c0:c0 + LANES]
            acc = y_off[:, pr * LANES:(pr + 1) * LANES]
            for sub in range(2):
                h = gi * hpg + pr * 2 + sub
                seg = a_cum[:, h:h + 1] - a_cum_t[h:h + 1, :]
                decay = jnp.exp(jnp.where(causal, seg, -jnp.inf))
                m = (cb * decay).astype(BF16)
                keep = (lane_q < SSD_HEAD_DIM) if sub == 0 else (lane_q >= SSD_HEAD_DIM)
                acc = acc + _dot(m, jnp.where(keep, xp, jnp.zeros_like(xp)))
            y_scr[:, c0:c0 + LANES] = acc

    y = y_scr[...] + xs * dskip_ref[...]
    zf = z_ref[0].astype(F32)
    y = y * (zf * _sigmoid(zf))
    for gi in range(SSD_GROUPS):
        cs = slice(gi * cpg, (gi + 1) * cpg)
        yg = y[:, cs]
        yg = yg * lax.rsqrt(jnp.mean(yg * yg, axis=-1, keepdims=True) + EPS)
        o_ref[0, :, cs] = (yg * nw_ref[:, cs]).astype(BF16)


def _ssd_mixer(proj, small, conv_w, conv_b, dt_bias, a_log, d_skip, norm_w, b, s):
    nc = s // CHUNK
    pad = LANES - SSD_HEADS
    dtb = jnp.pad(dt_bias, (0, pad)).reshape(1, LANES)
    alog = jnp.pad(a_log, (0, pad)).reshape(1, LANES)
    dskip = jnp.repeat(d_skip, SSD_HEAD_DIM).reshape(1, D_SSD)
    expand = (jnp.arange(LANES)[:, None] == (jnp.arange(D_SSD) // SSD_HEAD_DIM)[None, :]).astype(BF16)
    expand = jnp.concatenate([expand, expand], axis=0)
    d_bc = 2 * SSD_GROUPS * SSD_STATE
    d_conv = D_SSD + d_bc
    tt = jnp.arange((SSD_CONV - 1) * CHUNK)
    back = (SSD_CONV - 1) - tt // CHUNK
    shift = ((CHUNK + tt % CHUNK - back)[:, None] == jnp.arange(2 * CHUNK)[None, :]).astype(BF16)
    const = lambda shape: pl.BlockSpec(shape, lambda i, j: (0,) * len(shape))
    col = lambda width, off: pl.BlockSpec((1, CHUNK, width), lambda i, j: (i, j, off // width))
    return pl.pallas_call(
        _ssd_kernel,
        grid=(b, nc),
        in_specs=[col(D_SSD, COL_Z), col(D_SSD, COL_X), col(d_bc, COL_BC), col(LANES, 0),
                  const((SSD_CONV, d_conv)), const((1, d_conv)), const(((SSD_CONV - 1) * CHUNK, 2 * CHUNK)),
                  const((1, LANES)), const((1, LANES)), const((1, D_SSD)), const((1, D_SSD)),
                  const((2 * LANES, D_SSD))],
        out_specs=pl.BlockSpec((1, CHUNK, D_SSD), lambda i, j: (i, j, 0)),
        out_shape=jax.ShapeDtypeStruct((b, s, D_SSD), BF16),
        scratch_shapes=[pltpu.VMEM((2 * CHUNK, d_conv), BF16),
                        pltpu.VMEM((CHUNK, D_SSD), F32)]
        + [pltpu.VMEM((SSD_STATE, D_SSD // SSD_GROUPS), F32)] * SSD_GROUPS,
        compiler_params=_cparams(("arbitrary", "arbitrary")),
        name="ssd_mixer",
    )(proj, proj, proj, small, conv_w, conv_b.reshape(1, -1), shift, dtb, alog, dskip, norm_w.reshape(1, -1), expand)


def _mlstm_kernel(q_ref, k_ref, v0_ref, v1_ref, o0_ref, o1_ref, sm_ref, bias_ref, nw_ref, *rest, prep_gate_up):
    L = CHUNK
    heads_per_ref = COL_W // ML_DV

    def head_cols(refs, h):
        c0 = (h % heads_per_ref) * ML_DV
        return refs[h // heads_per_ref][0, :, c0:c0 + ML_DV]

    if prep_gate_up:
        wg_ref, perm_ref, out_ref, glu_ref, lin_ref, n_scr, m_scr, *ct_scrs = rest
        _deinterleave_tile(wg_ref, perm_ref, glu_ref, lin_ref)
    else:
        out_ref, n_scr, m_scr, *ct_scrs = rest

    @pl.when(pl.program_id(1) == 0)
    def _():
        for ct_scr in ct_scrs:
            ct_scr[...] = jnp.zeros_like(ct_scr)
        n_scr[...] = jnp.zeros_like(n_scr)
        m_scr[...] = jnp.zeros_like(m_scr)

    n_all = n_scr[...]
    m_all = m_scr[...]
    n_new_all = n_all
    m_new_all = m_all
    head_row = lax.broadcasted_iota(I32, (ML_HEADS, LANES), 0)

    g = sm_ref[0] + bias_ref[...]
    log_f = jnp.minimum(g, 0.0) - jnp.log1p(jnp.exp(-jnp.abs(g)))
    tri = _tril(L).astype(F32)
    bcum = _dot_exact(tri, log_f)
    bcum_t = bcum.T
    keys_before = jnp.logical_not(_tril(L, strict=True))
    row16 = lax.broadcasted_iota(I32, (2 * SUBLANES, ML_DQK), 0)
    scale = ML_DQK ** -0.5
    nt = (((1,), (1,)), ((), ()))
    tn = (((0,), (0,)), ((), ()))
    for h in range(ML_HEADS):
        li, lf = LANE_I + h, LANE_F + h
        li_col, b_col, b_row = g[:, li:li + 1], bcum[:, lf:lf + 1], bcum_t[lf:lf + 1, :]
        btot = bcum[L - 1:L, lf:lf + 1]
        m_prev = m_all[h:h + 1, 0:1]
        ct_prev = ct_scrs[h][...]
        n_prev = n_all[h:h + 1, :]
        qb = q_ref[0, :, h * ML_DQK:(h + 1) * ML_DQK]
        kb = k_ref[0, :, h * ML_DQK:(h + 1) * ML_DQK]
        vb = head_cols((v0_ref, v1_ref), h)
        w_end = btot - b_col + li_col
        m_loc = jnp.max(w_end, axis=0, keepdims=True)
        k_end = kb.astype(F32) * jnp.exp(w_end - m_loc)
        kv_t = lax.dot_general(vb, k_end.astype(BF16), tn, preferred_element_type=F32)
        nk = jnp.sum(k_end, axis=0, keepdims=True)
        d_log = jnp.where(keys_before, (li_col - b_col) + b_row, -jnp.inf)
        m_inter = b_row + m_prev
        m_t = jnp.maximum(m_inter, jnp.max(d_log, axis=0, keepdims=True))
        s_t = lax.dot_general(kb, qb, nt, preferred_element_type=F32)
        s_t = s_t * (scale * jnp.exp(d_log - m_t))
        inter = scale * jnp.exp(m_inter - m_t)
        num_t = (lax.dot_general(vb, s_t.astype(BF16), tn, preferred_element_type=F32)
                 + lax.dot_general(ct_prev.astype(BF16), qb, nt, preferred_element_type=F32) * inter)
        n_hi = n_prev.astype(BF16).astype(F32)
        n_rows = jnp.where(row16 == 0, n_hi, jnp.where(row16 == 1, n_prev - n_hi, 0.0)).astype(BF16)
        qn = lax.dot_general(n_rows, qb, nt, preferred_element_type=F32)
        den = jnp.sum(s_t, axis=0, keepdims=True) + (qn[0:1, :] + qn[1:2, :]) * inter
        den = jnp.maximum(jnp.abs(den), jnp.exp(-m_t))
        ht_t = num_t / den
        ht_t = ht_t * lax.rsqrt(jnp.mean(ht_t * ht_t, axis=0, keepdims=True) + EPS)
        vs = slice(h * ML_DV, (h + 1) * ML_DV)
        og = _sigmoid(head_cols((o0_ref, o1_ref), h).astype(F32))
        out_ref[0, :, vs] = (ht_t.T * nw_ref[:, vs] * og).astype(BF16)
        m_new = jnp.maximum(btot + m_prev, m_loc)
        fa = jnp.exp(btot + m_prev - m_new)
        fg = jnp.exp(m_loc - m_new)
        ct_scrs[h][...] = fa * ct_prev + fg * kv_t
        n_new_all = jnp.where(head_row == h, fa * n_prev + fg * nk, n_new_all)
        m_new_all = jnp.where(head_row == h, m_new, m_new_all)
    n_scr[...] = n_new_all
    m_scr[...] = m_new_all


def _mlstm_mixer(proj, small, i_bias, f_bias, norm_w, b, s, w_gate_up):
    nc = s // CHUNK
    bias = jnp.zeros((LANES,), F32).at[LANE_I:LANE_I + ML_HEADS].set(i_bias).at[LANE_F:LANE_F + ML_HEADS].set(f_bias)
    d_qk = ML_HEADS * ML_DQK
    const = lambda shape: pl.BlockSpec(shape, lambda i, j: (0,) * len(shape))
    col = lambda width, off: pl.BlockSpec((1, CHUNK, width), lambda i, j: (i, j, off // width))
    e, _, n = w_gate_up.shape
    tiles_per_expert = n // PREP_TN
    prep = b * nc == e * tiles_per_expert
    in_specs = [col(d_qk, COL_Q), col(d_qk, COL_K), col(COL_W, COL_V), col(COL_W, COL_V + COL_W),
                col(COL_W, COL_O), col(COL_W, COL_O + COL_W), col(LANES, 0), const((1, LANES)), const((1, D_ML))]
    out_specs = [pl.BlockSpec((1, CHUNK, D_ML), lambda i, j: (i, j, 0))]
    out_shape = [jax.ShapeDtypeStruct((b, s, D_ML), BF16)]
    args = [proj] * 6 + [small, bias.reshape(1, LANES), norm_w.reshape(1, -1)]
    if prep:
        w_in_specs, w_out_specs, w_out_shape = _deinterleave_specs(
            w_gate_up, lambda i, j: ((i * nc + j) // tiles_per_expert, (i * nc + j) % tiles_per_expert))
        in_specs += w_in_specs
        out_specs += w_out_specs
        out_shape += w_out_shape
        args += [w_gate_up, _deinterleave_perm()]
    outs = pl.pallas_call(
        functools.partial(_mlstm_kernel, prep_gate_up=prep),
        grid=(b, nc),
        in_specs=in_specs,
        out_specs=out_specs,
        out_shape=out_shape,
        scratch_shapes=[pltpu.VMEM((ML_HEADS, ML_DQK), F32), pltpu.VMEM((ML_HEADS, LANES), F32)]
        + [pltpu.VMEM((ML_DV, ML_DQK), F32)] * ML_HEADS,
        compiler_params=_cparams(("arbitrary", "arbitrary")),
        name="mlstm_mixer",
    )(*args)
    if prep:
        return outs[0], outs[1], outs[2]
    return (outs[0], *_deinterleave_gate_up(w_gate_up))


def _outproj_router_kernel(x_ref, ys_ref, ym_ref, wo_ref, n2w_ref, rw_ref, rb_ref,
                           h1_ref, xnp_ref, seli_ref, selg_ref, cnt_ref, carry):
    tm = x_ref.shape[0]

    @pl.when(pl.program_id(0) == 0)
    def _():
        carry[...] = jnp.zeros_like(carry)

    h1 = x_ref[...] + _dot(ys_ref[...], wo_ref[0:D_SSD, :]) + _dot(ym_ref[...], wo_ref[D_SSD:, :])
    h1_ref[...] = h1
    xn = h1 * lax.rsqrt(jnp.mean(h1 * h1, axis=-1, keepdims=True) + EPS) * n2w_ref[...]
    packed = _pack_bf16_pairs(xn)
    for sub in range(X_SUB):
        xnp_ref[pl.ds(sub, tm, stride=X_SUB), :] = packed[:, sub * LANES:(sub + 1) * LANES]

    lane = lax.broadcasted_iota(I32, (tm, LANES), 1)
    lane_f = lane.astype(F32)
    xn_hi = xn.astype(BF16)
    xn_lo = (xn - xn_hi.astype(F32)).astype(BF16)
    hh_hl = _dot(xn_hi, rw_ref[...])
    logits = hh_hl[:, :LANES] + hh_hl[:, LANES:] + _dot(xn_lo, rw_ref[:, :LANES]) + rb_ref[...]
    work = jnp.where(lane < N_EXPERTS, logits, -jnp.inf)
    vals, idxs = [], []
    chosen = jnp.zeros((tm, LANES), jnp.bool_)
    for _ in range(TOP_K):
        m = jnp.max(work, axis=1, keepdims=True)
        idx = jnp.min(jnp.where(work == m, lane_f, float(LANES)), axis=1, keepdims=True)
        sel = lane_f == idx
        vals.append(m)
        idxs.append(idx)
        chosen = jnp.logical_or(chosen, sel)
        work = jnp.where(sel, -jnp.inf, work)
    exps = [jnp.exp(v - vals[0]) for v in vals]
    denom = exps[0] + exps[1] + exps[2] + exps[3]

    onehot = jnp.where(chosen, 1.0, 0.0)
    ranks = _dot(_tril(tm, strict=True).astype(BF16), onehot.astype(BF16)) + carry[...]
    total = carry[...] + jnp.sum(onehot, axis=0, keepdims=True)
    carry[...] = total
    cnt_ref[...] = jnp.broadcast_to(total, cnt_ref.shape)

    seli = jnp.zeros((tm, LANES), F32)
    selg = jnp.zeros((tm, LANES), F32)
    for k in range(TOP_K):
        rank_k = jnp.sum(jnp.where(lane_f == idxs[k], ranks, 0.0), axis=1, keepdims=True)
        seli = jnp.where(lane == k, idxs[k], seli)
        seli = jnp.where(lane == TOP_K + k, rank_k, seli)
        selg = jnp.where(lane == k, exps[k] / denom, selg)
    seli_ref[...] = seli.astype(I32)
    selg_ref[...] = selg


def _outproj_router(xf, y_ssd, y_ml, w_out_b, norm2_w, router_w_pad, router_b_pad):
    t = xf.shape[0]
    tm = OUT_TM
    row = lambda width: pl.BlockSpec((tm, width), lambda i: (i, 0))
    const = lambda shape: pl.BlockSpec(shape, lambda i: (0,) * len(shape))
    return pl.pallas_call(
        _outproj_router_kernel,
        grid=(t // tm,),
        in_specs=[row(D_MODEL), row(D_SSD), row(D_ML),
                  pl.BlockSpec((D_SSD + D_ML, D_MODEL), lambda i: (0, 0), pipeline_mode=pl.Buffered(1)),
                  const((1, D_MODEL)),
                  const((D_MODEL, 2 * LANES)), const((1, LANES))],
        out_specs=[row(D_MODEL), pl.BlockSpec((tm * X_SUB, LANES), lambda i: (i, 0)), row(LANES), row(LANES),
                   const((SUBLANES, LANES))],
        out_shape=[jax.ShapeDtypeStruct((t, D_MODEL), F32),
                   jax.ShapeDtypeStruct((t * X_SUB, LANES), U32),
                   jax.ShapeDtypeStruct((t, LANES), I32),
                   jax.ShapeDtypeStruct((t, LANES), F32),
                   jax.ShapeDtypeStruct((SUBLANES, LANES), F32)],
        scratch_shapes=[pltpu.VMEM((1, LANES), F32)],
        compiler_params=_cparams(("arbitrary",)),
        name="outproj_router",
    )(xf, y_ssd, y_ml, w_out_b, norm2_w, router_w_pad, router_b_pad)


def _route_kernel(seli_ref, cnt_ref, dest_ref, be_ref):
    tm = seli_ref.shape[0]
    nbp = be_ref.shape[0]
    lane1 = lax.broadcasted_iota(I32, (1, LANES), 1)
    cnt = jnp.where(lane1 < N_EXPERTS, cnt_ref[0:1, :], 0.0)
    padded = jnp.ceil(cnt / MOE_TM) * MOE_TM
    r = lax.broadcasted_iota(I32, (LANES, LANES), 0)
    c = lax.broadcasted_iota(I32, (LANES, LANES), 1)
    upper = jnp.where(r <= c, 1.0, 0.0)
    pend = jnp.round(_dot_exact(jnp.broadcast_to(padded, (SUBLANES, LANES)), upper))[0:1, :]
    pstart = pend - padded

    lane = lax.broadcasted_iota(I32, (tm, LANES), 1)
    lane_f = lane.astype(F32)
    seli = seli_ref[...].astype(F32)
    dest = jnp.zeros((tm, LANES), F32)
    for k in range(TOP_K):
        idx_k = jnp.sum(jnp.where(lane == k, seli, 0.0), axis=1, keepdims=True)
        rank_k = jnp.sum(jnp.where(lane == TOP_K + k, seli, 0.0), axis=1, keepdims=True)
        start_k = jnp.sum(jnp.where(lane_f == idx_k, pstart, 0.0), axis=1, keepdims=True)
        dest = jnp.where(lane == k, start_k + rank_k, dest)
    dest_ref[...] = dest.astype(I32)

    blk_start = (lax.broadcasted_iota(I32, (nbp, LANES), 0) * MOE_TM).astype(F32)
    lane_b = lax.broadcasted_iota(I32, (nbp, LANES), 1)
    passed = jnp.where(jnp.logical_and(lane_b < N_EXPERTS, pend <= blk_start), 1.0, 0.0)
    expert = jnp.minimum(jnp.sum(passed, axis=1, keepdims=True), float(N_EXPERTS - 1))
    n_active = pend[:, N_EXPERTS - 1:N_EXPERTS] / MOE_TM
    tok_end = jnp.sum(jnp.where(lane_b.astype(F32) == expert, pstart + cnt, 0.0), axis=1, keepdims=True)
    valid = jnp.clip(tok_end - blk_start[:, 0:1], 0.0, float(MOE_TM))
    n_sub = jnp.ceil(valid / MOE_SB)
    own = lane_b == lax.broadcasted_iota(I32, (nbp, LANES), 0)
    pad_start = jnp.sum(jnp.where(own, pstart + cnt, 0.0), axis=1, keepdims=True)
    pad_cnt = jnp.sum(jnp.where(own, padded - cnt, 0.0), axis=1, keepdims=True)
    cols = (expert, n_active, n_sub, pad_start, pad_cnt, pend[:, N_EXPERTS - 1:N_EXPERTS])
    be = jnp.zeros((nbp, LANES), F32)
    for j, v in enumerate(cols):
        be = jnp.where(lane_b == j, v, be)
    be_ref[...] = be.astype(I32)


def _route_offsets(seli, cnt, n_blocks):
    t = seli.shape[0]
    nbp = -(-n_blocks // SUBLANES) * SUBLANES
    return pl.pallas_call(
        _route_kernel,
        grid=(t // ROUTE_TM,),
        in_specs=[pl.BlockSpec((ROUTE_TM, LANES), lambda i: (i, 0)),
                  pl.BlockSpec((SUBLANES, LANES), lambda i: (0, 0))],
        out_specs=[pl.BlockSpec((ROUTE_TM, LANES), lambda i: (i, 0)),
                   pl.BlockSpec((nbp, LANES), lambda i: (0, 0))],
        out_shape=[jax.ShapeDtypeStruct((t, LANES), I32),
                   jax.ShapeDtypeStruct((nbp, LANES), I32)],
        compiler_params=_cparams(("arbitrary",)),
        name="route_offsets",
    )(seli, cnt)


def _dispatch_kernel(pad_start_ref, pad_cnt_ref, tail_ref, dest_ref, xnp_ref, rows_ref, zbuf, sem, zsem, *, n_zero):
    tt = xnp_ref.shape[0] // X_SUB
    n_rows = rows_ref.shape[0] // X_SUB
    first = pl.program_id(0) == 0

    def zero_rows(start_row, n):
        dst = rows_ref.at[pl.ds(pl.multiple_of(start_row * X_SUB, X_SUB), n * X_SUB)]
        return pltpu.make_async_copy(zbuf.at[pl.ds(0, n * X_SUB)], dst, zsem)

    @pl.when(first)
    def _():
        zbuf[...] = jnp.zeros_like(zbuf)

        def expert_padding(e, carry):
            start, count = pad_start_ref[e], pad_cnt_ref[e]
            for k in reversed(range(MOE_TM.bit_length() - 1)):
                run = 1 << k
                pl.when((count & run) != 0)(
                    lambda run=run: zero_rows(start + (count & ~(2 * run - 1)), run).start())
            return carry

        lax.fori_loop(0, N_EXPERTS, expert_padding, 0)

        def trailing_block(j, carry):
            start = tail_ref[0] + j * MOE_TM
            pl.when(start < n_rows)(lambda: zero_rows(start, MOE_TM).start())
            return carry

        lax.fori_loop(0, N_EXPERTS, trailing_block, 0)

    def tile_copy(t, d):
        src = xnp_ref.at[pl.ds(pl.multiple_of(t * X_SUB, X_SUB), X_SUB)]
        dst = rows_ref.at[pl.ds(pl.multiple_of(d * X_SUB, X_SUB), X_SUB)]
        return pltpu.make_async_copy(src, dst, sem)

    def issue(t, carry):
        for k in range(TOP_K):
            tile_copy(t, dest_ref[t * TOP_K + k]).start(priority=k % 2)
        return carry

    lax.fori_loop(0, tt, issue, 0, unroll=4)
    for _ in range(TOP_K):
        pltpu.make_async_copy(xnp_ref, rows_ref.at[pl.ds(0, tt * X_SUB)], sem).wait()

    @pl.when(first)
    def _():
        span = rows_ref.at[pl.ds(0, n_zero * X_SUB)]
        pltpu.make_async_copy(span, span, zsem).wait()


def _dispatch(pad_start, pad_cnt, tail_start, dest_flat, xnp, n_rows):
    t = xnp.shape[0] // X_SUB
    grid_spec = pltpu.PrefetchScalarGridSpec(
        num_scalar_prefetch=3,
        grid=(t // DISP_TT,),
        in_specs=[pl.BlockSpec((DISP_TT * TOP_K,), lambda i, *_: (i,), memory_space=pltpu.SMEM),
                  pl.BlockSpec((DISP_TT * X_SUB, LANES), lambda i, *_: (i, 0))],
        out_specs=pl.BlockSpec(memory_space=pl.ANY),
        scratch_shapes=[pltpu.VMEM((MOE_TM * X_SUB, LANES), U32), pltpu.SemaphoreType.DMA(()),
                        pltpu.SemaphoreType.DMA(())],
    )
    return pl.pallas_call(
        functools.partial(_dispatch_kernel, n_zero=n_rows - t * TOP_K),
        grid_spec=grid_spec,
        out_shape=jax.ShapeDtypeStruct((n_rows * X_SUB, LANES), U32),
        compiler_params=_cparams(("arbitrary",)),
        name="moe_dispatch",
    )(pad_start, pad_cnt, tail_start, dest_flat, xnp)


def _moe_kernel(be_ref, nsb_ref, na_ref, x_ref, wglu_ref, wlin_ref, bglu_ref, blin_ref, wd_ref, bd_ref, o_ref,
                xb_next, xb_cur, *accs):
    m = pl.program_id(0)
    f = pl.program_id(1)
    nf = pl.num_programs(1)
    n_sub = MOE_TM // MOE_SB

    def ffn(nsb, first, last):
        rows = nsb * MOE_SB
        if first:
            @pl.when(m == 0)
            def _():
                _unpack_bf16_pairs(x_ref, MOE_TM, xb_next)

        xb = xb_next if first else xb_cur
        if last:
            _unpack_bf16_pairs(x_ref, MOE_TM, xb_next)
        wd = wd_ref[0].astype(BF16)
        for i in range(nsb):
            rs = slice(i * MOE_SB, (i + 1) * MOE_SB)
            xi = xb[rs, :]
            if first:
                xb_cur[rs, :] = xi
            glu = jnp.minimum(_dot(xi, wglu_ref[0]) + bglu_ref[0], SWIGLU_LIMIT)
            lin = jnp.clip(_dot(xi, wlin_ref[0]) + blin_ref[0], -SWIGLU_LIMIT, SWIGLU_LIMIT)
            act = glu * _sigmoid(SWIGLU_ALPHA * glu) * (lin + 1.0)
            part = _dot(act.astype(BF16), wd)
            accs[i][...] = (bd_ref[0] + part) if first else (accs[i][...] + part)

        if last:
            for i in range(nsb):
                packed = _pack_bf16_pairs(accs[i][...])
                for sub in range(X_SUB):
                    o_ref[pl.ds(i * MOE_SB * X_SUB + sub, MOE_SB, stride=X_SUB), :] = (
                        packed[:, sub * LANES:(sub + 1) * LANES])
            if nsb < n_sub:
                o_ref[rows * X_SUB:, :] = jnp.zeros(((MOE_TM - rows) * X_SUB, LANES), U32)

    active = m < na_ref[0]
    kinds = [(True, False, f == 0), (False, True, f == nf - 1)]
    for nsb in range(1, n_sub + 1):
        for first, last, when in kinds:
            cond = jnp.logical_and(jnp.logical_and(active, nsb_ref[m] == nsb), when)
            pl.when(cond)(functools.partial(ffn, nsb, first, last))

    @pl.when(jnp.logical_and(jnp.logical_not(active), f == nf - 1))
    def _():
        o_ref[...] = jnp.zeros_like(o_ref)


def _moe_ffn(block_expert, block_nsub, n_active, x_rows, w_glu, w_lin, b_glu, b_lin, w_down, b_down, n_blocks):
    nf = D_FF // MOE_TF
    assert nf == 2, "the kernel has a first-step and a last-step body and no middle one"

    def blk(m, na):
        return jnp.maximum(jnp.minimum(m, na[0] - 1), 0)

    def ftile(m, f, na):
        return jnp.where(m < na[0], f, nf - 1)

    def expert(m, be, na):
        return be[blk(m, na)]

    col_tile = lambda m, f, be, ns, na: (expert(m, be, na), 0, ftile(m, f, na))
    grid_spec = pltpu.PrefetchScalarGridSpec(
        num_scalar_prefetch=3,
        grid=(n_blocks, nf),
        in_specs=[pl.BlockSpec((MOE_TM * X_SUB, LANES),
                               lambda m, f, be, ns, na: (blk(jnp.where(f == nf - 1, m + 1, m), na), 0)),
                  pl.BlockSpec((1, D_MODEL, MOE_TF), col_tile),
                  pl.BlockSpec((1, D_MODEL, MOE_TF), col_tile),
                  pl.BlockSpec((1, 1, MOE_TF), col_tile),
                  pl.BlockSpec((1, 1, MOE_TF), col_tile),
                  pl.BlockSpec((1, MOE_TF, D_MODEL), lambda m, f, be, ns, na: (expert(m, be, na), ftile(m, f, na), 0)),
                  pl.BlockSpec((1, 1, D_MODEL), lambda m, f, be, ns, na: (expert(m, be, na), 0, 0))],
        out_specs=pl.BlockSpec((MOE_TM * X_SUB, LANES), lambda m, f, be, ns, na: (m, 0)),
        scratch_shapes=[pltpu.VMEM((MOE_TM, D_MODEL), BF16)] * 2
        + [pltpu.VMEM((MOE_SB, D_MODEL), F32)] * (MOE_TM // MOE_SB),
    )
    return pl.pallas_call(
        _moe_kernel,
        grid_spec=grid_spec,
        out_shape=jax.ShapeDtypeStruct((n_blocks * MOE_TM * X_SUB, LANES), U32),
        compiler_params=_cparams(("arbitrary", "arbitrary")),
        name="moe_ffn",
    )(block_expert, block_nsub, n_active, x_rows, w_glu, w_lin, b_glu, b_lin, w_down, b_down)


def _combine_kernel(dest_ref, dest_next_ref, h1_ref, selg_ref, fw_ref, rows_ref, o_ref, buf, h_scr, sem):
    tt = h1_ref.shape[0]
    i = pl.program_id(0)
    n = pl.num_programs(0)
    half = D_MODEL // 2

    def row_copy(slot, t, k, d):
        src = rows_ref.at[pl.ds(pl.multiple_of(d * X_SUB, X_SUB), X_SUB)]
        dst = buf.at[slot, k, pl.ds(pl.multiple_of(t * X_SUB, X_SUB), X_SUB)]
        return pltpu.make_async_copy(src, dst, sem.at[slot])

    def gather(slot, d_ref):
        def issue(t, carry):
            for k in range(TOP_K):
                row_copy(slot, t, k, d_ref[t * TOP_K + k]).start(priority=k % 2)
            return carry

        lax.fori_loop(0, tt, issue, 0, unroll=4)

    slot = i % 2

    @pl.when(i == 0)
    def _():
        gather(0, dest_ref)

    @pl.when(i + 1 < n)
    def _():
        gather(1 - slot, dest_next_ref)

    for k in range(TOP_K):
        pltpu.make_async_copy(rows_ref.at[pl.ds(0, tt * X_SUB)], buf.at[slot, k], sem.at[slot]).wait()

    gates = selg_ref[...]
    for sub in range(X_SUB):
        lo_cs = slice(sub * LANES, (sub + 1) * LANES)
        hi_cs = slice(half + sub * LANES, half + (sub + 1) * LANES)
        h_lo = h1_ref[:, lo_cs]
        h_hi = h1_ref[:, hi_cs]
        for k in range(TOP_K):
            u = buf[slot, k, pl.ds(sub, tt, stride=X_SUB), :]
            g = gates[:, k:k + 1]
            h_lo = h_lo + g * pltpu.bitcast(u << 16, F32)
            h_hi = h_hi + g * pltpu.bitcast(u & jnp.uint32(0xFFFF0000), F32)
        h_scr[:, lo_cs] = h_lo
        h_scr[:, hi_cs] = h_hi
    h = h_scr[...]
    o_ref[...] = h * lax.rsqrt(jnp.mean(h * h, axis=-1, keepdims=True) + EPS) * fw_ref[...]


def _combine(dest_flat, h1, selg, final_w, y_rows):
    t = h1.shape[0]
    tt = COMB_TT
    n = t // tt
    return pl.pallas_call(
        _combine_kernel,
        grid=(n,),
        in_specs=[pl.BlockSpec((tt * TOP_K,), lambda i: (i,), memory_space=pltpu.SMEM),
                  pl.BlockSpec((tt * TOP_K,), lambda i: (jnp.minimum(i + 1, n - 1),), memory_space=pltpu.SMEM),
                  pl.BlockSpec((tt, D_MODEL), lambda i: (i, 0)),
                  pl.BlockSpec((tt, LANES), lambda i: (i, 0)),
                  pl.BlockSpec((1, D_MODEL), lambda i: (0, 0)),
                  pl.BlockSpec(memory_space=pl.ANY)],
        out_specs=pl.BlockSpec((tt, D_MODEL), lambda i: (i, 0)),
        out_shape=jax.ShapeDtypeStruct((t, D_MODEL), F32),
        scratch_shapes=[pltpu.VMEM((2, TOP_K, tt * X_SUB, LANES), U32), pltpu.VMEM((tt, D_MODEL), F32),
                        pltpu.SemaphoreType.DMA((2,))],
        compiler_params=_cparams(("arbitrary",)),
        name="moe_combine",
    )(dest_flat, dest_flat, h1, selg, final_w, y_rows)


def kernel(x, norm1_w, w_in, conv_w, conv_b, dt_bias, a_log, d_skip, ssd_norm_w, i_bias, f_bias, ml_norm_w,
           w_out, norm2_w, router_w, router_b, w_gate_up, b_gate_up, w_down, b_down, final_norm_w):
    b, s, d = x.shape
    t = b * s
    assert w_in.shape[0] == 1, "single-layer problem"
    h = x.reshape(t, d)
    for l in range(1):
        wi = w_in[l]
        o_z, o_xbc, o_dt = 0, D_SSD, D_SSD + D_SSD + 2 * SSD_GROUPS * SSD_STATE
        o_q = o_dt + SSD_HEADS
        o_k = o_q + ML_HEADS * ML_DQK
        o_v = o_k + ML_HEADS * ML_DQK
        o_o = o_v + D_ML
        o_i = o_o + D_ML
        o_f = o_i + ML_HEADS
        assert o_i - (o_q - o_dt) == D_BIG
        w_big = _regroup_in_weights(w_in, o_dt, o_q - o_dt)
        w_small = jnp.concatenate(
            [wi[:, o_dt:o_q], wi[:, o_i:o_f], wi[:, o_f:o_f + ML_HEADS],
             jnp.zeros((d, LANES - SSD_HEADS - 2 * ML_HEADS), F32)], axis=1).astype(BF16)

        proj, small = _input_projection(h, norm1_w[l].reshape(1, d), w_big, w_small)
        proj3 = proj.reshape(b, s, D_BIG)
        small3 = small.reshape(b, s, LANES)
        y_ssd = _ssd_mixer(proj3, small3, conv_w[l], conv_b[l], dt_bias[l], a_log[l], d_skip[l], ssd_norm_w[l], b, s)
        y_ml, w_glu, w_lin = _mlstm_mixer(proj3, small3, i_bias[l], f_bias[l], ml_norm_w[l], b, s, w_gate_up[l])

        rw = jnp.pad(router_w[l], ((0, 0), (0, LANES - N_EXPERTS)))
        rw_hi = rw.astype(BF16)
        rw = jnp.concatenate([rw_hi, (rw - rw_hi.astype(F32)).astype(BF16)], axis=1)
        rb = jnp.pad(router_b[l], (0, LANES - N_EXPERTS)).reshape(1, LANES)
        h1, xnp, seli, selg, cnt = _outproj_router(
            h, y_ssd.reshape(t, D_SSD), y_ml.reshape(t, D_ML), w_out[l].astype(BF16), norm2_w[l].reshape(1, d), rw, rb)

        n_blocks = (t * TOP_K) // MOE_TM + N_EXPERTS
        dest, table = _route_offsets(seli, cnt, n_blocks)
        dest_flat = dest[:, :TOP_K].reshape(-1)
        block_expert = table[:n_blocks, 0]
        n_active = table[0:1, 1]
        block_nsub = table[:n_blocks, 2]

        x_rows = _dispatch(table[:N_EXPERTS, 3], table[:N_EXPERTS, 4], table[0:1, 5], dest_flat, xnp,
                           n_blocks * MOE_TM)
        b_glu = b_gate_up[l][:, 0::2].reshape(N_EXPERTS, 1, D_FF)
        b_lin = b_gate_up[l][:, 1::2].reshape(N_EXPERTS, 1, D_FF)
        y_rows = _moe_ffn(block_expert, block_nsub, n_active, x_rows, w_glu, w_lin, b_glu, b_lin, w_down[l],
                          b_down[l].reshape(N_EXPERTS, 1, d), n_blocks)
        h = _combine(dest_flat, h1, selg, final_norm_w.reshape(1, d), y_rows)
    return h.reshape(b, s, d)
```

```python
import functools

import jax
import jax.numpy as jnp
from jax import lax
from jax.experimental import pallas as pl
from jax.experimental.pallas import tpu as pltpu

F32 = jnp.float32
BF16 = jnp.bfloat16
I32 = jnp.int32
U32 = jnp.uint32

LANES = 128
SUBLANES = 8
VMEM_LIMIT = 56 * 1024 * 1024

D_MODEL = 2048
EPS = 1e-5
SSD_HEAD_DIM = 64
SSD_HEADS = 32
D_SSD = 2048
SSD_GROUPS = 4
SSD_STATE = 128
SSD_CONV = 4
CHUNK = 128
ML_HEADS = 8
ML_DV = 256
ML_DQK = 128
D_ML = 2048
N_EXPERTS = 32
TOP_K = 4
D_FF = 2048
SWIGLU_ALPHA = 1.702
SWIGLU_LIMIT = 7.0

COL_Z, COL_X, COL_BC, COL_Q, COL_K, COL_V, COL_O = 0, 2048, 4096, 5120, 6144, 7168, 9216
COL_W = 1024
D_BIG = 11264
LANE_DT, LANE_I, LANE_F = 0, 32, 40

IN_TM, IN_TN = 1024, 1024
OUT_TM = 512
ROUTE_TM = 2048
DISP_TT = 1024
MOE_TM = 512
MOE_SB = 256
MOE_TF = 1024
PERM_W = 256
PREP_TK = 512
COMB_TT = 512
X_SUB = D_MODEL // 2 // LANES


def _cparams(sem, vmem=VMEM_LIMIT):
    return pltpu.CompilerParams(dimension_semantics=sem, vmem_limit_bytes=vmem)


def _dot(a, b):
    return jnp.dot(a, b, preferred_element_type=F32)


def _dot_exact(a, b):
    return jnp.dot(a, b, preferred_element_type=F32, precision=lax.Precision.HIGHEST)


def _split2_dot(x, e2_bf16):
    hi = x.astype(BF16)
    lo = (x - hi.astype(F32)).astype(BF16)
    return _dot(jnp.concatenate([hi, lo], axis=1), e2_bf16)


def _sigmoid(x):
    return 1.0 / (1.0 + jnp.exp(-x))


def _softplus(x):
    return jnp.maximum(x, 0.0) + jnp.log1p(jnp.exp(-jnp.abs(x)))


def _tril(n, strict=False):
    r = lax.broadcasted_iota(I32, (n, n), 0)
    c = lax.broadcasted_iota(I32, (n, n), 1)
    return (r > c) if strict else (r >= c)


def _pack_bf16_pairs(y):
    half = D_MODEL // 2
    lo = pltpu.bitcast(y[:, :half].astype(BF16).astype(F32), U32)
    hi = pltpu.bitcast(y[:, half:].astype(BF16).astype(F32), U32)
    return hi | (lo >> 16)


def _unpack_bf16_pairs(src_ref, rows, dst_ref):
    half = D_MODEL // 2
    for sub in range(X_SUB):
        u = src_ref[pl.ds(sub, rows, stride=X_SUB), :]
        dst_ref[0:rows, sub * LANES:(sub + 1) * LANES] = pltpu.bitcast(u << 16, F32).astype(BF16)
        dst_ref[0:rows, half + sub * LANES:half + (sub + 1) * LANES] = (
            pltpu.bitcast(u & jnp.uint32(0xFFFF0000), F32).astype(BF16))


def _deinterleave_tile(w_ref, p_ref, glu_ref, lin_ref):
    hw = PERM_W // 2
    for s in range(w_ref.shape[2] // PERM_W):
        t = _dot(w_ref[0, :, s * PERM_W:(s + 1) * PERM_W].astype(BF16), p_ref[...])
        glu_ref[0, :, s * hw:(s + 1) * hw] = t[:, :hw].astype(BF16)
        lin_ref[0, :, s * hw:(s + 1) * hw] = t[:, hw:].astype(BF16)


def _deinterleave_perm():
    src = jnp.arange(PERM_W)
    dst = jnp.where(src % 2 == 0, src // 2, PERM_W // 2 + src // 2)
    return (dst[:, None] == jnp.arange(PERM_W)[None, :]).astype(BF16)


def _deinterleave_specs(w_gate_up, tile_index):
    e, k, n = w_gate_up.shape

    def at(*ids):
        ei, ci = tile_index(*ids)
        return ei, ci, 0

    in_specs = [pl.BlockSpec((1, PREP_TK, n), at), pl.BlockSpec((PERM_W, PERM_W), lambda *ids: (0, 0))]
    out_specs = [pl.BlockSpec((1, PREP_TK, n // 2), at)] * 2
    out_shape = [jax.ShapeDtypeStruct((e, k, n // 2), BF16)] * 2
    return in_specs, out_specs, out_shape


def _deinterleave_gate_up(w_gate_up):
    e, k, _ = w_gate_up.shape
    in_specs, out_specs, out_shape = _deinterleave_specs(w_gate_up, lambda i, j: (i, j))
    return pl.pallas_call(
        _deinterleave_tile,
        grid=(e, k // PREP_TK),
        in_specs=in_specs,
        out_specs=out_specs,
        out_shape=out_shape,
        compiler_params=_cparams(("arbitrary", "arbitrary")),
        name="moe_deinterleave",
    )(w_gate_up, _deinterleave_perm())


def _inproj_kernel(x_ref, nw_ref, w_ref, ws_ref, o_ref, os_ref, n_scr):
    @pl.when(pl.program_id(1) == 0)
    def _():
        x = x_ref[...]
        n = x * lax.rsqrt(jnp.mean(x * x, axis=-1, keepdims=True) + EPS) * nw_ref[...]
        nb = n.astype(BF16)
        n_scr[...] = nb
        os_ref[...] = _dot(nb, ws_ref[...])

    o_ref[...] = _dot(n_scr[...], w_ref[...]).astype(BF16)


def _input_projection(xf, norm_w, w_big, w_small):
    t = xf.shape[0]
    return pl.pallas_call(
        _inproj_kernel,
        grid=(t // IN_TM, D_BIG // IN_TN),
        in_specs=[pl.BlockSpec((IN_TM, D_MODEL), lambda i, j: (i, 0)),
                  pl.BlockSpec((1, D_MODEL), lambda i, j: (0, 0)),
                  pl.BlockSpec((D_MODEL, IN_TN), lambda i, j: (0, j)),
                  pl.BlockSpec((D_MODEL, LANES), lambda i, j: (0, 0))],
        out_specs=[pl.BlockSpec((IN_TM, IN_TN), lambda i, j: (i, j)),
                   pl.BlockSpec((IN_TM, LANES), lambda i, j: (i, 0))],
        out_shape=[jax.ShapeDtypeStruct((t, D_BIG), BF16),
                   jax.ShapeDtypeStruct((t, LANES), F32)],
        scratch_shapes=[pltpu.VMEM((IN_TM, D_MODEL), BF16)],
        compiler_params=_cparams(("arbitrary", "arbitrary")),
        name="inproj",
    )(xf, norm_w, w_big, w_small)


def _ssd_kernel(z_ref, x_ref, bc_ref, sm_ref, cw_ref, cb_ref, shift_ref, dtb_ref, alog_ref,
                dskip_ref, nw_ref, e_ref, o_ref, ext, y_scr, *states):
    q = CHUNK

    @pl.when(pl.program_id(1) == 0)
    def _():
        ext[0:q, :] = jnp.zeros((q, ext.shape[1]), BF16)
        for state in states:
            state[...] = jnp.zeros_like(state)

    ext[q:2 * q, 0:D_SSD] = x_ref[0]
    ext[q:2 * q, D_SSD:] = bc_ref[0]
    shifted = _dot(shift_ref[...], ext[...])
    acc = cb_ref[...] + ext[q:2 * q, :].astype(F32) * cw_ref[SSD_CONV - 1:SSD_CONV, :]
    for j in range(SSD_CONV - 1):
        acc = acc + shifted[j * q:(j + 1) * q, :] * cw_ref[j:j + 1, :]
    xbc = acc * _sigmoid(acc)
    xs = xbc[:, :D_SSD]
    bcs = xbc[:, D_SSD:]
    ext[0:q, :] = ext[q:2 * q, :]

    lane = lax.broadcasted_iota(I32, (1, LANES), 1)
    g = sm_ref[0]
    dt = _softplus(g + dtb_ref[...])
    a = jnp.where(lane < SSD_HEADS, -jnp.exp(alog_ref[...]), 0.0)
    da = dt * a
    tri = _tril(q).astype(F32)
    a_cum = _dot_exact(tri, da)
    a_cum_t = a_cum.T
    a_end = a_cum[q - 1:q, :]
    stack = jnp.concatenate(
        [dt, jnp.exp(a_end - a_cum), jnp.exp(a_cum), jnp.broadcast_to(jnp.exp(a_end), (SUBLANES, LANES))], axis=0)
    ex = _split2_dot(stack, e_ref[...])
    dt_x = ex[0:q]
    to_end_x = ex[q:2 * q]
    from_start_x = ex[2 * q:3 * q]
    chunk_decay_x = ex[3 * q:3 * q + 1]

    xdt = xs * dt_x
    xdt_b = xdt.astype(BF16)
    xde_b = (xdt * to_end_x).astype(BF16)
    causal = _tril(q)
    lane_q = lax.broadcasted_iota(I32, (1, LANES), 1)
    gw = SSD_STATE
    hpg = SSD_HEADS // SSD_GROUPS
    cpg = D_SSD // SSD_GROUPS
    for gi in range(SSD_GROUPS):
        bg = bcs[:, gi * gw:(gi + 1) * gw]
        cg_b = bcs[:, SSD_GROUPS * gw + gi * gw:SSD_GROUPS * gw + (gi + 1) * gw].astype(BF16)
        cb = lax.dot_general(cg_b, bg.astype(BF16), (((1,), (1,)), ((), ())), preferred_element_type=F32)
        bg_t = bg.T.astype(BF16)
        cs = slice(gi * cpg, (gi + 1) * cpg)
        prev = states[gi][...]
        y_off = _dot(cg_b, prev.astype(BF16)) * from_start_x[:, cs]
        states[gi][...] = prev * chunk_decay_x[:, cs] + _dot(bg_t, xde_b[:, cs])
        for pr in range(hpg // 2):
            c0 = gi * cpg + pr * LANES
            xp = xdt_b[:, c0:c0 + LANES]
            acc = y_off[:, pr * LANES:(pr + 1) * LANES]
            for sub in range(2):
                h = gi * hpg + pr * 2 + sub
                seg = a_cum[:, h:h + 1] - a_cum_t[h:h + 1, :]
                decay = jnp.exp(jnp.where(causal, seg, -jnp.inf))
                m = (cb * decay).astype(BF16)
                keep = (lane_q < SSD_HEAD_DIM) if sub == 0 else (lane_q >= SSD_HEAD_DIM)
                acc = acc + _dot(m, jnp.where(keep, xp, jnp.zeros_like(xp)))
            y_scr[:, c0:c0 + LANES] = acc

    y = y_scr[...] + xs * dskip_ref[...]
    zf = z_ref[0].astype(F32)
    y = y * (zf * _sigmoid(zf))
    for gi in range(SSD_GROUPS):
        cs = slice(gi * cpg, (gi + 1) * cpg)
        yg = y[:, cs]
        yg = yg * lax.rsqrt(jnp.mean(yg * yg, axis=-1, keepdims=True) + EPS)
        o_ref[0, :, cs] = (yg * nw_ref[:, cs]).astype(BF16)


def _ssd_mixer(proj, small, conv_w, conv_b, dt_bias, a_log, d_skip, norm_w, b, s):
    nc = s // CHUNK
    pad = LANES - SSD_HEADS
    dtb = jnp.pad(dt_bias, (0, pad)).reshape(1, LANES)
    alog = jnp.pad(a_log, (0, pad)).reshape(1, LANES)
    dskip = jnp.repeat(d_skip, SSD_HEAD_DIM).reshape(1, D_SSD)
    expand = (jnp.arange(LANES)[:, None] == (jnp.arange(D_SSD) // SSD_HEAD_DIM)[None, :]).astype(BF16)
    expand = jnp.concatenate([expand, expand], axis=0)
    d_bc = 2 * SSD_GROUPS * SSD_STATE
    d_conv = D_SSD + d_bc
    tt = jnp.arange((SSD_CONV - 1) * CHUNK)
    back = (SSD_CONV - 1) - tt // CHUNK
    shift = ((CHUNK + tt % CHUNK - back)[:, None] == jnp.arange(2 * CHUNK)[None, :]).astype(BF16)
    const = lambda shape: pl.BlockSpec(shape, lambda i, j: (0,) * len(shape))
    col = lambda width, off: pl.BlockSpec((1, CHUNK, width), lambda i, j: (i, j, off // width))
    return pl.pallas_call(
        _ssd_kernel,
        grid=(b, nc),
        in_specs=[col(D_SSD, COL_Z), col(D_SSD, COL_X), col(d_bc, COL_BC), col(LANES, 0),
                  const((SSD_CONV, d_conv)), const((1, d_conv)), const(((SSD_CONV - 1) * CHUNK, 2 * CHUNK)),
                  const((1, LANES)), const((1, LANES)), const((1, D_SSD)), const((1, D_SSD)),
                  const((2 * LANES, D_SSD))],
        out_specs=pl.BlockSpec((1, CHUNK, D_SSD), lambda i, j: (i, j, 0)),
        out_shape=jax.ShapeDtypeStruct((b, s, D_SSD), BF16),
        scratch_shapes=[pltpu.VMEM((2 * CHUNK, d_conv), BF16),
                        pltpu.VMEM((CHUNK, D_SSD), F32)]
        + [pltpu.VMEM((SSD_STATE, D_SSD // SSD_GROUPS), F32)] * SSD_GROUPS,
        compiler_params=_cparams(("arbitrary", "arbitrary")),
        name="ssd_mixer",
    )(proj, proj, proj, small, conv_w, conv_b.reshape(1, -1), shift, dtb, alog, dskip, norm_w.reshape(1, -1), expand)


def _mlstm_kernel(q_ref, k_ref, v0_ref, v1_ref, o0_ref, o1_ref, sm_ref, bias_ref, nw_ref, *rest, prep_gate_up):
    L = CHUNK
    heads_per_ref = COL_W // ML_DV

    def head_cols(refs, h):
        c0 = (h % heads_per_ref) * ML_DV
        return refs[h // heads_per_ref][0, :, c0:c0 + ML_DV]

    if prep_gate_up:
        wg_ref, perm_ref, out_ref, glu_ref, lin_ref, n_scr, m_scr, *ct_scrs = rest
        _deinterleave_tile(wg_ref, perm_ref, glu_ref, lin_ref)
    else:
        out_ref, n_scr, m_scr, *ct_scrs = rest

    @pl.when(pl.program_id(1) == 0)
    def _():
        for ct_scr in ct_scrs:
            ct_scr[...] = jnp.zeros_like(ct_scr)
        n_scr[...] = jnp.zeros_like(n_scr)
        m_scr[...] = jnp.zeros_like(m_scr)

    n_all = n_scr[...]
    m_all = m_scr[...]
    n_new_all = n_all
    m_new_all = m_all
    head_row = lax.broadcasted_iota(I32, (ML_HEADS, LANES), 0)

    g = sm_ref[0] + bias_ref[...]
    log_f = jnp.minimum(g, 0.0) - jnp.log1p(jnp.exp(-jnp.abs(g)))
    tri = _tril(L).astype(F32)
    bcum = _dot_exact(tri, log_f)
    bcum_t = bcum.T
    keys_before = jnp.logical_not(_tril(L, strict=True))
    row16 = lax.broadcasted_iota(I32, (2 * SUBLANES, ML_DQK), 0)
    scale = ML_DQK ** -0.5
    nt = (((1,), (1,)), ((), ()))
    tn = (((0,), (0,)), ((), ()))
    for h in range(ML_HEADS):
        li, lf = LANE_I + h, LANE_F + h
        li_col, b_col, b_row = g[:, li:li + 1], bcum[:, lf:lf + 1], bcum_t[lf:lf + 1, :]
        btot = bcum[L - 1:L, lf:lf + 1]
        m_prev = m_all[h:h + 1, 0:1]
        ct_prev = ct_scrs[h][...]
        n_prev = n_all[h:h + 1, :]
        qb = q_ref[0, :, h * ML_DQK:(h + 1) * ML_DQK]
        kb = k_ref[0, :, h * ML_DQK:(h + 1) * ML_DQK]
        vb = head_cols((v0_ref, v1_ref), h)
        w_end = btot - b_col + li_col
        m_loc = jnp.max(w_end, axis=0, keepdims=True)
        k_end = kb.astype(F32) * jnp.exp(w_end - m_loc)
        kv_t = lax.dot_general(vb, k_end.astype(BF16), tn, preferred_element_type=F32)
        nk = jnp.sum(k_end, axis=0, keepdims=True)
        d_log = jnp.where(keys_before, (li_col - b_col) + b_row, -jnp.inf)
        m_inter = b_row + m_prev
        m_t = jnp.maximum(m_inter, jnp.max(d_log, axis=0, keepdims=True))
        s_t = lax.dot_general(kb, qb, nt, preferred_element_type=F32)
        s_t = s_t * (scale * jnp.exp(d_log - m_t))
        inter = scale * jnp.exp(m_inter - m_t)
        num_t = (lax.dot_general(vb, s_t.astype(BF16), tn, preferred_element_type=F32)
                 + lax.dot_general(ct_prev.astype(BF16), qb, nt, preferred_element_type=F32) * inter)
        n_hi = n_prev.astype(BF16).astype(F32)
        n_rows = jnp.where(row16 == 0, n_hi, jnp.where(row16 == 1, n_prev - n_hi, 0.0)).astype(BF16)
        qn = lax.dot_general(n_rows, qb, nt, preferred_element_type=F32)
        den = jnp.sum(s_t, axis=0, keepdims=True) + (qn[0:1, :] + qn[1:2, :]) * inter
        den = jnp.maximum(jnp.abs(den), jnp.exp(-m_t))
        ht_t = num_t / den
        ht_t = ht_t * lax.rsqrt(jnp.mean(ht_t * ht_t, axis=0, keepdims=True) + EPS)
        vs = slice(h * ML_DV, (h + 1) * ML_DV)
        og = _sigmoid(head_cols((o0_ref, o1_ref), h).astype(F32))
        out_ref[0, :, vs] = (ht_t.T * nw_ref[:, vs] * og).astype(BF16)
        m_new = jnp.maximum(btot + m_prev, m_loc)
        fa = jnp.exp(btot + m_prev - m_new)
        fg = jnp.exp(m_loc - m_new)
        ct_scrs[h][...] = fa * ct_prev + fg * kv_t
        n_new_all = jnp.where(head_row == h, fa * n_prev + fg * nk, n_new_all)
        m_new_all = jnp.where(head_row == h, m_new, m_new_all)
    n_scr[...] = n_new_all
    m_scr[...] = m_new_all


def _mlstm_mixer(proj, small, i_bias, f_bias, norm_w, b, s, w_gate_up):
    nc = s // CHUNK
    bias = jnp.zeros((LANES,), F32).at[LANE_I:LANE_I + ML_HEADS].set(i_bias).at[LANE_F:LANE_F + ML_HEADS].set(f_bias)
    d_qk = ML_HEADS * ML_DQK
    const = lambda shape: pl.BlockSpec(shape, lambda i, j: (0,) * len(shape))
    col = lambda width, off: pl.BlockSpec((1, CHUNK, width), lambda i, j: (i, j, off // width))
    e, k, _ = w_gate_up.shape
    tiles_per_expert = k // PREP_TK
    prep = b * nc == e * tiles_per_expert
    in_specs = [col(d_qk, COL_Q), col(d_qk, COL_K), col(COL_W, COL_V), col(COL_W, COL_V + COL_W),
                col(COL_W, COL_O), col(COL_W, COL_O + COL_W), col(LANES, 0), const((1, LANES)), const((1, D_ML))]
    out_specs = [pl.BlockSpec((1, CHUNK, D_ML), lambda i, j: (i, j, 0))]
    out_shape = [jax.ShapeDtypeStruct((b, s, D_ML), BF16)]
    args = [proj] * 6 + [small, bias.reshape(1, LANES), norm_w.reshape(1, -1)]
    if prep:
        w_in_specs, w_out_specs, w_out_shape = _deinterleave_specs(
            w_gate_up, lambda i, j: ((i * nc + j) // tiles_per_expert, (i * nc + j) % tiles_per_expert))
        in_specs += w_in_specs
        out_specs += w_out_specs
        out_shape += w_out_shape
        args += [w_gate_up, _deinterleave_perm()]
    outs = pl.pallas_call(
        functools.partial(_mlstm_kernel, prep_gate_up=prep),
        grid=(b, nc),
        in_specs=in_specs,
        out_specs=out_specs,
        out_shape=out_shape,
        scratch_shapes=[pltpu.VMEM((ML_HEADS, ML_DQK), F32), pltpu.VMEM((ML_HEADS, LANES), F32)]
        + [pltpu.VMEM((ML_DV, ML_DQK), F32)] * ML_HEADS,
        compiler_params=_cparams(("arbitrary", "arbitrary")),
        name="mlstm_mixer",
    )(*args)
    if prep:
        return outs[0], outs[1], outs[2]
    return (outs[0], *_deinterleave_gate_up(w_gate_up))


def _outproj_router_kernel(x_ref, ys_ref, ym_ref, wo_ref, n2w_ref, rw_ref, rb_ref,
                           h1_ref, xnp_ref, seli_ref, selg_ref, cnt_ref, carry):
    tm = x_ref.shape[0]

    @pl.when(pl.program_id(0) == 0)
    def _():
        carry[...] = jnp.zeros_like(carry)

    h1 = x_ref[...] + _dot(ys_ref[...], wo_ref[0:D_SSD, :]) + _dot(ym_ref[...], wo_ref[D_SSD:, :])
    h1_ref[...] = h1
    xn = h1 * lax.rsqrt(jnp.mean(h1 * h1, axis=-1, keepdims=True) + EPS) * n2w_ref[...]
    packed = _pack_bf16_pairs(xn)
    for sub in range(X_SUB):
        xnp_ref[pl.ds(sub, tm, stride=X_SUB), :] = packed[:, sub * LANES:(sub + 1) * LANES]

    lane = lax.broadcasted_iota(I32, (tm, LANES), 1)
    lane_f = lane.astype(F32)
    xn_hi = xn.astype(BF16)
    xn_lo = (xn - xn_hi.astype(F32)).astype(BF16)
    hh_hl = _dot(xn_hi, rw_ref[...])
    logits = hh_hl[:, :LANES] + hh_hl[:, LANES:] + _dot(xn_lo, rw_ref[:, :LANES]) + rb_ref[...]
    work = jnp.where(lane < N_EXPERTS, logits, -jnp.inf)
    vals, idxs = [], []
    chosen = jnp.zeros((tm, LANES), jnp.bool_)
    for _ in range(TOP_K):
        m = jnp.max(work, axis=1, keepdims=True)
        idx = jnp.min(jnp.where(work == m, lane_f, float(LANES)), axis=1, keepdims=True)
        sel = lane_f == idx
        vals.append(m)
        idxs.append(idx)
        chosen = jnp.logical_or(chosen, sel)
        work = jnp.where(sel, -jnp.inf, work)
    exps = [jnp.exp(v - vals[0]) for v in vals]
    denom = exps[0] + exps[1] + exps[2] + exps[3]

    onehot = jnp.where(chosen, 1.0, 0.0)
    ranks = _dot(_tril(tm, strict=True).astype(BF16), onehot.astype(BF16)) + carry[...]
    total = carry[...] + jnp.sum(onehot, axis=0, keepdims=True)
    carry[...] = total
    cnt_ref[...] = jnp.broadcast_to(total, cnt_ref.shape)

    seli = jnp.zeros((tm, LANES), F32)
    selg = jnp.zeros((tm, LANES), F32)
    for k in range(TOP_K):
        rank_k = jnp.sum(jnp.where(lane_f == idxs[k], ranks, 0.0), axis=1, keepdims=True)
        seli = jnp.where(lane == k, idxs[k], seli)
        seli = jnp.where(lane == TOP_K + k, rank_k, seli)
        selg = jnp.where(lane == k, exps[k] / denom, selg)
    seli_ref[...] = seli.astype(I32)
    selg_ref[...] = selg


def _outproj_router(xf, y_ssd, y_ml, w_out_b, norm2_w, router_w_pad, router_b_pad):
    t = xf.shape[0]
    tm = OUT_TM
    row = lambda width: pl.BlockSpec((tm, width), lambda i: (i, 0))
    const = lambda shape: pl.BlockSpec(shape, lambda i: (0,) * len(shape))
    return pl.pallas_call(
        _outproj_router_kernel,
        grid=(t // tm,),
        in_specs=[row(D_MODEL), row(D_SSD), row(D_ML),
                  pl.BlockSpec((D_SSD + D_ML, D_MODEL), lambda i: (0, 0), pipeline_mode=pl.Buffered(1)),
                  const((1, D_MODEL)),
                  const((D_MODEL, 2 * LANES)), const((1, LANES))],
        out_specs=[row(D_MODEL), pl.BlockSpec((tm * X_SUB, LANES), lambda i: (i, 0)), row(LANES), row(LANES),
                   const((SUBLANES, LANES))],
        out_shape=[jax.ShapeDtypeStruct((t, D_MODEL), F32),
                   jax.ShapeDtypeStruct((t * X_SUB, LANES), U32),
                   jax.ShapeDtypeStruct((t, LANES), I32),
                   jax.ShapeDtypeStruct((t, LANES), F32),
                   jax.ShapeDtypeStruct((SUBLANES, LANES), F32)],
        scratch_shapes=[pltpu.VMEM((1, LANES), F32)],
        compiler_params=_cparams(("arbitrary",)),
        name="outproj_router",
    )(xf, y_ssd, y_ml, w_out_b, norm2_w, router_w_pad, router_b_pad)


def _route_kernel(seli_ref, cnt_ref, dest_ref, be_ref):
    tm = seli_ref.shape[0]
    nbp = be_ref.shape[0]
    lane1 = lax.broadcasted_iota(I32, (1, LANES), 1)
    cnt = jnp.where(lane1 < N_EXPERTS, cnt_ref[0:1, :], 0.0)
    padded = jnp.ceil(cnt / MOE_TM) * MOE_TM
    r = lax.broadcasted_iota(I32, (LANES, LANES), 0)
    c = lax.broadcasted_iota(I32, (LANES, LANES), 1)
    upper = jnp.where(r <= c, 1.0, 0.0)
    pend = jnp.round(_dot_exact(jnp.broadcast_to(padded, (SUBLANES, LANES)), upper))[0:1, :]
    pstart = pend - padded

    lane = lax.broadcasted_iota(I32, (tm, LANES), 1)
    lane_f = lane.astype(F32)
    seli = seli_ref[...].astype(F32)
    dest = jnp.zeros((tm, LANES), F32)
    for k in range(TOP_K):
        idx_k = jnp.sum(jnp.where(lane == k, seli, 0.0), axis=1, keepdims=True)
        rank_k = jnp.sum(jnp.where(lane == TOP_K + k, seli, 0.0), axis=1, keepdims=True)
        start_k = jnp.sum(jnp.where(lane_f == idx_k, pstart, 0.0), axis=1, keepdims=True)
        dest = jnp.where(lane == k, start_k + rank_k, dest)
    dest_ref[...] = dest.astype(I32)

    blk_start = (lax.broadcasted_iota(I32, (nbp, LANES), 0) * MOE_TM).astype(F32)
    lane_b = lax.broadcasted_iota(I32, (nbp, LANES), 1)
    passed = jnp.where(jnp.logical_and(lane_b < N_EXPERTS, pend <= blk_start), 1.0, 0.0)
    expert = jnp.minimum(jnp.sum(passed, axis=1, keepdims=True), float(N_EXPERTS - 1))
    n_active = pend[:, N_EXPERTS - 1:N_EXPERTS] / MOE_TM
    tok_end = jnp.sum(jnp.where(lane_b.astype(F32) == expert, pstart + cnt, 0.0), axis=1, keepdims=True)
    valid = jnp.clip(tok_end - blk_start[:, 0:1], 0.0, float(MOE_TM))
    n_sub = jnp.ceil(valid / MOE_SB)
    own = lane_b == lax.broadcasted_iota(I32, (nbp, LANES), 0)
    pad_start = jnp.sum(jnp.where(own, pstart + cnt, 0.0), axis=1, keepdims=True)
    pad_cnt = jnp.sum(jnp.where(own, padded - cnt, 0.0), axis=1, keepdims=True)
    cols = (expert, n_active, n_sub, pad_start, pad_cnt, pend[:, N_EXPERTS - 1:N_EXPERTS])
    be = jnp.zeros((nbp, LANES), F32)
    for j, v in enumerate(cols):
        be = jnp.where(lane_b == j, v, be)
    be_ref[...] = be.astype(I32)


def _route_offsets(seli, cnt, n_blocks):
    t = seli.shape[0]
    nbp = -(-n_blocks // SUBLANES) * SUBLANES
    return pl.pallas_call(
        _route_kernel,
        grid=(t // ROUTE_TM,),
        in_specs=[pl.BlockSpec((ROUTE_TM, LANES), lambda i: (i, 0)),
                  pl.BlockSpec((SUBLANES, LANES), lambda i: (0, 0))],
        out_specs=[pl.BlockSpec((ROUTE_TM, LANES), lambda i: (i, 0)),
                   pl.BlockSpec((nbp, LANES), lambda i: (0, 0))],
        out_shape=[jax.ShapeDtypeStruct((t, LANES), I32),
                   jax.ShapeDtypeStruct((nbp, LANES), I32)],
        compiler_params=_cparams(("arbitrary",)),
        name="route_offsets",
    )(seli, cnt)


def _dispatch_kernel(pad_start_ref, pad_cnt_ref, tail_ref, dest_ref, xnp_ref, rows_ref, zbuf, sem, zsem, *, n_zero):
    tt = xnp_ref.shape[0] // X_SUB
    n_rows = rows_ref.shape[0] // X_SUB
    first = pl.program_id(0) == 0

    def zero_rows(start_row, n):
        dst = rows_ref.at[pl.ds(pl.multiple_of(start_row * X_SUB, X_SUB), n * X_SUB)]
        return pltpu.make_async_copy(zbuf.at[pl.ds(0, n * X_SUB)], dst, zsem)

    @pl.when(first)
    def _():
        zbuf[...] = jnp.zeros_like(zbuf)

        def expert_padding(e, carry):
            start, count = pad_start_ref[e], pad_cnt_ref[e]
            for k in reversed(range(MOE_TM.bit_length() - 1)):
                run = 1 << k
                pl.when((count & run) != 0)(
                    lambda run=run: zero_rows(start + (count & ~(2 * run - 1)), run).start())
            return carry

        lax.fori_loop(0, N_EXPERTS, expert_padding, 0)

        def trailing_block(j, carry):
            start = tail_ref[0] + j * MOE_TM
            pl.when(start < n_rows)(lambda: zero_rows(start, MOE_TM).start())
            return carry

        lax.fori_loop(0, N_EXPERTS, trailing_block, 0)

    def tile_copy(t, d):
        src = xnp_ref.at[pl.ds(pl.multiple_of(t * X_SUB, X_SUB), X_SUB)]
        dst = rows_ref.at[pl.ds(pl.multiple_of(d * X_SUB, X_SUB), X_SUB)]
        return pltpu.make_async_copy(src, dst, sem)

    def issue(t, carry):
        for k in range(TOP_K):
            tile_copy(t, dest_ref[t * TOP_K + k]).start(priority=k % 2)
        return carry

    lax.fori_loop(0, tt, issue, 0, unroll=4)
    for _ in range(TOP_K):
        pltpu.make_async_copy(xnp_ref, rows_ref.at[pl.ds(0, tt * X_SUB)], sem).wait()

    @pl.when(first)
    def _():
        span = rows_ref.at[pl.ds(0, n_zero * X_SUB)]
        pltpu.make_async_copy(span, span, zsem).wait()


def _dispatch(pad_start, pad_cnt, tail_start, dest_flat, xnp, n_rows):
    t = xnp.shape[0] // X_SUB
    grid_spec = pltpu.PrefetchScalarGridSpec(
        num_scalar_prefetch=3,
        grid=(t // DISP_TT,),
        in_specs=[pl.BlockSpec((DISP_TT * TOP_K,), lambda i, *_: (i,), memory_space=pltpu.SMEM),
                  pl.BlockSpec((DISP_TT * X_SUB, LANES), lambda i, *_: (i, 0))],
        out_specs=pl.BlockSpec(memory_space=pl.ANY),
        scratch_shapes=[pltpu.VMEM((MOE_TM * X_SUB, LANES), U32), pltpu.SemaphoreType.DMA(()),
                        pltpu.SemaphoreType.DMA(())],
    )
    return pl.pallas_call(
        functools.partial(_dispatch_kernel, n_zero=n_rows - t * TOP_K),
        grid_spec=grid_spec,
        out_shape=jax.ShapeDtypeStruct((n_rows * X_SUB, LANES), U32),
        compiler_params=_cparams(("arbitrary",)),
        name="moe_dispatch",
    )(pad_start, pad_cnt, tail_start, dest_flat, xnp)


def _moe_kernel(be_ref, nsb_ref, na_ref, x_ref, wglu_ref, wlin_ref, bglu_ref, blin_ref, wd_ref, bd_ref, o_ref,
                xb_next, xb_cur, *accs):
    m = pl.program_id(0)
    f = pl.program_id(1)
    nf = pl.num_programs(1)
    n_sub = MOE_TM // MOE_SB

    def ffn(nsb, first, last):
        rows = nsb * MOE_SB
        if first:
            @pl.when(m == 0)
            def _():
                _unpack_bf16_pairs(x_ref, MOE_TM, xb_next)

        xb = xb_next if first else xb_cur
        if last:
            _unpack_bf16_pairs(x_ref, MOE_TM, xb_next)
        wd = wd_ref[0].astype(BF16)
        for i in range(nsb):
            rs = slice(i * MOE_SB, (i + 1) * MOE_SB)
            xi = xb[rs, :]
            if first:
                xb_cur[rs, :] = xi
            glu = jnp.minimum(_dot(xi, wglu_ref[0]) + bglu_ref[0], SWIGLU_LIMIT)
            lin = jnp.clip(_dot(xi, wlin_ref[0]) + blin_ref[0], -SWIGLU_LIMIT, SWIGLU_LIMIT)
            act = glu * _sigmoid(SWIGLU_ALPHA * glu) * (lin + 1.0)
            part = _dot(act.astype(BF16), wd)
            accs[i][...] = (bd_ref[0] + part) if first else (accs[i][...] + part)

        if last:
            for i in range(nsb):
                packed = _pack_bf16_pairs(accs[i][...])
                for sub in range(X_SUB):
                    o_ref[pl.ds(i * MOE_SB * X_SUB + sub, MOE_SB, stride=X_SUB), :] = (
                        packed[:, sub * LANES:(sub + 1) * LANES])
            if nsb < n_sub:
                o_ref[rows * X_SUB:, :] = jnp.zeros(((MOE_TM - rows) * X_SUB, LANES), U32)

    active = m < na_ref[0]
    kinds = [(True, False, f == 0), (False, True, f == nf - 1)]
    for nsb in range(1, n_sub + 1):
        for first, last, when in kinds:
            cond = jnp.logical_and(jnp.logical_and(active, nsb_ref[m] == nsb), when)
            pl.when(cond)(functools.partial(ffn, nsb, first, last))

    @pl.when(jnp.logical_and(jnp.logical_not(active), f == nf - 1))
    def _():
        o_ref[...] = jnp.zeros_like(o_ref)


def _moe_ffn(block_expert, block_nsub, n_active, x_rows, w_glu, w_lin, b_glu, b_lin, w_down, b_down, n_blocks):
    nf = D_FF // MOE_TF
    assert nf == 2, "the kernel has a first-step and a last-step body and no middle one"

    def blk(m, na):
        return jnp.maximum(jnp.minimum(m, na[0] - 1), 0)

    def ftile(m, f, na):
        return jnp.where(m < na[0], f, nf - 1)

    def expert(m, be, na):
        return be[blk(m, na)]

    col_tile = lambda m, f, be, ns, na: (expert(m, be, na), 0, ftile(m, f, na))
    grid_spec = pltpu.PrefetchScalarGridSpec(
        num_scalar_prefetch=3,
        grid=(n_blocks, nf),
        in_specs=[pl.BlockSpec((MOE_TM * X_SUB, LANES),
                               lambda m, f, be, ns, na: (blk(jnp.where(f == nf - 1, m + 1, m), na), 0)),
                  pl.BlockSpec((1, D_MODEL, MOE_TF), col_tile),
                  pl.BlockSpec((1, D_MODEL, MOE_TF), col_tile),
                  pl.BlockSpec((1, 1, MOE_TF), col_tile),
                  pl.BlockSpec((1, 1, MOE_TF), col_tile),
                  pl.BlockSpec((1, MOE_TF, D_MODEL), lambda m, f, be, ns, na: (expert(m, be, na), ftile(m, f, na), 0)),
                  pl.BlockSpec((1, 1, D_MODEL), lambda m, f, be, ns, na: (expert(m, be, na), 0, 0))],
        out_specs=pl.BlockSpec((MOE_TM * X_SUB, LANES), lambda m, f, be, ns, na: (m, 0)),
        scratch_shapes=[pltpu.VMEM((MOE_TM, D_MODEL), BF16)] * 2
        + [pltpu.VMEM((MOE_SB, D_MODEL), F32)] * (MOE_TM // MOE_SB),
    )
    return pl.pallas_call(
        _moe_kernel,
        grid_spec=grid_spec,
        out_shape=jax.ShapeDtypeStruct((n_blocks * MOE_TM * X_SUB, LANES), U32),
        compiler_params=_cparams(("arbitrary", "arbitrary")),
        name="moe_ffn",
    )(block_expert, block_nsub, n_active, x_rows, w_glu, w_lin, b_glu, b_lin, w_down, b_down)


def _combine_kernel(dest_ref, dest_next_ref, h1_ref, selg_ref, fw_ref, rows_ref, o_ref, buf, h_scr, sem):
    tt = h1_ref.shape[0]
    i = pl.program_id(0)
    n = pl.num_programs(0)
    half = D_MODEL // 2

    def row_copy(slot, t, k, d):
        src = rows_ref.at[pl.ds(pl.multiple_of(d * X_SUB, X_SUB), X_SUB)]
        dst = buf.at[slot, k, pl.ds(pl.multiple_of(t * X_SUB, X_SUB), X_SUB)]
        return pltpu.make_async_copy(src, dst, sem.at[slot])

    def gather(slot, d_ref):
        def issue(t, carry):
            for k in range(TOP_K):
                row_copy(slot, t, k, d_ref[t * TOP_K + k]).start(priority=k % 2)
            return carry

        lax.fori_loop(0, tt, issue, 0, unroll=4)

    slot = i % 2

    @pl.when(i == 0)
    def _():
        gather(0, dest_ref)

    @pl.when(i + 1 < n)
    def _():
        gather(1 - slot, dest_next_ref)

    for k in range(TOP_K):
        pltpu.make_async_copy(rows_ref.at[pl.ds(0, tt * X_SUB)], buf.at[slot, k], sem.at[slot]).wait()

    gates = selg_ref[...]
    for sub in range(X_SUB):
        lo_cs = slice(sub * LANES, (sub + 1) * LANES)
        hi_cs = slice(half + sub * LANES, half + (sub + 1) * LANES)
        h_lo = h1_ref[:, lo_cs]
        h_hi = h1_ref[:, hi_cs]
        for k in range(TOP_K):
            u = buf[slot, k, pl.ds(sub, tt, stride=X_SUB), :]
            g = gates[:, k:k + 1]
            h_lo = h_lo + g * pltpu.bitcast(u << 16, F32)
            h_hi = h_hi + g * pltpu.bitcast(u & jnp.uint32(0xFFFF0000), F32)
        h_scr[:, lo_cs] = h_lo
        h_scr[:, hi_cs] = h_hi
    h = h_scr[...]
    o_ref[...] = h * lax.rsqrt(jnp.mean(h * h, axis=-1, keepdims=True) + EPS) * fw_ref[...]


def _combine(dest_flat, h1, selg, final_w, y_rows):
    t = h1.shape[0]
    tt = COMB_TT
    n = t // tt
    return pl.pallas_call(
        _combine_kernel,
        grid=(n,),
        in_specs=[pl.BlockSpec((tt * TOP_K,), lambda i: (i,), memory_space=pltpu.SMEM),
                  pl.BlockSpec((tt * TOP_K,), lambda i: (jnp.minimum(i + 1, n - 1),), memory_space=pltpu.SMEM),
                  pl.BlockSpec((tt, D_MODEL), lambda i: (i, 0)),
                  pl.BlockSpec((tt, LANES), lambda i: (i, 0)),
                  pl.BlockSpec((1, D_MODEL), lambda i: (0, 0)),
                  pl.BlockSpec(memory_space=pl.ANY)],
        out_specs=pl.BlockSpec((tt, D_MODEL), lambda i: (i, 0)),
        out_shape=jax.ShapeDtypeStruct((t, D_MODEL), F32),
        scratch_shapes=[pltpu.VMEM((2, TOP_K, tt * X_SUB, LANES), U32), pltpu.VMEM((tt, D_MODEL), F32),
                        pltpu.SemaphoreType.DMA((2,))],
        compiler_params=_cparams(("arbitrary",)),
        name="moe_combine",
    )(dest_flat, dest_flat, h1, selg, final_w, y_rows)


def kernel(x, norm1_w, w_in, conv_w, conv_b, dt_bias, a_log, d_skip, ssd_norm_w, i_bias, f_bias, ml_norm_w,
           w_out, norm2_w, router_w, router_b, w_gate_up, b_gate_up, w_down, b_down, final_norm_w):
    b, s, d = x.shape
    t = b * s
    assert w_in.shape[0] == 1, "single-layer problem"
    h = x.reshape(t, d)
    for l in range(1):
        wi = w_in[l]
        o_z, o_xbc, o_dt = 0, D_SSD, D_SSD + D_SSD + 2 * SSD_GROUPS * SSD_STATE
        o_q = o_dt + SSD_HEADS
        o_k = o_q + ML_HEADS * ML_DQK
        o_v = o_k + ML_HEADS * ML_DQK
        o_o = o_v + D_ML
        o_i = o_o + D_ML
        o_f = o_i + ML_HEADS
        w_big = jnp.concatenate([wi[:, o_z:o_dt], wi[:, o_q:o_i]], axis=1).astype(BF16)
        w_small = jnp.concatenate(
            [wi[:, o_dt:o_q], wi[:, o_i:o_f], wi[:, o_f:o_f + ML_HEADS],
             jnp.zeros((d, LANES - SSD_HEADS - 2 * ML_HEADS), F32)], axis=1).astype(BF16)

        proj, small = _input_projection(h, norm1_w[l].reshape(1, d), w_big, w_small)
        proj3 = proj.reshape(b, s, D_BIG)
        small3 = small.reshape(b, s, LANES)
        y_ssd = _ssd_mixer(proj3, small3, conv_w[l], conv_b[l], dt_bias[l], a_log[l], d_skip[l], ssd_norm_w[l], b, s)
        y_ml, w_glu, w_lin = _mlstm_mixer(proj3, small3, i_bias[l], f_bias[l], ml_norm_w[l], b, s, w_gate_up[l])

        rw = jnp.pad(router_w[l], ((0, 0), (0, LANES - N_EXPERTS)))
        rw_hi = rw.astype(BF16)
        rw = jnp.concatenate([rw_hi, (rw - rw_hi.astype(F32)).astype(BF16)], axis=1)
        rb = jnp.pad(router_b[l], (0, LANES - N_EXPERTS)).reshape(1, LANES)
        h1, xnp, seli, selg, cnt = _outproj_router(
            h, y_ssd.reshape(t, D_SSD), y_ml.reshape(t, D_ML), w_out[l].astype(BF16), norm2_w[l].reshape(1, d), rw, rb)

        n_blocks = (t * TOP_K) // MOE_TM + N_EXPERTS
        dest, table = _route_offsets(seli, cnt, n_blocks)
        dest_flat = dest[:, :TOP_K].reshape(-1)
        block_expert = table[:n_blocks, 0]
        n_active = table[0:1, 1]
        block_nsub = table[:n_blocks, 2]

        x_rows = _dispatch(table[:N_EXPERTS, 3], table[:N_EXPERTS, 4], table[0:1, 5], dest_flat, xnp,
                           n_blocks * MOE_TM)
        b_glu = b_gate_up[l][:, 0::2].reshape(N_EXPERTS, 1, D_FF)
        b_lin = b_gate_up[l][:, 1::2].reshape(N_EXPERTS, 1, D_FF)
        y_rows = _moe_ffn(block_expert, block_nsub, n_active, x_rows, w_glu, w_lin, b_glu, b_lin, w_down[l],
                          b_down[l].reshape(N_EXPERTS, 1, d), n_blocks)
        h = _combine(dest_flat, h1, selg, final_norm_w.reshape(1, d), y_rows)
    return h.reshape(b, s, d)
```

```python
import functools

import jax
import jax.numpy as jnp
from jax import lax
from jax.experimental import pallas as pl
from jax.experimental.pallas import tpu as pltpu

F32 = jnp.float32
BF16 = jnp.bfloat16
I32 = jnp.int32
U32 = jnp.uint32

LANES = 128
SUBLANES = 8
VMEM_LIMIT = 56 * 1024 * 1024

D_MODEL = 2048
EPS = 1e-5
SSD_HEAD_DIM = 64
SSD_HEADS = 32
D_SSD = 2048
SSD_GROUPS = 4
SSD_STATE = 128
SSD_CONV = 4
CHUNK = 128
ML_HEADS = 8
ML_DV = 256
ML_DQK = 128
D_ML = 2048
N_EXPERTS = 32
TOP_K = 4
D_FF = 2048
SWIGLU_ALPHA = 1.702
SWIGLU_LIMIT = 7.0

COL_Z, COL_X, COL_BC, COL_Q, COL_K, COL_V, COL_O = 0, 2048, 4096, 5120, 6144, 7168, 9216
COL_W = 1024
D_BIG = 11264
LANE_DT, LANE_I, LANE_F = 0, 32, 40

IN_TM, IN_TN = 1024, 1024
OUT_TM = 512
ROUTE_TM = 2048
DISP_TT = 512
MOE_TM = 512
MOE_SB = 512
MOE_TF = 1024
PERM_W = 256
PREP_TN = 1024
COMB_TT = 256
X_SUB = D_MODEL // 2 // LANES


def _cparams(sem, vmem=VMEM_LIMIT):
    return pltpu.CompilerParams(dimension_semantics=sem, vmem_limit_bytes=vmem)


def _dot(a, b):
    return jnp.dot(a, b, preferred_element_type=F32)


def _dot_exact(a, b):
    return jnp.dot(a, b, preferred_element_type=F32, precision=lax.Precision.HIGHEST)


def _split2_dot(x, e2_bf16):
    hi = x.astype(BF16)
    lo = (x - hi.astype(F32)).astype(BF16)
    return _dot(jnp.concatenate([hi, lo], axis=1), e2_bf16)


def _sigmoid(x):
    return 1.0 / (1.0 + jnp.exp(-x))


def _softplus(x):
    return jnp.maximum(x, 0.0) + jnp.log1p(jnp.exp(-jnp.abs(x)))


def _tril(n, strict=False):
    r = lax.broadcasted_iota(I32, (n, n), 0)
    c = lax.broadcasted_iota(I32, (n, n), 1)
    return (r > c) if strict else (r >= c)


def _pack_bf16_pairs(y):
    half = D_MODEL // 2
    lo = pltpu.bitcast(y[:, :half].astype(BF16).astype(F32), U32)
    hi = pltpu.bitcast(y[:, half:].astype(BF16).astype(F32), U32)
    return hi | (lo >> 16)


def _unpack_bf16_pairs(src_ref, rows, dst_ref):
    half = D_MODEL // 2
    for sub in range(X_SUB):
        u = src_ref[pl.ds(sub, rows, stride=X_SUB), :]
        dst_ref[0:rows, sub * LANES:(sub + 1) * LANES] = pltpu.bitcast(u << 16, F32).astype(BF16)
        dst_ref[0:rows, half + sub * LANES:half + (sub + 1) * LANES] = (
            pltpu.bitcast(u & jnp.uint32(0xFFFF0000), F32).astype(BF16))


def _deinterleave_tile(w_ref, p_ref, glu_ref, lin_ref):
    hw = PERM_W // 2
    for s in range(w_ref.shape[2] // PERM_W):
        t = _dot(w_ref[0, :, s * PERM_W:(s + 1) * PERM_W].astype(BF16), p_ref[...])
        glu_ref[0, :, s * hw:(s + 1) * hw] = t[:, :hw].astype(BF16)
        lin_ref[0, :, s * hw:(s + 1) * hw] = t[:, hw:].astype(BF16)


def _deinterleave_perm():
    src = jnp.arange(PERM_W)
    dst = jnp.where(src % 2 == 0, src // 2, PERM_W // 2 + src // 2)
    return (dst[:, None] == jnp.arange(PERM_W)[None, :]).astype(BF16)


def _deinterleave_specs(w_gate_up, tile_index):
    e, k, n = w_gate_up.shape

    def at(*ids):
        ei, ci = tile_index(*ids)
        return ei, 0, ci

    in_specs = [pl.BlockSpec((1, k, PREP_TN), at), pl.BlockSpec((PERM_W, PERM_W), lambda *ids: (0, 0))]
    out_specs = [pl.BlockSpec((1, k, PREP_TN // 2), at)] * 2
    out_shape = [jax.ShapeDtypeStruct((e, k, n // 2), BF16)] * 2
    return in_specs, out_specs, out_shape


def _deinterleave_gate_up(w_gate_up):
    e, _, n = w_gate_up.shape
    in_specs, out_specs, out_shape = _deinterleave_specs(w_gate_up, lambda i, j: (i, j))
    return pl.pallas_call(
        _deinterleave_tile,
        grid=(e, n // PREP_TN),
        in_specs=in_specs,
        out_specs=out_specs,
        out_shape=out_shape,
        compiler_params=_cparams(("arbitrary", "arbitrary")),
        name="moe_deinterleave",
    )(w_gate_up, _deinterleave_perm())


def _inproj_kernel(x_ref, nw_ref, w_ref, ws_ref, o_ref, os_ref, n_scr):
    @pl.when(pl.program_id(1) == 0)
    def _():
        x = x_ref[...]
        n = x * lax.rsqrt(jnp.mean(x * x, axis=-1, keepdims=True) + EPS) * nw_ref[...]
        nb = n.astype(BF16)
        n_scr[...] = nb
        os_ref[...] = _dot(nb, ws_ref[...])

    o_ref[...] = _dot(n_scr[...], w_ref[...]).astype(BF16)


def _input_projection(xf, norm_w, w_big, w_small):
    t = xf.shape[0]
    return pl.pallas_call(
        _inproj_kernel,
        grid=(t // IN_TM, D_BIG // IN_TN),
        in_specs=[pl.BlockSpec((IN_TM, D_MODEL), lambda i, j: (i, 0)),
                  pl.BlockSpec((1, D_MODEL), lambda i, j: (0, 0)),
                  pl.BlockSpec((D_MODEL, IN_TN), lambda i, j: (0, j)),
                  pl.BlockSpec((D_MODEL, LANES), lambda i, j: (0, 0))],
        out_specs=[pl.BlockSpec((IN_TM, IN_TN), lambda i, j: (i, j)),
                   pl.BlockSpec((IN_TM, LANES), lambda i, j: (i, 0))],
        out_shape=[jax.ShapeDtypeStruct((t, D_BIG), BF16),
                   jax.ShapeDtypeStruct((t, LANES), F32)],
        scratch_shapes=[pltpu.VMEM((IN_TM, D_MODEL), BF16)],
        compiler_params=_cparams(("arbitrary", "arbitrary")),
        name="inproj",
    )(xf, norm_w, w_big, w_small)


def _ssd_kernel(z_ref, x_ref, bc_ref, sm_ref, cw_ref, cb_ref, shift_ref, dtb_ref, alog_ref,
                dskip_ref, nw_ref, e_ref, o_ref, ext, y_scr, *states):
    q = CHUNK

    @pl.when(pl.program_id(1) == 0)
    def _():
        ext[0:q, :] = jnp.zeros((q, ext.shape[1]), BF16)
        for state in states:
            state[...] = jnp.zeros_like(state)

    ext[q:2 * q, 0:D_SSD] = x_ref[0]
    ext[q:2 * q, D_SSD:] = bc_ref[0]
    shifted = _dot(shift_ref[...], ext[...])
    acc = cb_ref[...] + ext[q:2 * q, :].astype(F32) * cw_ref[SSD_CONV - 1:SSD_CONV, :]
    for j in range(SSD_CONV - 1):
        acc = acc + shifted[j * q:(j + 1) * q, :] * cw_ref[j:j + 1, :]
    xbc = acc * _sigmoid(acc)
    xs = xbc[:, :D_SSD]
    bcs = xbc[:, D_SSD:]
    ext[0:q, :] = ext[q:2 * q, :]

    lane = lax.broadcasted_iota(I32, (1, LANES), 1)
    g = sm_ref[0]
    dt = _softplus(g + dtb_ref[...])
    a = jnp.where(lane < SSD_HEADS, -jnp.exp(alog_ref[...]), 0.0)
    da = dt * a
    tri = _tril(q).astype(F32)
    a_cum = _dot_exact(tri, da)
    a_cum_t = a_cum.T
    a_end = a_cum[q - 1:q, :]
    stack = jnp.concatenate(
        [dt, jnp.exp(a_end - a_cum), jnp.exp(a_cum), jnp.broadcast_to(jnp.exp(a_end), (SUBLANES, LANES))], axis=0)
    ex = _split2_dot(stack, e_ref[...])
    dt_x = ex[0:q]
    to_end_x = ex[q:2 * q]
    from_start_x = ex[2 * q:3 * q]
    chunk_decay_x = ex[3 * q:3 * q + 1]

    xdt = xs * dt_x
    xdt_b = xdt.astype(BF16)
    xde_b = (xdt * to_end_x).astype(BF16)
    causal = _tril(q)
    lane_q = lax.broadcasted_iota(I32, (1, LANES), 1)
    gw = SSD_STATE
    hpg = SSD_HEADS // SSD_GROUPS
    cpg = D_SSD // SSD_GROUPS
    for gi in range(SSD_GROUPS):
        bg = bcs[:, gi * gw:(gi + 1) * gw]
        cg_b = bcs[:, SSD_GROUPS * gw + gi * gw:SSD_GROUPS * gw + (gi + 1) * gw].astype(BF16)
        cb = lax.dot_general(cg_b, bg.astype(BF16), (((1,), (1,)), ((), ())), preferred_element_type=F32)
        bg_t = bg.T.astype(BF16)
        cs = slice(gi * cpg, (gi + 1) * cpg)
        prev = states[gi][...]
        y_off = _dot(cg_b, prev.astype(BF16)) * from_start_x[:, cs]
        states[gi][...] = prev * chunk_decay_x[:, cs] + _dot(bg_t, xde_b[:, cs])
        for pr in range(hpg // 2):
            c0 = gi * cpg + pr * LANES
            xp = xdt_b[:, c0:c0 + LANES]
            acc = y_off[:, pr * LANES:(pr + 1) * LANES]
            for sub in range(2):
                h = gi * hpg + pr * 2 + sub
                seg = a_cum[:, h:h + 1] - a_cum_t[h:h + 1, :]
                decay = jnp.exp(jnp.where(causal, seg, -jnp.inf))
                m = (cb * decay).astype(BF16)
                keep = (lane_q < SSD_HEAD_DIM) if sub == 0 else (lane_q >= SSD_HEAD_DIM)
                acc = acc + _dot(m, jnp.where(keep, xp, jnp.zeros_like(xp)))
            y_scr[:, c0:c0 + LANES] = acc

    y = y_scr[...] + xs * dskip_ref[...]
    zf = z_ref[0].astype(F32)
    y = y * (zf * _sigmoid(zf))
    for gi in range(SSD_GROUPS):
        cs = slice(gi * cpg, (gi + 1) * cpg)
        yg = y[:, cs]
        yg = yg * lax.rsqrt(jnp.mean(yg * yg, axis=-1, keepdims=True) + EPS)
        o_ref[0, :, cs] = (yg * nw_ref[:, cs]).astype(BF16)


def _ssd_mixer(proj, small, conv_w, conv_b, dt_bias, a_log, d_skip, norm_w, b, s):
    nc = s // CHUNK
    pad = LANES - SSD_HEADS
    dtb = jnp.pad(dt_bias, (0, pad)).reshape(1, LANES)
    alog = jnp.pad(a_log, (0, pad)).reshape(1, LANES)
    dskip = jnp.repeat(d_skip, SSD_HEAD_DIM).reshape(1, D_SSD)
    expand = (jnp.arange(LANES)[:, None] == (jnp.arange(D_SSD) // SSD_HEAD_DIM)[None, :]).astype(BF16)
    expand = jnp.concatenate([expand, expand], axis=0)
    d_bc = 2 * SSD_GROUPS * SSD_STATE
    d_conv = D_SSD + d_bc
    tt = jnp.arange((SSD_CONV - 1) * CHUNK)
    back = (SSD_CONV - 1) - tt // CHUNK
    shift = ((CHUNK + tt % CHUNK - back)[:, None] == jnp.arange(2 * CHUNK)[None, :]).astype(BF16)
    const = lambda shape: pl.BlockSpec(shape, lambda i, j: (0,) * len(shape))
    col = lambda width, off: pl.BlockSpec((1, CHUNK, width), lambda i, j: (i, j, off // width))
    return pl.pallas_call(
        _ssd_kernel,
        grid=(b, nc),
        in_specs=[col(D_SSD, COL_Z), col(D_SSD, COL_X), col(d_bc, COL_BC), col(LANES, 0),
                  const((SSD_CONV, d_conv)), const((1, d_conv)), const(((SSD_CONV - 1) * CHUNK, 2 * CHUNK)),
                  const((1, LANES)), const((1, LANES)), const((1, D_SSD)), const((1, D_SSD)),
                  const((2 * LANES, D_SSD))],
        out_specs=pl.BlockSpec((1, CHUNK, D_SSD), lambda i, j: (i, j, 0)),
        out_shape=jax.ShapeDtypeStruct((b, s, D_SSD), BF16),
        scratch_shapes=[pltpu.VMEM((2 * CHUNK, d_conv), BF16),
                        pltpu.VMEM((CHUNK, D_SSD), F32)]
        + [pltpu.VMEM((SSD_STATE, D_SSD // SSD_GROUPS), F32)] * SSD_GROUPS,
        compiler_params=_cparams(("arbitrary", "arbitrary")),
        name="ssd_mixer",
    )(proj, proj, proj, small, conv_w, conv_b.reshape(1, -1), shift, dtb, alog, dskip, norm_w.reshape(1, -1), expand)


def _mlstm_kernel(q_ref, k_ref, v0_ref, v1_ref, o0_ref, o1_ref, sm_ref, bias_ref, nw_ref, *rest, prep_gate_up):
    L = CHUNK
    heads_per_ref = COL_W // ML_DV

    def head_cols(refs, h):
        c0 = (h % heads_per_ref) * ML_DV
        return refs[h // heads_per_ref][0, :, c0:c0 + ML_DV]

    if prep_gate_up:
        wg_ref, perm_ref, out_ref, glu_ref, lin_ref, n_scr, m_scr, *ct_scrs = rest
        _deinterleave_tile(wg_ref, perm_ref, glu_ref, lin_ref)
    else:
        out_ref, n_scr, m_scr, *ct_scrs = rest

    @pl.when(pl.program_id(1) == 0)
    def _():
        for ct_scr in ct_scrs:
            ct_scr[...] = jnp.zeros_like(ct_scr)
        n_scr[...] = jnp.zeros_like(n_scr)
        m_scr[...] = jnp.zeros_like(m_scr)

    n_all = n_scr[...]
    m_all = m_scr[...]
    n_new_all = n_all
    m_new_all = m_all
    head_row = lax.broadcasted_iota(I32, (ML_HEADS, LANES), 0)

    g = sm_ref[0] + bias_ref[...]
    log_f = jnp.minimum(g, 0.0) - jnp.log1p(jnp.exp(-jnp.abs(g)))
    tri = _tril(L).astype(F32)
    bcum = _dot_exact(tri, log_f)
    bcum_t = bcum.T
    keys_before = jnp.logical_not(_tril(L, strict=True))
    row16 = lax.broadcasted_iota(I32, (2 * SUBLANES, ML_DQK), 0)
    scale = ML_DQK ** -0.5
    nt = (((1,), (1,)), ((), ()))
    tn = (((0,), (0,)), ((), ()))
    for h in range(ML_HEADS):
        li, lf = LANE_I + h, LANE_F + h
        li_col, b_col, b_row = g[:, li:li + 1], bcum[:, lf:lf + 1], bcum_t[lf:lf + 1, :]
        btot = bcum[L - 1:L, lf:lf + 1]
        m_prev = m_all[h:h + 1, 0:1]
        ct_prev = ct_scrs[h][...]
        n_prev = n_all[h:h + 1, :]
        qb = q_ref[0, :, h * ML_DQK:(h + 1) * ML_DQK]
        kb = k_ref[0, :, h * ML_DQK:(h + 1) * ML_DQK]
        vb = head_cols((v0_ref, v1_ref), h)
        w_end = btot - b_col + li_col
        m_loc = jnp.max(w_end, axis=0, keepdims=True)
        k_end = kb.astype(F32) * jnp.exp(w_end - m_loc)
        kv_t = lax.dot_general(vb, k_end.astype(BF16), tn, preferred_element_type=F32)
        nk = jnp.sum(k_end, axis=0, keepdims=True)
        d_log = jnp.where(keys_before, (li_col - b_col) + b_row, -jnp.inf)
        m_inter = b_row + m_prev
        m_t = jnp.maximum(m_inter, jnp.max(d_log, axis=0, keepdims=True))
        s_t = lax.dot_general(kb, qb, nt, preferred_element_type=F32)
        s_t = s_t * (scale * jnp.exp(d_log - m_t))
        inter = scale * jnp.exp(m_inter - m_t)
        num_t = (lax.dot_general(vb, s_t.astype(BF16), tn, preferred_element_type=F32)
                 + lax.dot_general(ct_prev.astype(BF16), qb, nt, preferred_element_type=F32) * inter)
        n_hi = n_prev.astype(BF16).astype(F32)
        n_rows = jnp.where(row16 == 0, n_hi, jnp.where(row16 == 1, n_prev - n_hi, 0.0)).astype(BF16)
        qn = lax.dot_general(n_rows, qb, nt, preferred_element_type=F32)
        den = jnp.sum(s_t, axis=0, keepdims=True) + (qn[0:1, :] + qn[1:2, :]) * inter
        den = jnp.maximum(jnp.abs(den), jnp.exp(-m_t))
        ht_t = num_t / den
        ht_t = ht_t * lax.rsqrt(jnp.mean(ht_t * ht_t, axis=0, keepdims=True) + EPS)
        vs = slice(h * ML_DV, (h + 1) * ML_DV)
        og = _sigmoid(head_cols((o0_ref, o1_ref), h).astype(F32))
        out_ref[0, :, vs] = (ht_t.T * nw_ref[:, vs] * og).astype(BF16)
        m_new = jnp.maximum(btot + m_prev, m_loc)
        fa = jnp.exp(btot + m_prev - m_new)
        fg = jnp.exp(m_loc - m_new)
        ct_scrs[h][...] = fa * ct_prev + fg * kv_t
        n_new_all = jnp.where(head_row == h, fa * n_prev + fg * nk, n_new_all)
        m_new_all = jnp.where(head_row == h, m_new, m_new_all)
    n_scr[...] = n_new_all
    m_scr[...] = m_new_all


def _mlstm_mixer(proj, small, i_bias, f_bias, norm_w, b, s, w_gate_up):
    nc = s // CHUNK
    bias = jnp.zeros((LANES,), F32).at[LANE_I:LANE_I + ML_HEADS].set(i_bias).at[LANE_F:LANE_F + ML_HEADS].set(f_bias)
    d_qk = ML_HEADS * ML_DQK
    const = lambda shape: pl.BlockSpec(shape, lambda i, j: (0,) * len(shape))
    col = lambda width, off: pl.BlockSpec((1, CHUNK, width), lambda i, j: (i, j, off // width))
    e, _, n = w_gate_up.shape
    tiles_per_expert = n // PREP_TN
    prep = b * nc == e * tiles_per_expert
    in_specs = [col(d_qk, COL_Q), col(d_qk, COL_K), col(COL_W, COL_V), col(COL_W, COL_V + COL_W),
                col(COL_W, COL_O), col(COL_W, COL_O + COL_W), col(LANES, 0), const((1, LANES)), const((1, D_ML))]
    out_specs = [pl.BlockSpec((1, CHUNK, D_ML), lambda i, j: (i, j, 0))]
    out_shape = [jax.ShapeDtypeStruct((b, s, D_ML), BF16)]
    args = [proj] * 6 + [small, bias.reshape(1, LANES), norm_w.reshape(1, -1)]
    if prep:
        w_in_specs, w_out_specs, w_out_shape = _deinterleave_specs(
            w_gate_up, lambda i, j: ((i * nc + j) // tiles_per_expert, (i * nc + j) % tiles_per_expert))
        in_specs += w_in_specs
        out_specs += w_out_specs
        out_shape += w_out_shape
        args += [w_gate_up, _deinterleave_perm()]
    outs = pl.pallas_call(
        functools.partial(_mlstm_kernel, prep_gate_up=prep),
        grid=(b, nc),
        in_specs=in_specs,
        out_specs=out_specs,
        out_shape=out_shape,
        scratch_shapes=[pltpu.VMEM((ML_HEADS, ML_DQK), F32), pltpu.VMEM((ML_HEADS, LANES), F32)]
        + [pltpu.VMEM((ML_DV, ML_DQK), F32)] * ML_HEADS,
        compiler_params=_cparams(("arbitrary", "arbitrary")),
        name="mlstm_mixer",
    )(*args)
    if prep:
        return outs[0], outs[1], outs[2]
    return (outs[0], *_deinterleave_gate_up(w_gate_up))


def _outproj_router_kernel(x_ref, ys_ref, ym_ref, wo_ref, n2w_ref, rw_ref, rb_ref,
                           h1_ref, xnp_ref, seli_ref, selg_ref, cnt_ref, carry):
    tm = x_ref.shape[0]

    @pl.when(pl.program_id(0) == 0)
    def _():
        carry[...] = jnp.zeros_like(carry)

    h1 = x_ref[...] + _dot(ys_ref[...], wo_ref[0:D_SSD, :]) + _dot(ym_ref[...], wo_ref[D_SSD:, :])
    h1_ref[...] = h1
    xn = h1 * lax.rsqrt(jnp.mean(h1 * h1, axis=-1, keepdims=True) + EPS) * n2w_ref[...]
    packed = _pack_bf16_pairs(xn)
    for sub in range(X_SUB):
        xnp_ref[pl.ds(sub, tm, stride=X_SUB), :] = packed[:, sub * LANES:(sub + 1) * LANES]

    lane = lax.broadcasted_iota(I32, (tm, LANES), 1)
    lane_f = lane.astype(F32)
    xn_hi = xn.astype(BF16)
    xn_lo = (xn - xn_hi.astype(F32)).astype(BF16)
    hh_hl = _dot(xn_hi, rw_ref[...])
    logits = hh_hl[:, :LANES] + hh_hl[:, LANES:] + _dot(xn_lo, rw_ref[:, :LANES]) + rb_ref[...]
    work = jnp.where(lane < N_EXPERTS, logits, -jnp.inf)
    vals, idxs = [], []
    chosen = jnp.zeros((tm, LANES), jnp.bool_)
    for _ in range(TOP_K):
        m = jnp.max(work, axis=1, keepdims=True)
        idx = jnp.min(jnp.where(work == m, lane_f, float(LANES)), axis=1, keepdims=True)
        sel = lane_f == idx
        vals.append(m)
        idxs.append(idx)
        chosen = jnp.logical_or(chosen, sel)
        work = jnp.where(sel, -jnp.inf, work)
    exps = [jnp.exp(v - vals[0]) for v in vals]
    denom = exps[0] + exps[1] + exps[2] + exps[3]

    onehot = jnp.where(chosen, 1.0, 0.0)
    ranks = _dot(_tril(tm, strict=True).astype(BF16), onehot.astype(BF16)) + carry[...]
    total = carry[...] + jnp.sum(onehot, axis=0, keepdims=True)
    carry[...] = total
    cnt_ref[...] = jnp.broadcast_to(total, cnt_ref.shape)

    seli = jnp.zeros((tm, LANES), F32)
    selg = jnp.zeros((tm, LANES), F32)
    for k in range(TOP_K):
        rank_k = jnp.sum(jnp.where(lane_f == idxs[k], ranks, 0.0), axis=1, keepdims=True)
        seli = jnp.where(lane == k, idxs[k], seli)
        seli = jnp.where(lane == TOP_K + k, rank_k, seli)
        selg = jnp.where(lane == k, exps[k] / denom, selg)
    seli_ref[...] = seli.astype(I32)
    selg_ref[...] = selg


def _outproj_router(xf, y_ssd, y_ml, w_out_b, norm2_w, router_w_pad, router_b_pad):
    t = xf.shape[0]
    tm = OUT_TM
    row = lambda width: pl.BlockSpec((tm, width), lambda i: (i, 0))
    const = lambda shape: pl.BlockSpec(shape, lambda i: (0,) * len(shape))
    return pl.pallas_call(
        _outproj_router_kernel,
        grid=(t // tm,),
        in_specs=[row(D_MODEL), row(D_SSD), row(D_ML),
                  pl.BlockSpec((D_SSD + D_ML, D_MODEL), lambda i: (0, 0), pipeline_mode=pl.Buffered(1)),
                  const((1, D_MODEL)),
                  const((D_MODEL, 2 * LANES)), const((1, LANES))],
        out_specs=[row(D_MODEL), pl.BlockSpec((tm * X_SUB, LANES), lambda i: (i, 0)), row(LANES), row(LANES),
                   const((SUBLANES, LANES))],
        out_shape=[jax.ShapeDtypeStruct((t, D_MODEL), F32),
                   jax.ShapeDtypeStruct((t * X_SUB, LANES), U32),
                   jax.ShapeDtypeStruct((t, LANES), I32),
                   jax.ShapeDtypeStruct((t, LANES), F32),
                   jax.ShapeDtypeStruct((SUBLANES, LANES), F32)],
        scratch_shapes=[pltpu.VMEM((1, LANES), F32)],
        compiler_params=_cparams(("arbitrary",)),
        name="outproj_router",
    )(xf, y_ssd, y_ml, w_out_b, norm2_w, router_w_pad, router_b_pad)


def _route_kernel(seli_ref, cnt_ref, dest_ref, be_ref):
    tm = seli_ref.shape[0]
    nbp = be_ref.shape[0]
    lane1 = lax.broadcasted_iota(I32, (1, LANES), 1)
    cnt = jnp.where(lane1 < N_EXPERTS, cnt_ref[0:1, :], 0.0)
    padded = jnp.ceil(cnt / MOE_TM) * MOE_TM
    r = lax.broadcasted_iota(I32, (LANES, LANES), 0)
    c = lax.broadcasted_iota(I32, (LANES, LANES), 1)
    upper = jnp.where(r <= c, 1.0, 0.0)
    pend = jnp.round(_dot_exact(jnp.broadcast_to(padded, (SUBLANES, LANES)), upper))[0:1, :]
    pstart = pend - padded

    lane = lax.broadcasted_iota(I32, (tm, LANES), 1)
    lane_f = lane.astype(F32)
    seli = seli_ref[...].astype(F32)
    dest = jnp.zeros((tm, LANES), F32)
    for k in range(TOP_K):
        idx_k = jnp.sum(jnp.where(lane == k, seli, 0.0), axis=1, keepdims=True)
        rank_k = jnp.sum(jnp.where(lane == TOP_K + k, seli, 0.0), axis=1, keepdims=True)
        start_k = jnp.sum(jnp.where(lane_f == idx_k, pstart, 0.0), axis=1, keepdims=True)
        dest = jnp.where(lane == k, start_k + rank_k, dest)
    dest_ref[...] = dest.astype(I32)

    blk_start = (lax.broadcasted_iota(I32, (nbp, LANES), 0) * MOE_TM).astype(F32)
    lane_b = lax.broadcasted_iota(I32, (nbp, LANES), 1)
    passed = jnp.where(jnp.logical_and(lane_b < N_EXPERTS, pend <= blk_start), 1.0, 0.0)
    expert = jnp.minimum(jnp.sum(passed, axis=1, keepdims=True), float(N_EXPERTS - 1))
    n_active = pend[:, N_EXPERTS - 1:N_EXPERTS] / MOE_TM
    tok_end = jnp.sum(jnp.where(lane_b.astype(F32) == expert, pstart + cnt, 0.0), axis=1, keepdims=True)
    valid = jnp.clip(tok_end - blk_start[:, 0:1], 0.0, float(MOE_TM))
    n_sub = jnp.ceil(valid / MOE_SB)
    own = lane_b == lax.broadcasted_iota(I32, (nbp, LANES), 0)
    pad_start = jnp.sum(jnp.where(own, pstart + cnt, 0.0), axis=1, keepdims=True)
    pad_cnt = jnp.sum(jnp.where(own, padded - cnt, 0.0), axis=1, keepdims=True)
    cols = (expert, n_active, n_sub, pad_start, pad_cnt, pend[:, N_EXPERTS - 1:N_EXPERTS])
    be = jnp.zeros((nbp, LANES), F32)
    for j, v in enumerate(cols):
        be = jnp.where(lane_b == j, v, be)
    be_ref[...] = be.astype(I32)


def _route_offsets(seli, cnt, n_blocks):
    t = seli.shape[0]
    nbp = -(-n_blocks // SUBLANES) * SUBLANES
    return pl.pallas_call(
        _route_kernel,
        grid=(t // ROUTE_TM,),
        in_specs=[pl.BlockSpec((ROUTE_TM, LANES), lambda i: (i, 0)),
                  pl.BlockSpec((SUBLANES, LANES), lambda i: (0, 0))],
        out_specs=[pl.BlockSpec((ROUTE_TM, LANES), lambda i: (i, 0)),
                   pl.BlockSpec((nbp, LANES), lambda i: (0, 0))],
        out_shape=[jax.ShapeDtypeStruct((t, LANES), I32),
                   jax.ShapeDtypeStruct((nbp, LANES), I32)],
        compiler_params=_cparams(("arbitrary",)),
        name="route_offsets",
    )(seli, cnt)


def _dispatch_kernel(pad_start_ref, pad_cnt_ref, tail_ref, dest_ref, xnp_ref, rows_ref, zbuf, sem, zsem, *, n_zero):
    tt = xnp_ref.shape[0] // X_SUB
    n_rows = rows_ref.shape[0] // X_SUB
    first = pl.program_id(0) == 0

    def zero_rows(start_row, n):
        dst = rows_ref.at[pl.ds(pl.multiple_of(start_row * X_SUB, X_SUB), n * X_SUB)]
        return pltpu.make_async_copy(zbuf.at[pl.ds(0, n * X_SUB)], dst, zsem)

    @pl.when(first)
    def _():
        zbuf[...] = jnp.zeros_like(zbuf)

        def expert_padding(e, carry):
            start, count = pad_start_ref[e], pad_cnt_ref[e]
            for k in reversed(range(MOE_TM.bit_length() - 1)):
                run = 1 << k
                pl.when((count & run) != 0)(
                    lambda run=run: zero_rows(start + (count & ~(2 * run - 1)), run).start())
            return carry

        lax.fori_loop(0, N_EXPERTS, expert_padding, 0)

        def trailing_block(j, carry):
            start = tail_ref[0] + j * MOE_TM
            pl.when(start < n_rows)(lambda: zero_rows(start, MOE_TM).start())
            return carry

        lax.fori_loop(0, N_EXPERTS, trailing_block, 0)

    def tile_copy(t, d):
        src = xnp_ref.at[pl.ds(pl.multiple_of(t * X_SUB, X_SUB), X_SUB)]
        dst = rows_ref.at[pl.ds(pl.multiple_of(d * X_SUB, X_SUB), X_SUB)]
        return pltpu.make_async_copy(src, dst, sem)

    def issue(t, carry):
        for k in range(TOP_K):
            tile_copy(t, dest_ref[t * TOP_K + k]).start(priority=k % 2)
        return carry

    lax.fori_loop(0, tt, issue, 0, unroll=4)
    for _ in range(TOP_K):
        pltpu.make_async_copy(xnp_ref, rows_ref.at[pl.ds(0, tt * X_SUB)], sem).wait()

    @pl.when(first)
    def _():
        span = rows_ref.at[pl.ds(0, n_zero * X_SUB)]
        pltpu.make_async_copy(span, span, zsem).wait()


def _dispatch(pad_start, pad_cnt, tail_start, dest_flat, xnp, n_rows):
    t = xnp.shape[0] // X_SUB
    grid_spec = pltpu.PrefetchScalarGridSpec(
        num_scalar_prefetch=3,
        grid=(t // DISP_TT,),
        in_specs=[pl.BlockSpec((DISP_TT * TOP_K,), lambda i, *_: (i,), memory_space=pltpu.SMEM),
                  pl.BlockSpec((DISP_TT * X_SUB, LANES), lambda i, *_: (i, 0))],
        out_specs=pl.BlockSpec(memory_space=pl.ANY),
        scratch_shapes=[pltpu.VMEM((MOE_TM * X_SUB, LANES), U32), pltpu.SemaphoreType.DMA(()),
                        pltpu.SemaphoreType.DMA(())],
    )
    return pl.pallas_call(
        functools.partial(_dispatch_kernel, n_zero=n_rows - t * TOP_K),
        grid_spec=grid_spec,
        out_shape=jax.ShapeDtypeStruct((n_rows * X_SUB, LANES), U32),
        compiler_params=_cparams(("arbitrary",)),
        name="moe_dispatch",
    )(pad_start, pad_cnt, tail_start, dest_flat, xnp)


def _moe_kernel(be_ref, nsb_ref, na_ref, x_ref, wglu_ref, wlin_ref, bglu_ref, blin_ref, wd_ref, bd_ref, o_ref,
                xb_next, xb_cur, *accs):
    m = pl.program_id(0)
    f = pl.program_id(1)
    nf = pl.num_programs(1)
    n_sub = MOE_TM // MOE_SB

    def ffn(nsb, first, last):
        rows = nsb * MOE_SB
        if first:
            @pl.when(m == 0)
            def _():
                _unpack_bf16_pairs(x_ref, MOE_TM, xb_next)

        xb = xb_next if first else xb_cur
        if last:
            _unpack_bf16_pairs(x_ref, MOE_TM, xb_next)
        wd = wd_ref[0].astype(BF16)
        for i in range(nsb):
            rs = slice(i * MOE_SB, (i + 1) * MOE_SB)
            xi = xb[rs, :]
            if first:
                xb_cur[rs, :] = xi
            glu = jnp.minimum(_dot(xi, wglu_ref[0]) + bglu_ref[0], SWIGLU_LIMIT)
            lin = jnp.clip(_dot(xi, wlin_ref[0]) + blin_ref[0], -SWIGLU_LIMIT, SWIGLU_LIMIT)
            act = glu * _sigmoid(SWIGLU_ALPHA * glu) * (lin + 1.0)
            part = _dot(act.astype(BF16), wd)
            accs[i][...] = (bd_ref[0] + part) if first else (accs[i][...] + part)

        if last:
            for i in range(nsb):
                packed = _pack_bf16_pairs(accs[i][...])
                for sub in range(X_SUB):
                    o_ref[pl.ds(i * MOE_SB * X_SUB + sub, MOE_SB, stride=X_SUB), :] = (
                        packed[:, sub * LANES:(sub + 1) * LANES])
            if nsb < n_sub:
                o_ref[rows * X_SUB:, :] = jnp.zeros(((MOE_TM - rows) * X_SUB, LANES), U32)

    active = m < na_ref[0]
    kinds = [(True, False, f == 0), (False, True, f == nf - 1)]
    for nsb in range(1, n_sub + 1):
        for first, last, when in kinds:
            cond = jnp.logical_and(jnp.logical_and(active, nsb_ref[m] == nsb), when)
            pl.when(cond)(functools.partial(ffn, nsb, first, last))

    @pl.when(jnp.logical_and(jnp.logical_not(active), f == nf - 1))
    def _():
        o_ref[...] = jnp.zeros_like(o_ref)


def _moe_ffn(block_expert, block_nsub, n_active, x_rows, w_glu, w_lin, b_glu, b_lin, w_down, b_down, n_blocks):
    nf = D_FF // MOE_TF
    assert nf == 2, "the kernel has a first-step and a last-step body and no middle one"

    def blk(m, na):
        return jnp.maximum(jnp.minimum(m, na[0] - 1), 0)

    def ftile(m, f, na):
        return jnp.where(m < na[0], f, nf - 1)

    def expert(m, be, na):
        return be[blk(m, na)]

    col_tile = lambda m, f, be, ns, na: (expert(m, be, na), 0, ftile(m, f, na))
    grid_spec = pltpu.PrefetchScalarGridSpec(
        num_scalar_prefetch=3,
        grid=(n_blocks, nf),
        in_specs=[pl.BlockSpec((MOE_TM * X_SUB, LANES),
                               lambda m, f, be, ns, na: (blk(jnp.where(f == nf - 1, m + 1, m), na), 0)),
                  pl.BlockSpec((1, D_MODEL, MOE_TF), col_tile),
                  pl.BlockSpec((1, D_MODEL, MOE_TF), col_tile),
                  pl.BlockSpec((1, 1, MOE_TF), col_tile),
                  pl.BlockSpec((1, 1, MOE_TF), col_tile),
                  pl.BlockSpec((1, MOE_TF, D_MODEL), lambda m, f, be, ns, na: (expert(m, be, na), ftile(m, f, na), 0)),
                  pl.BlockSpec((1, 1, D_MODEL), lambda m, f, be, ns, na: (expert(m, be, na), 0, 0))],
        out_specs=pl.BlockSpec((MOE_TM * X_SUB, LANES), lambda m, f, be, ns, na: (m, 0)),
        scratch_shapes=[pltpu.VMEM((MOE_TM, D_MODEL), BF16)] * 2
        + [pltpu.VMEM((MOE_SB, D_MODEL), F32)] * (MOE_TM // MOE_SB),
    )
    return pl.pallas_call(
        _moe_kernel,
        grid_spec=grid_spec,
        out_shape=jax.ShapeDtypeStruct((n_blocks * MOE_TM * X_SUB, LANES), U32),
        compiler_params=_cparams(("arbitrary", "arbitrary")),
        name="moe_ffn",
    )(block_expert, block_nsub, n_active, x_rows, w_glu, w_lin, b_glu, b_lin, w_down, b_down)


def _combine_kernel(dest_ref, dest_next_ref, h1_ref, selg_ref, fw_ref, rows_ref, o_ref, buf, h_scr, sem):
    tt = h1_ref.shape[0]
    i = pl.program_id(0)
    n = pl.num_programs(0)
    half = D_MODEL // 2

    def row_copy(slot, t, k, d):
        src = rows_ref.at[pl.ds(pl.multiple_of(d * X_SUB, X_SUB), X_SUB)]
        dst = buf.at[slot, k, pl.ds(pl.multiple_of(t * X_SUB, X_SUB), X_SUB)]
        return pltpu.make_async_copy(src, dst, sem.at[slot])

    def gather(slot, d_ref):
        def issue(t, carry):
            for k in range(TOP_K):
                row_copy(slot, t, k, d_ref[t * TOP_K + k]).start(priority=k % 2)
            return carry

        lax.fori_loop(0, tt, issue, 0, unroll=4)

    slot = i % 2

    @pl.when(i == 0)
    def _():
        gather(0, dest_ref)

    @pl.when(i + 1 < n)
    def _():
        gather(1 - slot, dest_next_ref)

    for k in range(TOP_K):
        pltpu.make_async_copy(rows_ref.at[pl.ds(0, tt * X_SUB)], buf.at[slot, k], sem.at[slot]).wait()

    gates = selg_ref[...]
    for sub in range(X_SUB):
        lo_cs = slice(sub * LANES, (sub + 1) * LANES)
        hi_cs = slice(half + sub * LANES, half + (sub + 1) * LANES)
        h_lo = h1_ref[:, lo_cs]
        h_hi = h1_ref[:, hi_cs]
        for k in range(TOP_K):
            u = buf[slot, k, pl.ds(sub, tt, stride=X_SUB), :]
            g = gates[:, k:k + 1]
            h_lo = h_lo + g * pltpu.bitcast(u << 16, F32)
            h_hi = h_hi + g * pltpu.bitcast(u & jnp.uint32(0xFFFF0000), F32)
        h_scr[:, lo_cs] = h_lo
        h_scr[:, hi_cs] = h_hi
    h = h_scr[...]
    o_ref[...] = h * lax.rsqrt(jnp.mean(h * h, axis=-1, keepdims=True) + EPS) * fw_ref[...]


def _combine(dest_flat, h1, selg, final_w, y_rows):
    t = h1.shape[0]
    tt = COMB_TT
    n = t // tt
    return pl.pallas_call(
        _combine_kernel,
        grid=(n,),
        in_specs=[pl.BlockSpec((tt * TOP_K,), lambda i: (i,), memory_space=pltpu.SMEM),
                  pl.BlockSpec((tt * TOP_K,), lambda i: (jnp.minimum(i + 1, n - 1),), memory_space=pltpu.SMEM),
                  pl.BlockSpec((tt, D_MODEL), lambda i: (i, 0)),
                  pl.BlockSpec((tt, LANES), lambda i: (i, 0)),
                  pl.BlockSpec((1, D_MODEL), lambda i: (0, 0)),
                  pl.BlockSpec(memory_space=pl.ANY)],
        out_specs=pl.BlockSpec((tt, D_MODEL), lambda i: (i, 0)),
        out_shape=jax.ShapeDtypeStruct((t, D_MODEL), F32),
        scratch_shapes=[pltpu.VMEM((2, TOP_K, tt * X_SUB, LANES), U32), pltpu.VMEM((tt, D_MODEL), F32),
                        pltpu.SemaphoreType.DMA((2,))],
        compiler_params=_cparams(("arbitrary",)),
        name="moe_combine",
    )(dest_flat, dest_flat, h1, selg, final_w, y_rows)


def kernel(x, norm1_w, w_in, conv_w, conv_b, dt_bias, a_log, d_skip, ssd_norm_w, i_bias, f_bias, ml_norm_w,
           w_out, norm2_w, router_w, router_b, w_gate_up, b_gate_up, w_down, b_down, final_norm_w):
    b, s, d = x.shape
    t = b * s
    assert w_in.shape[0] == 1, "single-layer problem"
    h = x.reshape(t, d)
    for l in range(1):
        wi = w_in[l]
        o_z, o_xbc, o_dt = 0, D_SSD, D_SSD + D_SSD + 2 * SSD_GROUPS * SSD_STATE
        o_q = o_dt + SSD_HEADS
        o_k = o_q + ML_HEADS * ML_DQK
        o_v = o_k + ML_HEADS * ML_DQK
        o_o = o_v + D_ML
        o_i = o_o + D_ML
        o_f = o_i + ML_HEADS
        w_big = jnp.concatenate([wi[:, o_z:o_dt], wi[:, o_q:o_i]], axis=1).astype(BF16)
        w_small = jnp.concatenate(
            [wi[:, o_dt:o_q], wi[:, o_i:o_f], wi[:, o_f:o_f + ML_HEADS],
             jnp.zeros((d, LANES - SSD_HEADS - 2 * ML_HEADS), F32)], axis=1).astype(BF16)

        proj, small = _input_projection(h, norm1_w[l].reshape(1, d), w_big, w_small)
        proj3 = proj.reshape(b, s, D_BIG)
        small3 = small.reshape(b, s, LANES)
        y_ssd = _ssd_mixer(proj3, small3, conv_w[l], conv_b[l], dt_bias[l], a_log[l], d_skip[l], ssd_norm_w[l], b, s)
        y_ml, w_glu, w_lin = _mlstm_mixer(proj3, small3, i_bias[l], f_bias[l], ml_norm_w[l], b, s, w_gate_up[l])

        rw = jnp.pad(router_w[l], ((0, 0), (0, LANES - N_EXPERTS)))
        rw_hi = rw.astype(BF16)
        rw = jnp.concatenate([rw_hi, (rw - rw_hi.astype(F32)).astype(BF16)], axis=1)
        rb = jnp.pad(router_b[l], (0, LANES - N_EXPERTS)).reshape(1, LANES)
        h1, xnp, seli, selg, cnt = _outproj_router(
            h, y_ssd.reshape(t, D_SSD), y_ml.reshape(t, D_ML), w_out[l].astype(BF16), norm2_w[l].reshape(1, d), rw, rb)

        n_blocks = (t * TOP_K) // MOE_TM + N_EXPERTS
        dest, table = _route_offsets(seli, cnt, n_blocks)
        dest_flat = dest[:, :TOP_K].reshape(-1)
        block_expert = table[:n_blocks, 0]
        n_active = table[0:1, 1]
        block_nsub = table[:n_blocks, 2]

        x_rows = _dispatch(table[:N_EXPERTS, 3], table[:N_EXPERTS, 4], table[0:1, 5], dest_flat, xnp,
                           n_blocks * MOE_TM)
        b_glu = b_gate_up[l][:, 0::2].reshape(N_EXPERTS, 1, D_FF)
        b_lin = b_gate_up[l][:, 1::2].reshape(N_EXPERTS, 1, D_FF)
        y_rows = _moe_ffn(block_expert, block_nsub, n_active, x_rows, w_glu, w_lin, b_glu, b_lin, w_down[l],
                          b_down[l].reshape(N_EXPERTS, 1, d), n_blocks)
        h = _combine(dest_flat, h1, selg, final_norm_w.reshape(1, d), y_rows)
    return h.reshape(b, s, d)
```

```python
import functools

import jax
import jax.numpy as jnp
from jax import lax
from jax.experimental import pallas as pl
from jax.experimental.pallas import tpu as pltpu

F32 = jnp.float32
BF16 = jnp.bfloat16
I32 = jnp.int32
U32 = jnp.uint32

LANES = 128
SUBLANES = 8
VMEM_LIMIT = 56 * 1024 * 1024

D_MODEL = 2048
EPS = 1e-5
SSD_HEAD_DIM = 64
SSD_HEADS = 32
D_SSD = 2048
SSD_GROUPS = 4
SSD_STATE = 128
SSD_CONV = 4
CHUNK = 128
ML_HEADS = 8
ML_DV = 256
ML_DQK = 128
D_ML = 2048
N_EXPERTS = 32
TOP_K = 4
D_FF = 2048
SWIGLU_ALPHA = 1.702
SWIGLU_LIMIT = 7.0

COL_Z, COL_X, COL_BC, COL_Q, COL_K, COL_V, COL_O = 0, 2048, 4096, 5120, 6144, 7168, 9216
COL_W = 1024
D_BIG = 11264
LANE_DT, LANE_I, LANE_F = 0, 32, 40

IN_TM, IN_TN = 1024, 1024
OUT_TM = 512
ROUTE_TM = 2048
DISP_TT = 512
MOE_TM = 512
MOE_SB = 256
MOE_TF = 1024
PERM_W = 256
PREP_TN = 1024
COMB_TT = 256
X_SUB = D_MODEL // 2 // LANES


def _cparams(sem, vmem=VMEM_LIMIT):
    return pltpu.CompilerParams(dimension_semantics=sem, vmem_limit_bytes=vmem)


def _dot(a, b):
    return jnp.dot(a, b, preferred_element_type=F32)


def _dot_exact(a, b):
    return jnp.dot(a, b, preferred_element_type=F32, precision=lax.Precision.HIGHEST)


def _split2_dot(x, e2_bf16):
    hi = x.astype(BF16)
    lo = (x - hi.astype(F32)).astype(BF16)
    return _dot(jnp.concatenate([hi, lo], axis=1), e2_bf16)


def _sigmoid(x):
    return 1.0 / (1.0 + jnp.exp(-x))


def _softplus(x):
    return jnp.maximum(x, 0.0) + jnp.log1p(jnp.exp(-jnp.abs(x)))


def _tril(n, strict=False):
    r = lax.broadcasted_iota(I32, (n, n), 0)
    c = lax.broadcasted_iota(I32, (n, n), 1)
    return (r > c) if strict else (r >= c)


def _pack_bf16_pairs(y):
    half = D_MODEL // 2
    lo = pltpu.bitcast(y[:, :half].astype(BF16).astype(F32), U32)
    hi = pltpu.bitcast(y[:, half:].astype(BF16).astype(F32), U32)
    return hi | (lo >> 16)


def _unpack_bf16_pairs(src_ref, rows, dst_ref):
    half = D_MODEL // 2
    for sub in range(X_SUB):
        u = src_ref[pl.ds(sub, rows, stride=X_SUB), :]
        dst_ref[0:rows, sub * LANES:(sub + 1) * LANES] = pltpu.bitcast(u << 16, F32).astype(BF16)
        dst_ref[0:rows, half + sub * LANES:half + (sub + 1) * LANES] = (
            pltpu.bitcast(u & jnp.uint32(0xFFFF0000), F32).astype(BF16))


def _deinterleave_tile(w_ref, p_ref, glu_ref, lin_ref):
    hw = PERM_W // 2
    for s in range(w_ref.shape[2] // PERM_W):
        t = _dot(w_ref[0, :, s * PERM_W:(s + 1) * PERM_W].astype(BF16), p_ref[...])
        glu_ref[0, :, s * hw:(s + 1) * hw] = t[:, :hw].astype(BF16)
        lin_ref[0, :, s * hw:(s + 1) * hw] = t[:, hw:].astype(BF16)


def _deinterleave_perm():
    src = jnp.arange(PERM_W)
    dst = jnp.where(src % 2 == 0, src // 2, PERM_W // 2 + src // 2)
    return (dst[:, None] == jnp.arange(PERM_W)[None, :]).astype(BF16)


def _deinterleave_specs(w_gate_up, tile_index):
    e, k, n = w_gate_up.shape

    def at(*ids):
        ei, ci = tile_index(*ids)
        return ei, 0, ci

    in_specs = [pl.BlockSpec((1, k, PREP_TN), at), pl.BlockSpec((PERM_W, PERM_W), lambda *ids: (0, 0))]
    out_specs = [pl.BlockSpec((1, k, PREP_TN // 2), at)] * 2
    out_shape = [jax.ShapeDtypeStruct((e, k, n // 2), BF16)] * 2
    return in_specs, out_specs, out_shape


def _deinterleave_gate_up(w_gate_up):
    e, _, n = w_gate_up.shape
    in_specs, out_specs, out_shape = _deinterleave_specs(w_gate_up, lambda i, j: (i, j))
    return pl.pallas_call(
        _deinterleave_tile,
        grid=(e, n // PREP_TN),
        in_specs=in_specs,
        out_specs=out_specs,
        out_shape=out_shape,
        compiler_params=_cparams(("arbitrary", "arbitrary")),
        name="moe_deinterleave",
    )(w_gate_up, _deinterleave_perm())


def _inproj_kernel(x_ref, nw_ref, w_ref, ws_ref, o_ref, os_ref, n_scr):
    @pl.when(pl.program_id(1) == 0)
    def _():
        x = x_ref[...]
        n = x * lax.rsqrt(jnp.mean(x * x, axis=-1, keepdims=True) + EPS) * nw_ref[...]
        nb = n.astype(BF16)
        n_scr[...] = nb
        os_ref[...] = _dot(nb, ws_ref[...])

    o_ref[...] = _dot(n_scr[...], w_ref[...]).astype(BF16)


def _input_projection(xf, norm_w, w_big, w_small):
    t = xf.shape[0]
    return pl.pallas_call(
        _inproj_kernel,
        grid=(t // IN_TM, D_BIG // IN_TN),
        in_specs=[pl.BlockSpec((IN_TM, D_MODEL), lambda i, j: (i, 0)),
                  pl.BlockSpec((1, D_MODEL), lambda i, j: (0, 0)),
                  pl.BlockSpec((D_MODEL, IN_TN), lambda i, j: (0, j)),
                  pl.BlockSpec((D_MODEL, LANES), lambda i, j: (0, 0))],
        out_specs=[pl.BlockSpec((IN_TM, IN_TN), lambda i, j: (i, j)),
                   pl.BlockSpec((IN_TM, LANES), lambda i, j: (i, 0))],
        out_shape=[jax.ShapeDtypeStruct((t, D_BIG), BF16),
                   jax.ShapeDtypeStruct((t, LANES), F32)],
        scratch_shapes=[pltpu.VMEM((IN_TM, D_MODEL), BF16)],
        compiler_params=_cparams(("arbitrary", "arbitrary")),
        name="inproj",
    )(xf, norm_w, w_big, w_small)


def _ssd_kernel(z_ref, x_ref, bc_ref, sm_ref, cw_ref, cb_ref, shift_ref, dtb_ref, alog_ref,
                dskip_ref, nw_ref, e_ref, o_ref, ext, y_scr, *states):
    q = CHUNK

    @pl.when(pl.program_id(1) == 0)
    def _():
        ext[0:q, :] = jnp.zeros((q, ext.shape[1]), BF16)
        for state in states:
            state[...] = jnp.zeros_like(state)

    ext[q:2 * q, 0:D_SSD] = x_ref[0]
    ext[q:2 * q, D_SSD:] = bc_ref[0]
    shifted = _dot(shift_ref[...], ext[...])
    acc = cb_ref[...] + ext[q:2 * q, :].astype(F32) * cw_ref[SSD_CONV - 1:SSD_CONV, :]
    for j in range(SSD_CONV - 1):
        acc = acc + shifted[j * q:(j + 1) * q, :] * cw_ref[j:j + 1, :]
    xbc = acc * _sigmoid(acc)
    xs = xbc[:, :D_SSD]
    bcs = xbc[:, D_SSD:]
    ext[0:q, :] = ext[q:2 * q, :]

    lane = lax.broadcasted_iota(I32, (1, LANES), 1)
    g = sm_ref[0]
    dt = _softplus(g + dtb_ref[...])
    a = jnp.where(lane < SSD_HEADS, -jnp.exp(alog_ref[...]), 0.0)
    da = dt * a
    tri = _tril(q).astype(F32)
    a_cum = _dot_exact(tri, da)
    a_cum_t = a_cum.T
    a_end = a_cum[q - 1:q, :]
    stack = jnp.concatenate(
        [dt, jnp.exp(a_end - a_cum), jnp.exp(a_cum), jnp.broadcast_to(jnp.exp(a_end), (SUBLANES, LANES))], axis=0)
    ex = _split2_dot(stack, e_ref[...])
    dt_x = ex[0:q]
    to_end_x = ex[q:2 * q]
    from_start_x = ex[2 * q:3 * q]
    chunk_decay_x = ex[3 * q:3 * q + 1]

    xdt = xs * dt_x
    xdt_b = xdt.astype(BF16)
    xde_b = (xdt * to_end_x).astype(BF16)
    causal = _tril(q)
    lane_q = lax.broadcasted_iota(I32, (1, LANES), 1)
    gw = SSD_STATE
    hpg = SSD_HEADS // SSD_GROUPS
    cpg = D_SSD // SSD_GROUPS
    for gi in range(SSD_GROUPS):
        bg = bcs[:, gi * gw:(gi + 1) * gw]
        cg_b = bcs[:, SSD_GROUPS * gw + gi * gw:SSD_GROUPS * gw + (gi + 1) * gw].astype(BF16)
        cb = lax.dot_general(cg_b, bg.astype(BF16), (((1,), (1,)), ((), ())), preferred_element_type=F32)
        bg_t = bg.T.astype(BF16)
        cs = slice(gi * cpg, (gi + 1) * cpg)
        prev = states[gi][...]
        y_off = _dot(cg_b, prev.astype(BF16)) * from_start_x[:, cs]
        states[gi][...] = prev * chunk_decay_x[:, cs] + _dot(bg_t, xde_b[:, cs])
        for pr in range(hpg // 2):
            c0 = gi * cpg + pr * LANES
            xp = xdt_b[:, c0:c0 + LANES]
            acc = y_off[:, pr * LANES:(pr + 1) * LANES]
            for sub in range(2):
                h = gi * hpg + pr * 2 + sub
                seg = a_cum[:, h:h + 1] - a_cum_t[h:h + 1, :]
                decay = jnp.exp(jnp.where(causal, seg, -jnp.inf))
                m = (cb * decay).astype(BF16)
                keep = (lane_q < SSD_HEAD_DIM) if sub == 0 else (lane_q >= SSD_HEAD_DIM)
                acc = acc + _dot(m, jnp.where(keep, xp, jnp.zeros_like(xp)))
            y_scr[:, c0:c0 + LANES] = acc

    y = y_scr[...] + xs * dskip_ref[...]
    zf = z_ref[0].astype(F32)
    y = y * (zf * _sigmoid(zf))
    for gi in range(SSD_GROUPS):
        cs = slice(gi * cpg, (gi + 1) * cpg)
        yg = y[:, cs]
        yg = yg * lax.rsqrt(jnp.mean(yg * yg, axis=-1, keepdims=True) + EPS)
        o_ref[0, :, cs] = (yg * nw_ref[:, cs]).astype(BF16)


def _ssd_mixer(proj, small, conv_w, conv_b, dt_bias, a_log, d_skip, norm_w, b, s):
    nc = s // CHUNK
    pad = LANES - SSD_HEADS
    dtb = jnp.pad(dt_bias, (0, pad)).reshape(1, LANES)
    alog = jnp.pad(a_log, (0, pad)).reshape(1, LANES)
    dskip = jnp.repeat(d_skip, SSD_HEAD_DIM).reshape(1, D_SSD)
    expand = (jnp.arange(LANES)[:, None] == (jnp.arange(D_SSD) // SSD_HEAD_DIM)[None, :]).astype(BF16)
    expand = jnp.concatenate([expand, expand], axis=0)
    d_bc = 2 * SSD_GROUPS * SSD_STATE
    d_conv = D_SSD + d_bc
    tt = jnp.arange((SSD_CONV - 1) * CHUNK)
    back = (SSD_CONV - 1) - tt // CHUNK
    shift = ((CHUNK + tt % CHUNK - back)[:, None] == jnp.arange(2 * CHUNK)[None, :]).astype(BF16)
    const = lambda shape: pl.BlockSpec(shape, lambda i, j: (0,) * len(shape))
    col = lambda width, off: pl.BlockSpec((1, CHUNK, width), lambda i, j: (i, j, off // width))
    return pl.pallas_call(
        _ssd_kernel,
        grid=(b, nc),
        in_specs=[col(D_SSD, COL_Z), col(D_SSD, COL_X), col(d_bc, COL_BC), col(LANES, 0),
                  const((SSD_CONV, d_conv)), const((1, d_conv)), const(((SSD_CONV - 1) * CHUNK, 2 * CHUNK)),
                  const((1, LANES)), const((1, LANES)), const((1, D_SSD)), const((1, D_SSD)),
                  const((2 * LANES, D_SSD))],
        out_specs=pl.BlockSpec((1, CHUNK, D_SSD), lambda i, j: (i, j, 0)),
        out_shape=jax.ShapeDtypeStruct((b, s, D_SSD), BF16),
        scratch_shapes=[pltpu.VMEM((2 * CHUNK, d_conv), BF16),
                        pltpu.VMEM((CHUNK, D_SSD), F32)]
        + [pltpu.VMEM((SSD_STATE, D_SSD // SSD_GROUPS), F32)] * SSD_GROUPS,
        compiler_params=_cparams(("arbitrary", "arbitrary")),
        name="ssd_mixer",
    )(proj, proj, proj, small, conv_w, conv_b.reshape(1, -1), shift, dtb, alog, dskip, norm_w.reshape(1, -1), expand)


def _mlstm_kernel(q_ref, k_ref, v0_ref, v1_ref, o0_ref, o1_ref, sm_ref, bias_ref, nw_ref, *rest, prep_gate_up):
    L = CHUNK
    heads_per_ref = COL_W // ML_DV

    def head_cols(refs, h):
        c0 = (h % heads_per_ref) * ML_DV
        return refs[h // heads_per_ref][0, :, c0:c0 + ML_DV]

    if prep_gate_up:
        wg_ref, perm_ref, out_ref, glu_ref, lin_ref, n_scr, m_scr, *ct_scrs = rest
        _deinterleave_tile(wg_ref, perm_ref, glu_ref, lin_ref)
    else:
        out_ref, n_scr, m_scr, *ct_scrs = rest

    @pl.when(pl.program_id(1) == 0)
    def _():
        for ct_scr in ct_scrs:
            ct_scr[...] = jnp.zeros_like(ct_scr)
        n_scr[...] = jnp.zeros_like(n_scr)
        m_scr[...] = jnp.zeros_like(m_scr)

    n_all = n_scr[...]
    m_all = m_scr[...]
    n_new_all = n_all
    m_new_all = m_all
    head_row = lax.broadcasted_iota(I32, (ML_HEADS, LANES), 0)

    g = sm_ref[0] + bias_ref[...]
    log_f = jnp.minimum(g, 0.0) - jnp.log1p(jnp.exp(-jnp.abs(g)))
    tri = _tril(L).astype(F32)
    bcum = _dot_exact(tri, log_f)
    bcum_t = bcum.T
    keys_before = jnp.logical_not(_tril(L, strict=True))
    row16 = lax.broadcasted_iota(I32, (2 * SUBLANES, ML_DQK), 0)
    scale = ML_DQK ** -0.5
    nt = (((1,), (1,)), ((), ()))
    tn = (((0,), (0,)), ((), ()))
    for h in range(ML_HEADS):
        li, lf = LANE_I + h, LANE_F + h
        li_col, b_col, b_row = g[:, li:li + 1], bcum[:, lf:lf + 1], bcum_t[lf:lf + 1, :]
        btot = bcum[L - 1:L, lf:lf + 1]
        m_prev = m_all[h:h + 1, 0:1]
        ct_prev = ct_scrs[h][...]
        n_prev = n_all[h:h + 1, :]
        qb = q_ref[0, :, h * ML_DQK:(h + 1) * ML_DQK]
        kb = k_ref[0, :, h * ML_DQK:(h + 1) * ML_DQK]
        vb = head_cols((v0_ref, v1_ref), h)
        w_end = btot - b_col + li_col
        m_loc = jnp.max(w_end, axis=0, keepdims=True)
        k_end = kb.astype(F32) * jnp.exp(w_end - m_loc)
        kv_t = lax.dot_general(vb, k_end.astype(BF16), tn, preferred_element_type=F32)
        nk = jnp.sum(k_end, axis=0, keepdims=True)
        d_log = jnp.where(keys_before, (li_col - b_col) + b_row, -jnp.inf)
        m_inter = b_row + m_prev
        m_t = jnp.maximum(m_inter, jnp.max(d_log, axis=0, keepdims=True))
        s_t = lax.dot_general(kb, qb, nt, preferred_element_type=F32)
        s_t = s_t * (scale * jnp.exp(d_log - m_t))
        inter = scale * jnp.exp(m_inter - m_t)
        num_t = (lax.dot_general(vb, s_t.astype(BF16), tn, preferred_element_type=F32)
                 + lax.dot_general(ct_prev.astype(BF16), qb, nt, preferred_element_type=F32) * inter)
        n_hi = n_prev.astype(BF16).astype(F32)
        n_rows = jnp.where(row16 == 0, n_hi, jnp.where(row16 == 1, n_prev - n_hi, 0.0)).astype(BF16)
        qn = lax.dot_general(n_rows, qb, nt, preferred_element_type=F32)
        den = jnp.sum(s_t, axis=0, keepdims=True) + (qn[0:1, :] + qn[1:2, :]) * inter
        den = jnp.maximum(jnp.abs(den), jnp.exp(-m_t))
        ht_t = num_t / den
        ht_t = ht_t * lax.rsqrt(jnp.mean(ht_t * ht_t, axis=0, keepdims=True) + EPS)
        vs = slice(h * ML_DV, (h + 1) * ML_DV)
        og = _sigmoid(head_cols((o0_ref, o1_ref), h).astype(F32))
        out_ref[0, :, vs] = (ht_t.T * nw_ref[:, vs] * og).astype(BF16)
        m_new = jnp.maximum(btot + m_prev, m_loc)
        fa = jnp.exp(btot + m_prev - m_new)
        fg = jnp.exp(m_loc - m_new)
        ct_scrs[h][...] = fa * ct_prev + fg * kv_t
        n_new_all = jnp.where(head_row == h, fa * n_prev + fg * nk, n_new_all)
        m_new_all = jnp.where(head_row == h, m_new, m_new_all)
    n_scr[...] = n_new_all
    m_scr[...] = m_new_all


def _mlstm_mixer(proj, small, i_bias, f_bias, norm_w, b, s, w_gate_up):
    nc = s // CHUNK
    bias = jnp.zeros((LANES,), F32).at[LANE_I:LANE_I + ML_HEADS].set(i_bias).at[LANE_F:LANE_F + ML_HEADS].set(f_bias)
    d_qk = ML_HEADS * ML_DQK
    const = lambda shape: pl.BlockSpec(shape, lambda i, j: (0,) * len(shape))
    col = lambda width, off: pl.BlockSpec((1, CHUNK, width), lambda i, j: (i, j, off // width))
    e, _, n = w_gate_up.shape
    tiles_per_expert = n // PREP_TN
    prep = b * nc == e * tiles_per_expert
    in_specs = [col(d_qk, COL_Q), col(d_qk, COL_K), col(COL_W, COL_V), col(COL_W, COL_V + COL_W),
                col(COL_W, COL_O), col(COL_W, COL_O + COL_W), col(LANES, 0), const((1, LANES)), const((1, D_ML))]
    out_specs = [pl.BlockSpec((1, CHUNK, D_ML), lambda i, j: (i, j, 0))]
    out_shape = [jax.ShapeDtypeStruct((b, s, D_ML), BF16)]
    args = [proj] * 6 + [small, bias.reshape(1, LANES), norm_w.reshape(1, -1)]
    if prep:
        w_in_specs, w_out_specs, w_out_shape = _deinterleave_specs(
            w_gate_up, lambda i, j: ((i * nc + j) // tiles_per_expert, (i * nc + j) % tiles_per_expert))
        in_specs += w_in_specs
        out_specs += w_out_specs
        out_shape += w_out_shape
        args += [w_gate_up, _deinterleave_perm()]
    outs = pl.pallas_call(
        functools.partial(_mlstm_kernel, prep_gate_up=prep),
        grid=(b, nc),
        in_specs=in_specs,
        out_specs=out_specs,
        out_shape=out_shape,
        scratch_shapes=[pltpu.VMEM((ML_HEADS, ML_DQK), F32), pltpu.VMEM((ML_HEADS, LANES), F32)]
        + [pltpu.VMEM((ML_DV, ML_DQK), F32)] * ML_HEADS,
        compiler_params=_cparams(("arbitrary", "arbitrary")),
        name="mlstm_mixer",
    )(*args)
    if prep:
        return outs[0], outs[1], outs[2]
    return (outs[0], *_deinterleave_gate_up(w_gate_up))


def _outproj_router_kernel(x_ref, ys_ref, ym_ref, wo_ref, n2w_ref, rw_ref, rb_ref,
                           h1_ref, xnp_ref, seli_ref, selg_ref, cnt_ref, carry):
    tm = x_ref.shape[0]

    @pl.when(pl.program_id(0) == 0)
    def _():
        carry[...] = jnp.zeros_like(carry)

    h1 = x_ref[...] + _dot(ys_ref[...], wo_ref[0:D_SSD, :]) + _dot(ym_ref[...], wo_ref[D_SSD:, :])
    h1_ref[...] = h1
    xn = h1 * lax.rsqrt(jnp.mean(h1 * h1, axis=-1, keepdims=True) + EPS) * n2w_ref[...]
    packed = _pack_bf16_pairs(xn)
    for sub in range(X_SUB):
        xnp_ref[pl.ds(sub, tm, stride=X_SUB), :] = packed[:, sub * LANES:(sub + 1) * LANES]

    lane = lax.broadcasted_iota(I32, (tm, LANES), 1)
    lane_f = lane.astype(F32)
    xn_hi = xn.astype(BF16)
    xn_lo = (xn - xn_hi.astype(F32)).astype(BF16)
    hh_hl = _dot(xn_hi, rw_ref[...])
    logits = hh_hl[:, :LANES] + hh_hl[:, LANES:] + _dot(xn_lo, rw_ref[:, :LANES]) + rb_ref[...]
    work = jnp.where(lane < N_EXPERTS, logits, -jnp.inf)
    vals, idxs = [], []
    chosen = jnp.zeros((tm, LANES), jnp.bool_)
    for _ in range(TOP_K):
        m = jnp.max(work, axis=1, keepdims=True)
        idx = jnp.min(jnp.where(work == m, lane_f, float(LANES)), axis=1, keepdims=True)
        sel = lane_f == idx
        vals.append(m)
        idxs.append(idx)
        chosen = jnp.logical_or(chosen, sel)
        work = jnp.where(sel, -jnp.inf, work)
    exps = [jnp.exp(v - vals[0]) for v in vals]
    denom = exps[0] + exps[1] + exps[2] + exps[3]

    onehot = jnp.where(chosen, 1.0, 0.0)
    ranks = _dot(_tril(tm, strict=True).astype(BF16), onehot.astype(BF16)) + carry[...]
    total = carry[...] + jnp.sum(onehot, axis=0, keepdims=True)
    carry[...] = total
    cnt_ref[...] = jnp.broadcast_to(total, cnt_ref.shape)

    seli = jnp.zeros((tm, LANES), F32)
    selg = jnp.zeros((tm, LANES), F32)
    for k in range(TOP_K):
        rank_k = jnp.sum(jnp.where(lane_f == idxs[k], ranks, 0.0), axis=1, keepdims=True)
        seli = jnp.where(lane == k, idxs[k], seli)
        seli = jnp.where(lane == TOP_K + k, rank_k, seli)
        selg = jnp.where(lane == k, exps[k] / denom, selg)
    seli_ref[...] = seli.astype(I32)
    selg_ref[...] = selg


def _outproj_router(xf, y_ssd, y_ml, w_out_b, norm2_w, router_w_pad, router_b_pad):
    t = xf.shape[0]
    tm = OUT_TM
    row = lambda width: pl.BlockSpec((tm, width), lambda i: (i, 0))
    const = lambda shape: pl.BlockSpec(shape, lambda i: (0,) * len(shape))
    return pl.pallas_call(
        _outproj_router_kernel,
        grid=(t // tm,),
        in_specs=[row(D_MODEL), row(D_SSD), row(D_ML),
                  pl.BlockSpec((D_SSD + D_ML, D_MODEL), lambda i: (0, 0), pipeline_mode=pl.Buffered(1)),
                  const((1, D_MODEL)),
                  const((D_MODEL, 2 * LANES)), const((1, LANES))],
        out_specs=[row(D_MODEL), pl.BlockSpec((tm * X_SUB, LANES), lambda i: (i, 0)), row(LANES), row(LANES),
                   const((SUBLANES, LANES))],
        out_shape=[jax.ShapeDtypeStruct((t, D_MODEL), F32),
                   jax.ShapeDtypeStruct((t * X_SUB, LANES), U32),
                   jax.ShapeDtypeStruct((t, LANES), I32),
                   jax.ShapeDtypeStruct((t, LANES), F32),
                   jax.ShapeDtypeStruct((SUBLANES, LANES), F32)],
        scratch_shapes=[pltpu.VMEM((1, LANES), F32)],
        compiler_params=_cparams(("arbitrary",)),
        name="outproj_router",
    )(xf, y_ssd, y_ml, w_out_b, norm2_w, router_w_pad, router_b_pad)


def _route_kernel(seli_ref, cnt_ref, dest_ref, be_ref):
    tm = seli_ref.shape[0]
    nbp = be_ref.shape[0]
    lane1 = lax.broadcasted_iota(I32, (1, LANES), 1)
    cnt = jnp.where(lane1 < N_EXPERTS, cnt_ref[0:1, :], 0.0)
    padded = jnp.ceil(cnt / MOE_TM) * MOE_TM
    r = lax.broadcasted_iota(I32, (LANES, LANES), 0)
    c = lax.broadcasted_iota(I32, (LANES, LANES), 1)
    upper = jnp.where(r <= c, 1.0, 0.0)
    pend = jnp.round(_dot_exact(jnp.broadcast_to(padded, (SUBLANES, LANES)), upper))[0:1, :]
    pstart = pend - padded

    lane = lax.broadcasted_iota(I32, (tm, LANES), 1)
    lane_f = lane.astype(F32)
    seli = seli_ref[...].astype(F32)
    dest = jnp.zeros((tm, LANES), F32)
    for k in range(TOP_K):
        idx_k = jnp.sum(jnp.where(lane == k, seli, 0.0), axis=1, keepdims=True)
        rank_k = jnp.sum(jnp.where(lane == TOP_K + k, seli, 0.0), axis=1, keepdims=True)
        start_k = jnp.sum(jnp.where(lane_f == idx_k, pstart, 0.0), axis=1, keepdims=True)
        dest = jnp.where(lane == k, start_k + rank_k, dest)
    dest_ref[...] = dest.astype(I32)

    blk_start = (lax.broadcasted_iota(I32, (nbp, LANES), 0) * MOE_TM).astype(F32)
    lane_b = lax.broadcasted_iota(I32, (nbp, LANES), 1)
    passed = jnp.where(jnp.logical_and(lane_b < N_EXPERTS, pend <= blk_start), 1.0, 0.0)
    expert = jnp.minimum(jnp.sum(passed, axis=1, keepdims=True), float(N_EXPERTS - 1))
    n_active = pend[:, N_EXPERTS - 1:N_EXPERTS] / MOE_TM
    tok_end = jnp.sum(jnp.where(lane_b.astype(F32) == expert, pstart + cnt, 0.0), axis=1, keepdims=True)
    valid = jnp.clip(tok_end - blk_start[:, 0:1], 0.0, float(MOE_TM))
    n_sub = jnp.ceil(valid / MOE_SB)
    own = lane_b == lax.broadcasted_iota(I32, (nbp, LANES), 0)
    pad_start = jnp.sum(jnp.where(own, pstart + cnt, 0.0), axis=1, keepdims=True)
    pad_cnt = jnp.sum(jnp.where(own, padded - cnt, 0.0), axis=1, keepdims=True)
    cols = (expert, n_active, n_sub, pad_start, pad_cnt, pend[:, N_EXPERTS - 1:N_EXPERTS])
    be = jnp.zeros((nbp, LANES), F32)
    for j, v in enumerate(cols):
        be = jnp.where(lane_b == j, v, be)
    be_ref[...] = be.astype(I32)


def _route_offsets(seli, cnt, n_blocks):
    t = seli.shape[0]
    nbp = -(-n_blocks // SUBLANES) * SUBLANES
    return pl.pallas_call(
        _route_kernel,
        grid=(t // ROUTE_TM,),
        in_specs=[pl.BlockSpec((ROUTE_TM, LANES), lambda i: (i, 0)),
                  pl.BlockSpec((SUBLANES, LANES), lambda i: (0, 0))],
        out_specs=[pl.BlockSpec((ROUTE_TM, LANES), lambda i: (i, 0)),
                   pl.BlockSpec((nbp, LANES), lambda i: (0, 0))],
        out_shape=[jax.ShapeDtypeStruct((t, LANES), I32),
                   jax.ShapeDtypeStruct((nbp, LANES), I32)],
        compiler_params=_cparams(("arbitrary",)),
        name="route_offsets",
    )(seli, cnt)


def _dispatch_kernel(pad_start_ref, pad_cnt_ref, tail_ref, dest_ref, xnp_ref, rows_ref, zbuf, sem, zsem, *, n_zero):
    tt = xnp_ref.shape[0] // X_SUB
    n_rows = rows_ref.shape[0] // X_SUB
    first = pl.program_id(0) == 0

    def zero_rows(start_row, n):
        dst = rows_ref.at[pl.ds(pl.multiple_of(start_row * X_SUB, X_SUB), n * X_SUB)]
        return pltpu.make_async_copy(zbuf.at[pl.ds(0, n * X_SUB)], dst, zsem)

    @pl.when(first)
    def _():
        zbuf[...] = jnp.zeros_like(zbuf)

        def expert_padding(e, carry):
            start, count = pad_start_ref[e], pad_cnt_ref[e]
            for k in reversed(range(MOE_TM.bit_length() - 1)):
                run = 1 << k
                pl.when((count & run) != 0)(
                    lambda run=run: zero_rows(start + (count & ~(2 * run - 1)), run).start())
            return carry

        lax.fori_loop(0, N_EXPERTS, expert_padding, 0)

        def trailing_block(j, carry):
            start = tail_ref[0] + j * MOE_TM
            pl.when(start < n_rows)(lambda: zero_rows(start, MOE_TM).start())
            return carry

        lax.fori_loop(0, N_EXPERTS, trailing_block, 0)

    def tile_copy(t, d):
        src = xnp_ref.at[pl.ds(pl.multiple_of(t * X_SUB, X_SUB), X_SUB)]
        dst = rows_ref.at[pl.ds(pl.multiple_of(d * X_SUB, X_SUB), X_SUB)]
        return pltpu.make_async_copy(src, dst, sem)

    def issue(t, carry):
        for k in range(TOP_K):
            tile_copy(t, dest_ref[t * TOP_K + k]).start(priority=k % 2)
        return carry

    lax.fori_loop(0, tt, issue, 0, unroll=4)
    for _ in range(TOP_K):
        pltpu.make_async_copy(xnp_ref, rows_ref.at[pl.ds(0, tt * X_SUB)], sem).wait()

    @pl.when(first)
    def _():
        span = rows_ref.at[pl.ds(0, n_zero * X_SUB)]
        pltpu.make_async_copy(span, span, zsem).wait()


def _dispatch(pad_start, pad_cnt, tail_start, dest_flat, xnp, n_rows):
    t = xnp.shape[0] // X_SUB
    grid_spec = pltpu.PrefetchScalarGridSpec(
        num_scalar_prefetch=3,
        grid=(t // DISP_TT,),
        in_specs=[pl.BlockSpec((DISP_TT * TOP_K,), lambda i, *_: (i,), memory_space=pltpu.SMEM),
                  pl.BlockSpec((DISP_TT * X_SUB, LANES), lambda i, *_: (i, 0))],
        out_specs=pl.BlockSpec(memory_space=pl.ANY),
        scratch_shapes=[pltpu.VMEM((MOE_TM * X_SUB, LANES), U32), pltpu.SemaphoreType.DMA(()),
                        pltpu.SemaphoreType.DMA(())],
    )
    return pl.pallas_call(
        functools.partial(_dispatch_kernel, n_zero=n_rows - t * TOP_K),
        grid_spec=grid_spec,
        out_shape=jax.ShapeDtypeStruct((n_rows * X_SUB, LANES), U32),
        compiler_params=_cparams(("arbitrary",)),
        name="moe_dispatch",
    )(pad_start, pad_cnt, tail_start, dest_flat, xnp)


def _moe_kernel(be_ref, nsb_ref, na_ref, x_ref, wglu_ref, wlin_ref, bglu_ref, blin_ref, wd_ref, bd_ref, o_ref,
                xb_next, xb_cur, acc):
    m = pl.program_id(0)
    f = pl.program_id(1)
    nf = pl.num_programs(1)
    n_sub = MOE_TM // MOE_SB

    def ffn(nsb, first, last):
        rows = nsb * MOE_SB
        if first:
            @pl.when(m == 0)
            def _():
                _unpack_bf16_pairs(x_ref, MOE_TM, xb_next)

        xb = xb_next if first else xb_cur
        if last:
            _unpack_bf16_pairs(x_ref, MOE_TM, xb_next)
        wd = wd_ref[0].astype(BF16)
        xi = xb[0:rows, :]
        if first:
            xb_cur[0:rows, :] = xi
        glu = jnp.minimum(_dot(xi, wglu_ref[0]) + bglu_ref[0], SWIGLU_LIMIT)
        lin = jnp.clip(_dot(xi, wlin_ref[0]) + blin_ref[0], -SWIGLU_LIMIT, SWIGLU_LIMIT)
        act = glu * _sigmoid(SWIGLU_ALPHA * glu) * (lin + 1.0)
        part = _dot(act.astype(BF16), wd)
        acc[0:rows, :] = (bd_ref[0] + part) if first else (acc[0:rows, :] + part)

        if last:
            packed = _pack_bf16_pairs(acc[0:rows, :])
            for sub in range(X_SUB):
                o_ref[pl.ds(sub, rows, stride=X_SUB), :] = packed[:, sub * LANES:(sub + 1) * LANES]
            if nsb < n_sub:
                o_ref[rows * X_SUB:, :] = jnp.zeros(((MOE_TM - rows) * X_SUB, LANES), U32)

    active = m < na_ref[0]
    kinds = [(True, False, f == 0), (False, True, f == nf - 1)]
    for nsb in range(1, n_sub + 1):
        for first, last, when in kinds:
            cond = jnp.logical_and(jnp.logical_and(active, nsb_ref[m] == nsb), when)
            pl.when(cond)(functools.partial(ffn, nsb, first, last))

    @pl.when(jnp.logical_and(jnp.logical_not(active), f == nf - 1))
    def _():
        o_ref[...] = jnp.zeros_like(o_ref)


def _moe_ffn(block_expert, block_nsub, n_active, x_rows, w_glu, w_lin, b_glu, b_lin, w_down, b_down, n_blocks):
    nf = D_FF // MOE_TF
    assert nf == 2, "the kernel has a first-step and a last-step body and no middle one"

    def blk(m, na):
        return jnp.maximum(jnp.minimum(m, na[0] - 1), 0)

    def ftile(m, f, na):
        return jnp.where(m < na[0], f, nf - 1)

    def expert(m, be, na):
        return be[blk(m, na)]

    col_tile = lambda m, f, be, ns, na: (expert(m, be, na), 0, ftile(m, f, na))
    grid_spec = pltpu.PrefetchScalarGridSpec(
        num_scalar_prefetch=3,
        grid=(n_blocks, nf),
        in_specs=[pl.BlockSpec((MOE_TM * X_SUB, LANES),
                               lambda m, f, be, ns, na: (blk(jnp.where(f == nf - 1, m + 1, m), na), 0)),
                  pl.BlockSpec((1, D_MODEL, MOE_TF), col_tile),
                  pl.BlockSpec((1, D_MODEL, MOE_TF), col_tile),
                  pl.BlockSpec((1, 1, MOE_TF), col_tile),
                  pl.BlockSpec((1, 1, MOE_TF), col_tile),
                  pl.BlockSpec((1, MOE_TF, D_MODEL), lambda m, f, be, ns, na: (expert(m, be, na), ftile(m, f, na), 0)),
                  pl.BlockSpec((1, 1, D_MODEL), lambda m, f, be, ns, na: (expert(m, be, na), 0, 0))],
        out_specs=pl.BlockSpec((MOE_TM * X_SUB, LANES), lambda m, f, be, ns, na: (m, 0)),
        scratch_shapes=[pltpu.VMEM((MOE_TM, D_MODEL), BF16)] * 2 + [pltpu.VMEM((MOE_TM, D_MODEL), F32)],
    )
    return pl.pallas_call(
        _moe_kernel,
        grid_spec=grid_spec,
        out_shape=jax.ShapeDtypeStruct((n_blocks * MOE_TM * X_SUB, LANES), U32),
        compiler_params=_cparams(("arbitrary", "arbitrary")),
        name="moe_ffn",
    )(block_expert, block_nsub, n_active, x_rows, w_glu, w_lin, b_glu, b_lin, w_down, b_down)


def _combine_kernel(dest_ref, dest_next_ref, h1_ref, selg_ref, fw_ref, rows_ref, o_ref, buf, h_scr, sem):
    tt = h1_ref.shape[0]
    i = pl.program_id(0)
    n = pl.num_programs(0)
    half = D_MODEL // 2

    def row_copy(slot, t, k, d):
        src = rows_ref.at[pl.ds(pl.multiple_of(d * X_SUB, X_SUB), X_SUB)]
        dst = buf.at[slot, k, pl.ds(pl.multiple_of(t * X_SUB, X_SUB), X_SUB)]
        return pltpu.make_async_copy(src, dst, sem.at[slot])

    def gather(slot, d_ref):
        def issue(t, carry):
            for k in range(TOP_K):
                row_copy(slot, t, k, d_ref[t * TOP_K + k]).start(priority=k % 2)
            return carry

        lax.fori_loop(0, tt, issue, 0, unroll=4)

    slot = i % 2

    @pl.when(i == 0)
    def _():
        gather(0, dest_ref)

    @pl.when(i + 1 < n)
    def _():
        gather(1 - slot, dest_next_ref)

    for k in range(TOP_K):
        pltpu.make_async_copy(rows_ref.at[pl.ds(0, tt * X_SUB)], buf.at[slot, k], sem.at[slot]).wait()

    gates = selg_ref[...]
    for sub in range(X_SUB):
        lo_cs = slice(sub * LANES, (sub + 1) * LANES)
        hi_cs = slice(half + sub * LANES, half + (sub + 1) * LANES)
        h_lo = h1_ref[:, lo_cs]
        h_hi = h1_ref[:, hi_cs]
        for k in range(TOP_K):
            u = buf[slot, k, pl.ds(sub, tt, stride=X_SUB), :]
            g = gates[:, k:k + 1]
            h_lo = h_lo + g * pltpu.bitcast(u << 16, F32)
            h_hi = h_hi + g * pltpu.bitcast(u & jnp.uint32(0xFFFF0000), F32)
        h_scr[:, lo_cs] = h_lo
        h_scr[:, hi_cs] = h_hi
    h = h_scr[...]
    o_ref[...] = h * lax.rsqrt(jnp.mean(h * h, axis=-1, keepdims=True) + EPS) * fw_ref[...]


def _combine(dest_flat, h1, selg, final_w, y_rows):
    t = h1.shape[0]
    tt = COMB_TT
    n = t // tt
    return pl.pallas_call(
        _combine_kernel,
        grid=(n,),
        in_specs=[pl.BlockSpec((tt * TOP_K,), lambda i: (i,), memory_space=pltpu.SMEM),
                  pl.BlockSpec((tt * TOP_K,), lambda i: (jnp.minimum(i + 1, n - 1),), memory_space=pltpu.SMEM),
                  pl.BlockSpec((tt, D_MODEL), lambda i: (i, 0)),
                  pl.BlockSpec((tt, LANES), lambda i: (i, 0)),
                  pl.BlockSpec((1, D_MODEL), lambda i: (0, 0)),
                  pl.BlockSpec(memory_space=pl.ANY)],
        out_specs=pl.BlockSpec((tt, D_MODEL), lambda i: (i, 0)),
        out_shape=jax.ShapeDtypeStruct((t, D_MODEL), F32),
        scratch_shapes=[pltpu.VMEM((2, TOP_K, tt * X_SUB, LANES), U32), pltpu.VMEM((tt, D_MODEL), F32),
                        pltpu.SemaphoreType.DMA((2,))],
        compiler_params=_cparams(("arbitrary",)),
        name="moe_combine",
    )(dest_flat, dest_flat, h1, selg, final_w, y_rows)


def kernel(x, norm1_w, w_in, conv_w, conv_b, dt_bias, a_log, d_skip, ssd_norm_w, i_bias, f_bias, ml_norm_w,
           w_out, norm2_w, router_w, router_b, w_gate_up, b_gate_up, w_down, b_down, final_norm_w):
    b, s, d = x.shape
    t = b * s
    assert w_in.shape[0] == 1, "single-layer problem"
    h = x.reshape(t, d)
    for l in range(1):
        wi = w_in[l]
        o_z, o_xbc, o_dt = 0, D_SSD, D_SSD + D_SSD + 2 * SSD_GROUPS * SSD_STATE
        o_q = o_dt + SSD_HEADS
        o_k = o_q + ML_HEADS * ML_DQK
        o_v = o_k + ML_HEADS * ML_DQK
        o_o = o_v + D_ML
        o_i = o_o + D_ML
        o_f = o_i + ML_HEADS
        w_big = jnp.concatenate([wi[:, o_z:o_dt], wi[:, o_q:o_i]], axis=1).astype(BF16)
        w_small = jnp.concatenate(
            [wi[:, o_dt:o_q], wi[:, o_i:o_f], wi[:, o_f:o_f + ML_HEADS],
             jnp.zeros((d, LANES - SSD_HEADS - 2 * ML_HEADS), F32)], axis=1).astype(BF16)

        proj, small = _input_projection(h, norm1_w[l].reshape(1, d), w_big, w_small)
        proj3 = proj.reshape(b, s, D_BIG)
        small3 = small.reshape(b, s, LANES)
        y_ssd = _ssd_mixer(proj3, small3, conv_w[l], conv_b[l], dt_bias[l], a_log[l], d_skip[l], ssd_norm_w[l], b, s)
        y_ml, w_glu, w_lin = _mlstm_mixer(proj3, small3, i_bias[l], f_bias[l], ml_norm_w[l], b, s, w_gate_up[l])

        rw = jnp.pad(router_w[l], ((0, 0), (0, LANES - N_EXPERTS)))
        rw_hi = rw.astype(BF16)
        rw = jnp.concatenate([rw_hi, (rw - rw_hi.astype(F32)).astype(BF16)], axis=1)
        rb = jnp.pad(router_b[l], (0, LANES - N_EXPERTS)).reshape(1, LANES)
        h1, xnp, seli, selg, cnt = _outproj_router(
            h, y_ssd.reshape(t, D_SSD), y_ml.reshape(t, D_ML), w_out[l].astype(BF16), norm2_w[l].reshape(1, d), rw, rb)

        n_blocks = (t * TOP_K) // MOE_TM + N_EXPERTS
        dest, table = _route_offsets(seli, cnt, n_blocks)
        dest_flat = dest[:, :TOP_K].reshape(-1)
        block_expert = table[:n_blocks, 0]
        n_active = table[0:1, 1]
        block_nsub = table[:n_blocks, 2]

        x_rows = _dispatch(table[:N_EXPERTS, 3], table[:N_EXPERTS, 4], table[0:1, 5], dest_flat, xnp,
                           n_blocks * MOE_TM)
        b_glu = b_gate_up[l][:, 0::2].reshape(N_EXPERTS, 1, D_FF)
        b_lin = b_gate_up[l][:, 1::2].reshape(N_EXPERTS, 1, D_FF)
        y_rows = _moe_ffn(block_expert, block_nsub, n_active, x_rows, w_glu, w_lin, b_glu, b_lin, w_down[l],
                          b_down[l].reshape(N_EXPERTS, 1, d), n_blocks)
        h = _combine(dest_flat, h1, selg, final_norm_w.reshape(1, d), y_rows)
    return h.reshape(b, s, d)
```

```python
import functools

import jax
import jax.numpy as jnp
from jax import lax
from jax.experimental import pallas as pl
from jax.experimental.pallas import tpu as pltpu

F32 = jnp.float32
BF16 = jnp.bfloat16
I32 = jnp.int32
U32 = jnp.uint32

LANES = 128
SUBLANES = 8
VMEM_LIMIT = 56 * 1024 * 1024

D_MODEL = 2048
EPS = 1e-5
SSD_HEAD_DIM = 64
SSD_HEADS = 32
D_SSD = 2048
SSD_GROUPS = 4
SSD_STATE = 128
SSD_CONV = 4
CHUNK = 128
ML_HEADS = 8
ML_DV = 256
ML_DQK = 128
D_ML = 2048
N_EXPERTS = 32
TOP_K = 4
D_FF = 2048
SWIGLU_ALPHA = 1.702
SWIGLU_LIMIT = 7.0

COL_Z, COL_X, COL_BC, COL_Q, COL_K, COL_V, COL_O = 0, 2048, 4096, 5120, 6144, 7168, 9216
COL_W = 1024
D_BIG = 11264
LANE_DT, LANE_I, LANE_F = 0, 32, 40

IN_TM, IN_TN = 1024, 1024
OUT_TM = 512
ROUTE_TM = 2048
DISP_TT = 512
MOE_TM = 512
MOE_SB = 256
MOE_TF = 1024
PERM_W = 256
PREP_TN = 1024
COMB_TT = 256
X_SUB = D_MODEL // 2 // LANES


def _cparams(sem, vmem=VMEM_LIMIT):
    return pltpu.CompilerParams(dimension_semantics=sem, vmem_limit_bytes=vmem)


def _dot(a, b):
    return jnp.dot(a, b, preferred_element_type=F32)


def _dot_exact(a, b):
    return jnp.dot(a, b, preferred_element_type=F32, precision=lax.Precision.HIGHEST)


def _split2_dot(x, e2_bf16):
    hi = x.astype(BF16)
    lo = (x - hi.astype(F32)).astype(BF16)
    return _dot(jnp.concatenate([hi, lo], axis=1), e2_bf16)


def _sigmoid(x):
    return 1.0 / (1.0 + jnp.exp(-x))


def _softplus(x):
    return jnp.maximum(x, 0.0) + jnp.log1p(jnp.exp(-jnp.abs(x)))


def _tril(n, strict=False):
    r = lax.broadcasted_iota(I32, (n, n), 0)
    c = lax.broadcasted_iota(I32, (n, n), 1)
    return (r > c) if strict else (r >= c)


def _pack_bf16_pairs(y):
    half = D_MODEL // 2
    lo = pltpu.bitcast(y[:, :half].astype(BF16).astype(F32), U32)
    hi = pltpu.bitcast(y[:, half:].astype(BF16).astype(F32), U32)
    return hi | (lo >> 16)


def _unpack_bf16_pairs(src_ref, rows, dst_ref):
    half = D_MODEL // 2
    for sub in range(X_SUB):
        u = src_ref[pl.ds(sub, rows, stride=X_SUB), :]
        dst_ref[0:rows, sub * LANES:(sub + 1) * LANES] = pltpu.bitcast(u << 16, F32).astype(BF16)
        dst_ref[0:rows, half + sub * LANES:half + (sub + 1) * LANES] = (
            pltpu.bitcast(u & jnp.uint32(0xFFFF0000), F32).astype(BF16))


def _deinterleave_tile(w_ref, p_ref, glu_ref, lin_ref):
    hw = PERM_W // 2
    for s in range(w_ref.shape[2] // PERM_W):
        t = _dot(w_ref[0, :, s * PERM_W:(s + 1) * PERM_W].astype(BF16), p_ref[...])
        glu_ref[0, :, s * hw:(s + 1) * hw] = t[:, :hw].astype(BF16)
        lin_ref[0, :, s * hw:(s + 1) * hw] = t[:, hw:].astype(BF16)


def _deinterleave_perm():
    src = jnp.arange(PERM_W)
    dst = jnp.where(src % 2 == 0, src // 2, PERM_W // 2 + src // 2)
    return (dst[:, None] == jnp.arange(PERM_W)[None, :]).astype(BF16)


def _deinterleave_specs(w_gate_up, tile_index):
    e, k, n = w_gate_up.shape

    def at(*ids):
        ei, ci = tile_index(*ids)
        return ei, 0, ci

    in_specs = [pl.BlockSpec((1, k, PREP_TN), at), pl.BlockSpec((PERM_W, PERM_W), lambda *ids: (0, 0))]
    out_specs = [pl.BlockSpec((1, k, PREP_TN // 2), at)] * 2
    out_shape = [jax.ShapeDtypeStruct((e, k, n // 2), BF16)] * 2
    return in_specs, out_specs, out_shape


def _deinterleave_gate_up(w_gate_up):
    e, _, n = w_gate_up.shape
    in_specs, out_specs, out_shape = _deinterleave_specs(w_gate_up, lambda i, j: (i, j))
    return pl.pallas_call(
        _deinterleave_tile,
        grid=(e, n // PREP_TN),
        in_specs=in_specs,
        out_specs=out_specs,
        out_shape=out_shape,
        compiler_params=_cparams(("arbitrary", "arbitrary")),
        name="moe_deinterleave",
    )(w_gate_up, _deinterleave_perm())


def _inproj_kernel(x_ref, nw_ref, w_ref, ws_ref, o_ref, os_ref):
    x = x_ref[...]
    nb = (x * lax.rsqrt(jnp.mean(x * x, axis=-1, keepdims=True) + EPS) * nw_ref[...]).astype(BF16)

    @pl.when(pl.program_id(0) == 0)
    def _():
        os_ref[...] = _dot(nb, ws_ref[...])

    o_ref[...] = _dot(nb, w_ref[...]).astype(BF16)


def _input_projection(xf, norm_w, w_big, w_small):
    t = xf.shape[0]
    n_i = t // IN_TM
    return pl.pallas_call(
        _inproj_kernel,
        grid=(D_BIG // IN_TN, n_i),
        in_specs=[pl.BlockSpec((IN_TM, D_MODEL), lambda j, i: (i, 0)),
                  pl.BlockSpec((1, D_MODEL), lambda j, i: (0, 0)),
                  pl.BlockSpec((D_MODEL, IN_TN), lambda j, i: (0, j)),
                  pl.BlockSpec((D_MODEL, LANES), lambda j, i: (0, 0))],
        out_specs=[pl.BlockSpec((IN_TM, IN_TN), lambda j, i: (i, j)),
                   pl.BlockSpec((IN_TM, LANES), lambda j, i: (jnp.where(j == 0, i, n_i - 1), 0))],
        out_shape=[jax.ShapeDtypeStruct((t, D_BIG), BF16),
                   jax.ShapeDtypeStruct((t, LANES), F32)],
        compiler_params=_cparams(("arbitrary", "arbitrary")),
        name="inproj",
    )(xf, norm_w, w_big, w_small)


def _ssd_kernel(z_ref, x_ref, bc_ref, sm_ref, cw_ref, cb_ref, shift_ref, dtb_ref, alog_ref,
                dskip_ref, nw_ref, e_ref, o_ref, ext, y_scr, *states):
    q = CHUNK

    @pl.when(pl.program_id(1) == 0)
    def _():
        ext[0:q, :] = jnp.zeros((q, ext.shape[1]), BF16)
        for state in states:
            state[...] = jnp.zeros_like(state)

    ext[q:2 * q, 0:D_SSD] = x_ref[0]
    ext[q:2 * q, D_SSD:] = bc_ref[0]
    shifted = _dot(shift_ref[...], ext[...])
    acc = cb_ref[...] + ext[q:2 * q, :].astype(F32) * cw_ref[SSD_CONV - 1:SSD_CONV, :]
    for j in range(SSD_CONV - 1):
        acc = acc + shifted[j * q:(j + 1) * q, :] * cw_ref[j:j + 1, :]
    xbc = acc * _sigmoid(acc)
    xs = xbc[:, :D_SSD]
    bcs = xbc[:, D_SSD:]
    ext[0:q, :] = ext[q:2 * q, :]

    lane = lax.broadcasted_iota(I32, (1, LANES), 1)
    g = sm_ref[0]
    dt = _softplus(g + dtb_ref[...])
    a = jnp.where(lane < SSD_HEADS, -jnp.exp(alog_ref[...]), 0.0)
    da = dt * a
    tri = _tril(q).astype(F32)
    a_cum = _dot_exact(tri, da)
    a_cum_t = a_cum.T
    a_end = a_cum[q - 1:q, :]
    stack = jnp.concatenate(
        [dt, jnp.exp(a_end - a_cum), jnp.exp(a_cum), jnp.broadcast_to(jnp.exp(a_end), (SUBLANES, LANES))], axis=0)
    ex = _split2_dot(stack, e_ref[...])
    dt_x = ex[0:q]
    to_end_x = ex[q:2 * q]
    from_start_x = ex[2 * q:3 * q]
    chunk_decay_x = ex[3 * q:3 * q + 1]

    xdt = xs * dt_x
    xdt_b = xdt.astype(BF16)
    xde_b = (xdt * to_end_x).astype(BF16)
    causal = _tril(q)
    lane_q = lax.broadcasted_iota(I32, (1, LANES), 1)
    gw = SSD_STATE
    hpg = SSD_HEADS // SSD_GROUPS
    cpg = D_SSD // SSD_GROUPS
    for gi in range(SSD_GROUPS):
        bg = bcs[:, gi * gw:(gi + 1) * gw]
        cg_b = bcs[:, SSD_GROUPS * gw + gi * gw:SSD_GROUPS * gw + (gi + 1) * gw].astype(BF16)
        cb = lax.dot_general(cg_b, bg.astype(BF16), (((1,), (1,)), ((), ())), preferred_element_type=F32)
        bg_t = bg.T.astype(BF16)
        cs = slice(gi * cpg, (gi + 1) * cpg)
        prev = states[gi][...]
        y_off = _dot(cg_b, prev.astype(BF16)) * from_start_x[:, cs]
        states[gi][...] = prev * chunk_decay_x[:, cs] + _dot(bg_t, xde_b[:, cs])
        for pr in range(hpg // 2):
            c0 = gi * cpg + pr * LANES
            xp = xdt_b[:, c0:c0 + LANES]
            acc = y_off[:, pr * LANES:(pr + 1) * LANES]
            for sub in range(2):
                h = gi * hpg + pr * 2 + sub
                seg = a_cum[:, h:h + 1] - a_cum_t[h:h + 1, :]
                decay = jnp.exp(jnp.where(causal, seg, -jnp.inf))
                m = (cb * decay).astype(BF16)
                keep = (lane_q < SSD_HEAD_DIM) if sub == 0 else (lane_q >= SSD_HEAD_DIM)
                acc = acc + _dot(m, jnp.where(keep, xp, jnp.zeros_like(xp)))
            y_scr[:, c0:c0 + LANES] = acc

    y = y_scr[...] + xs * dskip_ref[...]
    zf = z_ref[0].astype(F32)
    y = y * (zf * _sigmoid(zf))
    for gi in range(SSD_GROUPS):
        cs = slice(gi * cpg, (gi + 1) * cpg)
        yg = y[:, cs]
        yg = yg * lax.rsqrt(jnp.mean(yg * yg, axis=-1, keepdims=True) + EPS)
        o_ref[0, :, cs] = (yg * nw_ref[:, cs]).astype(BF16)


def _ssd_mixer(proj, small, conv_w, conv_b, dt_bias, a_log, d_skip, norm_w, b, s):
    nc = s // CHUNK
    pad = LANES - SSD_HEADS
    dtb = jnp.pad(dt_bias, (0, pad)).reshape(1, LANES)
    alog = jnp.pad(a_log, (0, pad)).reshape(1, LANES)
    dskip = jnp.repeat(d_skip, SSD_HEAD_DIM).reshape(1, D_SSD)
    expand = (jnp.arange(LANES)[:, None] == (jnp.arange(D_SSD) // SSD_HEAD_DIM)[None, :]).astype(BF16)
    expand = jnp.concatenate([expand, expand], axis=0)
    d_bc = 2 * SSD_GROUPS * SSD_STATE
    d_conv = D_SSD + d_bc
    tt = jnp.arange((SSD_CONV - 1) * CHUNK)
    back = (SSD_CONV - 1) - tt // CHUNK
    shift = ((CHUNK + tt % CHUNK - back)[:, None] == jnp.arange(2 * CHUNK)[None, :]).astype(BF16)
    const = lambda shape: pl.BlockSpec(shape, lambda i, j: (0,) * len(shape))
    col = lambda width, off: pl.BlockSpec((1, CHUNK, width), lambda i, j: (i, j, off // width))
    return pl.pallas_call(
        _ssd_kernel,
        grid=(b, nc),
        in_specs=[col(D_SSD, COL_Z), col(D_SSD, COL_X), col(d_bc, COL_BC), col(LANES, 0),
                  const((SSD_CONV, d_conv)), const((1, d_conv)), const(((SSD_CONV - 1) * CHUNK, 2 * CHUNK)),
                  const((1, LANES)), const((1, LANES)), const((1, D_SSD)), const((1, D_SSD)),
                  const((2 * LANES, D_SSD))],
        out_specs=pl.BlockSpec((1, CHUNK, D_SSD), lambda i, j: (i, j, 0)),
        out_shape=jax.ShapeDtypeStruct((b, s, D_SSD), BF16),
        scratch_shapes=[pltpu.VMEM((2 * CHUNK, d_conv), BF16),
                        pltpu.VMEM((CHUNK, D_SSD), F32)]
        + [pltpu.VMEM((SSD_STATE, D_SSD // SSD_GROUPS), F32)] * SSD_GROUPS,
        compiler_params=_cparams(("arbitrary", "arbitrary")),
        name="ssd_mixer",
    )(proj, proj, proj, small, conv_w, conv_b.reshape(1, -1), shift, dtb, alog, dskip, norm_w.reshape(1, -1), expand)


def _mlstm_kernel(q_ref, k_ref, v0_ref, v1_ref, o0_ref, o1_ref, sm_ref, bias_ref, nw_ref, *rest, prep_gate_up):
    L = CHUNK
    heads_per_ref = COL_W // ML_DV

    def head_cols(refs, h):
        c0 = (h % heads_per_ref) * ML_DV
        return refs[h // heads_per_ref][0, :, c0:c0 + ML_DV]

    if prep_gate_up:
        wg_ref, perm_ref, out_ref, glu_ref, lin_ref, n_scr, m_scr, *ct_scrs = rest
        _deinterleave_tile(wg_ref, perm_ref, glu_ref, lin_ref)
    else:
        out_ref, n_scr, m_scr, *ct_scrs = rest

    @pl.when(pl.program_id(1) == 0)
    def _():
        for ct_scr in ct_scrs:
            ct_scr[...] = jnp.zeros_like(ct_scr)
        n_scr[...] = jnp.zeros_like(n_scr)
        m_scr[...] = jnp.zeros_like(m_scr)

    n_all = n_scr[...]
    m_all = m_scr[...]
    n_new_all = n_all
    m_new_all = m_all
    head_row = lax.broadcasted_iota(I32, (ML_HEADS, LANES), 0)

    g = sm_ref[0] + bias_ref[...]
    log_f = jnp.minimum(g, 0.0) - jnp.log1p(jnp.exp(-jnp.abs(g)))
    tri = _tril(L).astype(F32)
    bcum = _dot_exact(tri, log_f)
    bcum_t = bcum.T
    keys_before = jnp.logical_not(_tril(L, strict=True))
    row16 = lax.broadcasted_iota(I32, (2 * SUBLANES, ML_DQK), 0)
    scale = ML_DQK ** -0.5
    nt = (((1,), (1,)), ((), ()))
    tn = (((0,), (0,)), ((), ()))
    for h in range(ML_HEADS):
        li, lf = LANE_I + h, LANE_F + h
        li_col, b_col, b_row = g[:, li:li + 1], bcum[:, lf:lf + 1], bcum_t[lf:lf + 1, :]
        btot = bcum[L - 1:L, lf:lf + 1]
        m_prev = m_all[h:h + 1, 0:1]
        ct_prev = ct_scrs[h][...]
        n_prev = n_all[h:h + 1, :]
        qb = q_ref[0, :, h * ML_DQK:(h + 1) * ML_DQK]
        kb = k_ref[0, :, h * ML_DQK:(h + 1) * ML_DQK]
        vb = head_cols((v0_ref, v1_ref), h)
        w_end = btot - b_col + li_col
        m_loc = jnp.max(w_end, axis=0, keepdims=True)
        k_end = kb.astype(F32) * jnp.exp(w_end - m_loc)
        kv_t = lax.dot_general(vb, k_end.astype(BF16), tn, preferred_element_type=F32)
        nk = jnp.sum(k_end, axis=0, keepdims=True)
        d_log = jnp.where(keys_before, (li_col - b_col) + b_row, -jnp.inf)
        m_inter = b_row + m_prev
        m_t = jnp.maximum(m_inter, jnp.max(d_log, axis=0, keepdims=True))
        s_t = lax.dot_general(kb, qb, nt, preferred_element_type=F32)
        s_t = s_t * (scale * jnp.exp(d_log - m_t))
        inter = scale * jnp.exp(m_inter - m_t)
        num_t = (lax.dot_general(vb, s_t.astype(BF16), tn, preferred_element_type=F32)
                 + lax.dot_general(ct_prev.astype(BF16), qb, nt, preferred_element_type=F32) * inter)
        n_hi = n_prev.astype(BF16).astype(F32)
        n_rows = jnp.where(row16 == 0, n_hi, jnp.where(row16 == 1, n_prev - n_hi, 0.0)).astype(BF16)
        qn = lax.dot_general(n_rows, qb, nt, preferred_element_type=F32)
        den = jnp.sum(s_t, axis=0, keepdims=True) + (qn[0:1, :] + qn[1:2, :]) * inter
        den = jnp.maximum(jnp.abs(den), jnp.exp(-m_t))
        ht_t = num_t / den
        ht_t = ht_t * lax.rsqrt(jnp.mean(ht_t * ht_t, axis=0, keepdims=True) + EPS)
        vs = slice(h * ML_DV, (h + 1) * ML_DV)
        og = _sigmoid(head_cols((o0_ref, o1_ref), h).astype(F32))
        out_ref[0, :, vs] = (ht_t.T * nw_ref[:, vs] * og).astype(BF16)
        m_new = jnp.maximum(btot + m_prev, m_loc)
        fa = jnp.exp(btot + m_prev - m_new)
        fg = jnp.exp(m_loc - m_new)
        ct_scrs[h][...] = fa * ct_prev + fg * kv_t
        n_new_all = jnp.where(head_row == h, fa * n_prev + fg * nk, n_new_all)
        m_new_all = jnp.where(head_row == h, m_new, m_new_all)
    n_scr[...] = n_new_all
    m_scr[...] = m_new_all


def _mlstm_mixer(proj, small, i_bias, f_bias, norm_w, b, s, w_gate_up):
    nc = s // CHUNK
    bias = jnp.zeros((LANES,), F32).at[LANE_I:LANE_I + ML_HEADS].set(i_bias).at[LANE_F:LANE_F + ML_HEADS].set(f_bias)
    d_qk = ML_HEADS * ML_DQK
    const = lambda shape: pl.BlockSpec(shape, lambda i, j: (0,) * len(shape))
    col = lambda width, off: pl.BlockSpec((1, CHUNK, width), lambda i, j: (i, j, off // width))
    e, _, n = w_gate_up.shape
    tiles_per_expert = n // PREP_TN
    prep = b * nc == e * tiles_per_expert
    in_specs = [col(d_qk, COL_Q), col(d_qk, COL_K), col(COL_W, COL_V), col(COL_W, COL_V + COL_W),
                col(COL_W, COL_O), col(COL_W, COL_O + COL_W), col(LANES, 0), const((1, LANES)), const((1, D_ML))]
    out_specs = [pl.BlockSpec((1, CHUNK, D_ML), lambda i, j: (i, j, 0))]
    out_shape = [jax.ShapeDtypeStruct((b, s, D_ML), BF16)]
    args = [proj] * 6 + [small, bias.reshape(1, LANES), norm_w.reshape(1, -1)]
    if prep:
        w_in_specs, w_out_specs, w_out_shape = _deinterleave_specs(
            w_gate_up, lambda i, j: ((i * nc + j) // tiles_per_expert, (i * nc + j) % tiles_per_expert))
        in_specs += w_in_specs
        out_specs += w_out_specs
        out_shape += w_out_shape
        args += [w_gate_up, _deinterleave_perm()]
    outs = pl.pallas_call(
        functools.partial(_mlstm_kernel, prep_gate_up=prep),
        grid=(b, nc),
        in_specs=in_specs,
        out_specs=out_specs,
        out_shape=out_shape,
        scratch_shapes=[pltpu.VMEM((ML_HEADS, ML_DQK), F32), pltpu.VMEM((ML_HEADS, LANES), F32)]
        + [pltpu.VMEM((ML_DV, ML_DQK), F32)] * ML_HEADS,
        compiler_params=_cparams(("arbitrary", "arbitrary")),
        name="mlstm_mixer",
    )(*args)
    if prep:
        return outs[0], outs[1], outs[2]
    return (outs[0], *_deinterleave_gate_up(w_gate_up))


def _outproj_router_kernel(x_ref, ys_ref, ym_ref, wo_ref, n2w_ref, rw_ref, rb_ref,
                           h1_ref, xnp_ref, seli_ref, selg_ref, cnt_ref, carry):
    tm = x_ref.shape[0]

    @pl.when(pl.program_id(0) == 0)
    def _():
        carry[...] = jnp.zeros_like(carry)

    h1 = x_ref[...] + _dot(ys_ref[...], wo_ref[0:D_SSD, :]) + _dot(ym_ref[...], wo_ref[D_SSD:, :])
    h1_ref[...] = h1
    xn = h1 * lax.rsqrt(jnp.mean(h1 * h1, axis=-1, keepdims=True) + EPS) * n2w_ref[...]
    packed = _pack_bf16_pairs(xn)
    for sub in range(X_SUB):
        xnp_ref[pl.ds(sub, tm, stride=X_SUB), :] = packed[:, sub * LANES:(sub + 1) * LANES]

    lane = lax.broadcasted_iota(I32, (tm, LANES), 1)
    lane_f = lane.astype(F32)
    xn_hi = xn.astype(BF16)
    xn_lo = (xn - xn_hi.astype(F32)).astype(BF16)
    hh_hl = _dot(xn_hi, rw_ref[...])
    logits = hh_hl[:, :LANES] + hh_hl[:, LANES:] + _dot(xn_lo, rw_ref[:, :LANES]) + rb_ref[...]
    work = jnp.where(lane < N_EXPERTS, logits, -jnp.inf)
    vals, idxs = [], []
    chosen = jnp.zeros((tm, LANES), jnp.bool_)
    for _ in range(TOP_K):
        m = jnp.max(work, axis=1, keepdims=True)
        idx = jnp.min(jnp.where(work == m, lane_f, float(LANES)), axis=1, keepdims=True)
        sel = lane_f == idx
        vals.append(m)
        idxs.append(idx)
        chosen = jnp.logical_or(chosen, sel)
        work = jnp.where(sel, -jnp.inf, work)
    exps = [jnp.exp(v - vals[0]) for v in vals]
    denom = exps[0] + exps[1] + exps[2] + exps[3]

    onehot = jnp.where(chosen, 1.0, 0.0)
    ranks = _dot(_tril(tm, strict=True).astype(BF16), onehot.astype(BF16)) + carry[...]
    total = carry[...] + jnp.sum(onehot, axis=0, keepdims=True)
    carry[...] = total
    cnt_ref[...] = jnp.broadcast_to(total, cnt_ref.shape)

    seli = jnp.zeros((tm, LANES), F32)
    selg = jnp.zeros((tm, LANES), F32)
    for k in range(TOP_K):
        rank_k = jnp.sum(jnp.where(lane_f == idxs[k], ranks, 0.0), axis=1, keepdims=True)
        seli = jnp.where(lane == k, idxs[k], seli)
        seli = jnp.where(lane == TOP_K + k, rank_k, seli)
        selg = jnp.where(lane == k, exps[k] / denom, selg)
    seli_ref[...] = seli.astype(I32)
    selg_ref[...] = selg


def _outproj_router(xf, y_ssd, y_ml, w_out_b, norm2_w, router_w_pad, router_b_pad):
    t = xf.shape[0]
    tm = OUT_TM
    row = lambda width: pl.BlockSpec((tm, width), lambda i: (i, 0))
    const = lambda shape: pl.BlockSpec(shape, lambda i: (0,) * len(shape))
    return pl.pallas_call(
        _outproj_router_kernel,
        grid=(t // tm,),
        in_specs=[row(D_MODEL), row(D_SSD), row(D_ML),
                  pl.BlockSpec((D_SSD + D_ML, D_MODEL), lambda i: (0, 0), pipeline_mode=pl.Buffered(1)),
                  const((1, D_MODEL)),
                  const((D_MODEL, 2 * LANES)), const((1, LANES))],
        out_specs=[row(D_MODEL), pl.BlockSpec((tm * X_SUB, LANES), lambda i: (i, 0)), row(LANES), row(LANES),
                   const((SUBLANES, LANES))],
        out_shape=[jax.ShapeDtypeStruct((t, D_MODEL), F32),
                   jax.ShapeDtypeStruct((t * X_SUB, LANES), U32),
                   jax.ShapeDtypeStruct((t, LANES), I32),
                   jax.ShapeDtypeStruct((t, LANES), F32),
                   jax.ShapeDtypeStruct((SUBLANES, LANES), F32)],
        scratch_shapes=[pltpu.VMEM((1, LANES), F32)],
        compiler_params=_cparams(("arbitrary",)),
        name="outproj_router",
    )(xf, y_ssd, y_ml, w_out_b, norm2_w, router_w_pad, router_b_pad)


def _route_kernel(seli_ref, cnt_ref, dest_ref, be_ref):
    tm = seli_ref.shape[0]
    nbp = be_ref.shape[0]
    lane1 = lax.broadcasted_iota(I32, (1, LANES), 1)
    cnt = jnp.where(lane1 < N_EXPERTS, cnt_ref[0:1, :], 0.0)
    padded = jnp.ceil(cnt / MOE_TM) * MOE_TM
    r = lax.broadcasted_iota(I32, (LANES, LANES), 0)
    c = lax.broadcasted_iota(I32, (LANES, LANES), 1)
    upper = jnp.where(r <= c, 1.0, 0.0)
    pend = jnp.round(_dot_exact(jnp.broadcast_to(padded, (SUBLANES, LANES)), upper))[0:1, :]
    pstart = pend - padded

    lane = lax.broadcasted_iota(I32, (tm, LANES), 1)
    lane_f = lane.astype(F32)
    seli = seli_ref[...].astype(F32)
    dest = jnp.zeros((tm, LANES), F32)
    for k in range(TOP_K):
        idx_k = jnp.sum(jnp.where(lane == k, seli, 0.0), axis=1, keepdims=True)
        rank_k = jnp.sum(jnp.where(lane == TOP_K + k, seli, 0.0), axis=1, keepdims=True)
        start_k = jnp.sum(jnp.where(lane_f == idx_k, pstart, 0.0), axis=1, keepdims=True)
        dest = jnp.where(lane == k, start_k + rank_k, dest)
    dest_ref[...] = dest.astype(I32)

    blk_start = (lax.broadcasted_iota(I32, (nbp, LANES), 0) * MOE_TM).astype(F32)
    lane_b = lax.broadcasted_iota(I32, (nbp, LANES), 1)
    passed = jnp.where(jnp.logical_and(lane_b < N_EXPERTS, pend <= blk_start), 1.0, 0.0)
    expert = jnp.minimum(jnp.sum(passed, axis=1, keepdims=True), float(N_EXPERTS - 1))
    n_active = pend[:, N_EXPERTS - 1:N_EXPERTS] / MOE_TM
    tok_end = jnp.sum(jnp.where(lane_b.astype(F32) == expert, pstart + cnt, 0.0), axis=1, keepdims=True)
    valid = jnp.clip(tok_end - blk_start[:, 0:1], 0.0, float(MOE_TM))
    n_sub = jnp.ceil(valid / MOE_SB)
    own = lane_b == lax.broadcasted_iota(I32, (nbp, LANES), 0)
    pad_start = jnp.sum(jnp.where(own, pstart + cnt, 0.0), axis=1, keepdims=True)
    pad_cnt = jnp.sum(jnp.where(own, padded - cnt, 0.0), axis=1, keepdims=True)
    cols = (expert, n_active, n_sub, pad_start, pad_cnt, pend[:, N_EXPERTS - 1:N_EXPERTS])
    be = jnp.zeros((nbp, LANES), F32)
    for j, v in enumerate(cols):
        be = jnp.where(lane_b == j, v, be)
    be_ref[...] = be.astype(I32)


def _route_offsets(seli, cnt, n_blocks):
    t = seli.shape[0]
    nbp = -(-n_blocks // SUBLANES) * SUBLANES
    return pl.pallas_call(
        _route_kernel,
        grid=(t // ROUTE_TM,),
        in_specs=[pl.BlockSpec((ROUTE_TM, LANES), lambda i: (i, 0)),
                  pl.BlockSpec((SUBLANES, LANES), lambda i: (0, 0))],
        out_specs=[pl.BlockSpec((ROUTE_TM, LANES), lambda i: (i, 0)),
                   pl.BlockSpec((nbp, LANES), lambda i: (0, 0))],
        out_shape=[jax.ShapeDtypeStruct((t, LANES), I32),
                   jax.ShapeDtypeStruct((nbp, LANES), I32)],
        compiler_params=_cparams(("arbitrary",)),
        name="route_offsets",
    )(seli, cnt)


def _dispatch_kernel(pad_start_ref, pad_cnt_ref, tail_ref, dest_ref, xnp_ref, rows_ref, zbuf, sem, zsem, *, n_zero):
    tt = xnp_ref.shape[0] // X_SUB
    n_rows = rows_ref.shape[0] // X_SUB
    first = pl.program_id(0) == 0

    def zero_rows(start_row, n):
        dst = rows_ref.at[pl.ds(pl.multiple_of(start_row * X_SUB, X_SUB), n * X_SUB)]
        return pltpu.make_async_copy(zbuf.at[pl.ds(0, n * X_SUB)], dst, zsem)

    @pl.when(first)
    def _():
        zbuf[...] = jnp.zeros_like(zbuf)

        def expert_padding(e, carry):
            start, count = pad_start_ref[e], pad_cnt_ref[e]
            for k in reversed(range(MOE_TM.bit_length() - 1)):
                run = 1 << k
                pl.when((count & run) != 0)(
                    lambda run=run: zero_rows(start + (count & ~(2 * run - 1)), run).start())
            return carry

        lax.fori_loop(0, N_EXPERTS, expert_padding, 0)

        def trailing_block(j, carry):
            start = tail_ref[0] + j * MOE_TM
            pl.when(start < n_rows)(lambda: zero_rows(start, MOE_TM).start())
            return carry

        lax.fori_loop(0, N_EXPERTS, trailing_block, 0)

    def tile_copy(t, d):
        src = xnp_ref.at[pl.ds(pl.multiple_of(t * X_SUB, X_SUB), X_SUB)]
        dst = rows_ref.at[pl.ds(pl.multiple_of(d * X_SUB, X_SUB), X_SUB)]
        return pltpu.make_async_copy(src, dst, sem)

    def issue(t, carry):
        for k in range(TOP_K):
            tile_copy(t, dest_ref[t * TOP_K + k]).start(priority=k % 2)
        return carry

    lax.fori_loop(0, tt, issue, 0, unroll=4)
    for _ in range(TOP_K):
        pltpu.make_async_copy(xnp_ref, rows_ref.at[pl.ds(0, tt * X_SUB)], sem).wait()

    @pl.when(first)
    def _():
        span = rows_ref.at[pl.ds(0, n_zero * X_SUB)]
        pltpu.make_async_copy(span, span, zsem).wait()


def _dispatch(pad_start, pad_cnt, tail_start, dest_flat, xnp, n_rows):
    t = xnp.shape[0] // X_SUB
    grid_spec = pltpu.PrefetchScalarGridSpec(
        num_scalar_prefetch=3,
        grid=(t // DISP_TT,),
        in_specs=[pl.BlockSpec((DISP_TT * TOP_K,), lambda i, *_: (i,), memory_space=pltpu.SMEM),
                  pl.BlockSpec((DISP_TT * X_SUB, LANES), lambda i, *_: (i, 0))],
        out_specs=pl.BlockSpec(memory_space=pl.ANY),
        scratch_shapes=[pltpu.VMEM((MOE_TM * X_SUB, LANES), U32), pltpu.SemaphoreType.DMA(()),
                        pltpu.SemaphoreType.DMA(())],
    )
    return pl.pallas_call(
        functools.partial(_dispatch_kernel, n_zero=n_rows - t * TOP_K),
        grid_spec=grid_spec,
        out_shape=jax.ShapeDtypeStruct((n_rows * X_SUB, LANES), U32),
        compiler_params=_cparams(("arbitrary",)),
        name="moe_dispatch",
    )(pad_start, pad_cnt, tail_start, dest_flat, xnp)


def _moe_kernel(be_ref, nsb_ref, na_ref, x_ref, wglu_ref, wlin_ref, bglu_ref, blin_ref, wd_ref, bd_ref, o_ref,
                xb_next, xb_cur, acc):
    m = pl.program_id(0)
    f = pl.program_id(1)
    nf = pl.num_programs(1)
    n_sub = MOE_TM // MOE_SB

    def ffn(nsb, first, last):
        rows = nsb * MOE_SB
        if first:
            @pl.when(m == 0)
            def _():
                _unpack_bf16_pairs(x_ref, MOE_TM, xb_next)

        xb = xb_next if first else xb_cur
        if last:
            _unpack_bf16_pairs(x_ref, MOE_TM, xb_next)
        wd = wd_ref[0].astype(BF16)
        xi = xb[0:rows, :]
        if first:
            xb_cur[0:rows, :] = xi
        glu = jnp.minimum(_dot(xi, wglu_ref[0]) + bglu_ref[0], SWIGLU_LIMIT)
        lin = jnp.clip(_dot(xi, wlin_ref[0]) + blin_ref[0], -SWIGLU_LIMIT, SWIGLU_LIMIT)
        act = glu * _sigmoid(SWIGLU_ALPHA * glu) * (lin + 1.0)
        part = _dot(act.astype(BF16), wd)
        acc[0:rows, :] = (bd_ref[0] + part) if first else (acc[0:rows, :] + part)

        if last:
            packed = _pack_bf16_pairs(acc[0:rows, :])
            for sub in range(X_SUB):
                o_ref[pl.ds(sub, rows, stride=X_SUB), :] = packed[:, sub * LANES:(sub + 1) * LANES]
            if nsb < n_sub:
                o_ref[rows * X_SUB:, :] = jnp.zeros(((MOE_TM - rows) * X_SUB, LANES), U32)

    active = m < na_ref[0]
    kinds = [(True, False, f == 0), (False, True, f == nf - 1)]
    for nsb in range(1, n_sub + 1):
        for first, last, when in kinds:
            cond = jnp.logical_and(jnp.logical_and(active, nsb_ref[m] == nsb), when)
            pl.when(cond)(functools.partial(ffn, nsb, first, last))

    @pl.when(jnp.logical_and(jnp.logical_not(active), f == nf - 1))
    def _():
        o_ref[...] = jnp.zeros_like(o_ref)


def _moe_ffn(block_expert, block_nsub, n_active, x_rows, w_glu, w_lin, b_glu, b_lin, w_down, b_down, n_blocks):
    nf = D_FF // MOE_TF
    assert nf == 2, "the kernel has a first-step and a last-step body and no middle one"

    def blk(m, na):
        return jnp.maximum(jnp.minimum(m, na[0] - 1), 0)

    def ftile(m, f, na):
        return jnp.where(m < na[0], f, nf - 1)

    def expert(m, be, na):
        return be[blk(m, na)]

    col_tile = lambda m, f, be, ns, na: (expert(m, be, na), 0, ftile(m, f, na))
    grid_spec = pltpu.PrefetchScalarGridSpec(
        num_scalar_prefetch=3,
        grid=(n_blocks, nf),
        in_specs=[pl.BlockSpec((MOE_TM * X_SUB, LANES),
                               lambda m, f, be, ns, na: (blk(jnp.where(f == nf - 1, m + 1, m), na), 0)),
                  pl.BlockSpec((1, D_MODEL, MOE_TF), col_tile),
                  pl.BlockSpec((1, D_MODEL, MOE_TF), col_tile),
                  pl.BlockSpec((1, 1, MOE_TF), col_tile),
                  pl.BlockSpec((1, 1, MOE_TF), col_tile),
                  pl.BlockSpec((1, MOE_TF, D_MODEL), lambda m, f, be, ns, na: (expert(m, be, na), ftile(m, f, na), 0)),
                  pl.BlockSpec((1, 1, D_MODEL), lambda m, f, be, ns, na: (expert(m, be, na), 0, 0))],
        out_specs=pl.BlockSpec((MOE_TM * X_SUB, LANES), lambda m, f, be, ns, na: (m, 0)),
        scratch_shapes=[pltpu.VMEM((MOE_TM, D_MODEL), BF16)] * 2 + [pltpu.VMEM((MOE_TM, D_MODEL), F32)],
    )
    return pl.pallas_call(
        _moe_kernel,
        grid_spec=grid_spec,
        out_shape=jax.ShapeDtypeStruct((n_blocks * MOE_TM * X_SUB, LANES), U32),
        compiler_params=_cparams(("arbitrary", "arbitrary")),
        name="moe_ffn",
    )(block_expert, block_nsub, n_active, x_rows, w_glu, w_lin, b_glu, b_lin, w_down, b_down)


def _combine_kernel(dest_ref, dest_next_ref, h1_ref, selg_ref, fw_ref, rows_ref, o_ref, buf, h_scr, sem):
    tt = h1_ref.shape[0]
    i = pl.program_id(0)
    n = pl.num_programs(0)
    half = D_MODEL // 2

    def row_copy(slot, t, k, d):
        src = rows_ref.at[pl.ds(pl.multiple_of(d * X_SUB, X_SUB), X_SUB)]
        dst = buf.at[slot, k, pl.ds(pl.multiple_of(t * X_SUB, X_SUB), X_SUB)]
        return pltpu.make_async_copy(src, dst, sem.at[slot])

    def gather(slot, d_ref):
        def issue(t, carry):
            for k in range(TOP_K):
                row_copy(slot, t, k, d_ref[t * TOP_K + k]).start(priority=k % 2)
            return carry

        lax.fori_loop(0, tt, issue, 0, unroll=4)

    slot = i % 2

    @pl.when(i == 0)
    def _():
        gather(0, dest_ref)

    @pl.when(i + 1 < n)
    def _():
        gather(1 - slot, dest_next_ref)

    for k in range(TOP_K):
        pltpu.make_async_copy(rows_ref.at[pl.ds(0, tt * X_SUB)], buf.at[slot, k], sem.at[slot]).wait()

    gates = selg_ref[...]
    for sub in range(X_SUB):
        lo_cs = slice(sub * LANES, (sub + 1) * LANES)
        hi_cs = slice(half + sub * LANES, half + (sub + 1) * LANES)
        h_lo = h1_ref[:, lo_cs]
        h_hi = h1_ref[:, hi_cs]
        for k in range(TOP_K):
            u = buf[slot, k, pl.ds(sub, tt, stride=X_SUB), :]
            g = gates[:, k:k + 1]
            h_lo = h_lo + g * pltpu.bitcast(u << 16, F32)
            h_hi = h_hi + g * pltpu.bitcast(u & jnp.uint32(0xFFFF0000), F32)
        h_scr[:, lo_cs] = h_lo
        h_scr[:, hi_cs] = h_hi
    h = h_scr[...]
    o_ref[...] = h * lax.rsqrt(jnp.mean(h * h, axis=-1, keepdims=True) + EPS) * fw_ref[...]


def _combine(dest_flat, h1, selg, final_w, y_rows):
    t = h1.shape[0]
    tt = COMB_TT
    n = t // tt
    return pl.pallas_call(
        _combine_kernel,
        grid=(n,),
        in_specs=[pl.BlockSpec((tt * TOP_K,), lambda i: (i,), memory_space=pltpu.SMEM),
                  pl.BlockSpec((tt * TOP_K,), lambda i: (jnp.minimum(i + 1, n - 1),), memory_space=pltpu.SMEM),
                  pl.BlockSpec((tt, D_MODEL), lambda i: (i, 0)),
                  pl.BlockSpec((tt, LANES), lambda i: (i, 0)),
                  pl.BlockSpec((1, D_MODEL), lambda i: (0, 0)),
                  pl.BlockSpec(memory_space=pl.ANY)],
        out_specs=pl.BlockSpec((tt, D_MODEL), lambda i: (i, 0)),
        out_shape=jax.ShapeDtypeStruct((t, D_MODEL), F32),
        scratch_shapes=[pltpu.VMEM((2, TOP_K, tt * X_SUB, LANES), U32), pltpu.VMEM((tt, D_MODEL), F32),
                        pltpu.SemaphoreType.DMA((2,))],
        compiler_params=_cparams(("arbitrary",)),
        name="moe_combine",
    )(dest_flat, dest_flat, h1, selg, final_w, y_rows)


def kernel(x, norm1_w, w_in, conv_w, conv_b, dt_bias, a_log, d_skip, ssd_norm_w, i_bias, f_bias, ml_norm_w,
           w_out, norm2_w, router_w, router_b, w_gate_up, b_gate_up, w_down, b_down, final_norm_w):
    b, s, d = x.shape
    t = b * s
    assert w_in.shape[0] == 1, "single-layer problem"
    h = x.reshape(t, d)
    for l in range(1):
        wi = w_in[l]
        o_z, o_xbc, o_dt = 0, D_SSD, D_SSD + D_SSD + 2 * SSD_GROUPS * SSD_STATE
        o_q = o_dt + SSD_HEADS
        o_k = o_q + ML_HEADS * ML_DQK
        o_v = o_k + ML_HEADS * ML_DQK
        o_o = o_v + D_ML
        o_i = o_o + D_ML
        o_f = o_i + ML_HEADS
        w_big = jnp.concatenate([wi[:, o_z:o_dt], wi[:, o_q:o_i]], axis=1).astype(BF16)
        w_small = jnp.concatenate(
            [wi[:, o_dt:o_q], wi[:, o_i:o_f], wi[:, o_f:o_f + ML_HEADS],
             jnp.zeros((d, LANES - SSD_HEADS - 2 * ML_HEADS), F32)], axis=1).astype(BF16)

        proj, small = _input_projection(h, norm1_w[l].reshape(1, d), w_big, w_small)
        proj3 = proj.reshape(b, s, D_BIG)
        small3 = small.reshape(b, s, LANES)
        y_ssd = _ssd_mixer(proj3, small3, conv_w[l], conv_b[l], dt_bias[l], a_log[l], d_skip[l], ssd_norm_w[l], b, s)
        y_ml, w_glu, w_lin = _mlstm_mixer(proj3, small3, i_bias[l], f_bias[l], ml_norm_w[l], b, s, w_gate_up[l])

        rw = jnp.pad(router_w[l], ((0, 0), (0, LANES - N_EXPERTS)))
        rw_hi = rw.astype(BF16)
        rw = jnp.concatenate([rw_hi, (rw - rw_hi.astype(F32)).astype(BF16)], axis=1)
        rb = jnp.pad(router_b[l], (0, LANES - N_EXPERTS)).reshape(1, LANES)
        h1, xnp, seli, selg, cnt = _outproj_router(
            h, y_ssd.reshape(t, D_SSD), y_ml.reshape(t, D_ML), w_out[l].astype(BF16), norm2_w[l].reshape(1, d), rw, rb)

        n_blocks = (t * TOP_K) // MOE_TM + N_EXPERTS
        dest, table = _route_offsets(seli, cnt, n_blocks)
        dest_flat = dest[:, :TOP_K].reshape(-1)
        block_expert = table[:n_blocks, 0]
        n_active = table[0:1, 1]
        block_nsub = table[:n_blocks, 2]

        x_rows = _dispatch(table[:N_EXPERTS, 3], table[:N_EXPERTS, 4], table[0:1, 5], dest_flat, xnp,
                           n_blocks * MOE_TM)
        b_glu = b_gate_up[l][:, 0::2].reshape(N_EXPERTS, 1, D_FF)
        b_lin = b_gate_up[l][:, 1::2].reshape(N_EXPERTS, 1, D_FF)
        y_rows = _moe_ffn(block_expert, block_nsub, n_active, x_rows, w_glu, w_lin, b_glu, b_lin, w_down[l],
                          b_down[l].reshape(N_EXPERTS, 1, d), n_blocks)
        h = _combine(dest_flat, h1, selg, final_norm_w.reshape(1, d), y_rows)
    return h.reshape(b, s, d)
```
